```python
import math
import jax, jax.numpy as jnp
from jax import lax
import numpy as np

D_MODEL = 2048
BATCH = 2
SEQ = 16384
DEPTH = 2

PLE_DIM = 256
EPS = 1e-6
BLOCK = 128
MLA_HEADS = 8
MLA_NOPE = 128
MLA_ROPE = 64
MLA_V = 128
Q_RANK = 384
KV_RANK = 256
ROPE_THETA = 10000.0
SWA_HEADS = 8
SWA_KV_HEADS = 2
SWA_GROUP = SWA_HEADS // SWA_KV_HEADS
HEAD_DIM = 128
WINDOW = 128
NUM_BUCKETS = 32
T5_MAX_DISTANCE = 128
D_FF = 5504
CONV_W = 3
MLA_WIDTH = MLA_HEADS * MLA_V
SWA_WIDTH = SWA_HEADS * HEAD_DIM
MIX_WIDTH = MLA_WIDTH + SWA_WIDTH
IN_SPLITS = (Q_RANK, KV_RANK, MLA_ROPE, SWA_HEADS * HEAD_DIM, SWA_KV_HEADS * HEAD_DIM, SWA_KV_HEADS * HEAD_DIM)
IN_WIDTH = sum(IN_SPLITS)
IN_OFFSETS = tuple(int(v) for v in np.cumsum(IN_SPLITS)[:-1])
NEG = -1e30

kernel_name = "hybrid_mla_swa_convglu_encoder"


def rmsnorm(x, g):
    x32 = x.astype(jnp.float32)
    y = x32 * lax.rsqrt(jnp.mean(x32 * x32, axis=-1, keepdims=True) + EPS)
    return (y * g.astype(jnp.float32)).astype(x.dtype)


def rope_tables(positions):
    inv_freq = ROPE_THETA ** (-jnp.arange(0, MLA_ROPE, 2, dtype=jnp.float32) / MLA_ROPE)
    ang = positions.astype(jnp.float32)[..., None] * inv_freq
    return jnp.cos(ang), jnp.sin(ang)


def apply_rope(t, cos, sin):
    t32 = t.astype(jnp.float32)
    t1, t2 = t32[..., : MLA_ROPE // 2], t32[..., MLA_ROPE // 2:]
    return jnp.concatenate([t1 * cos - t2 * sin, t1 * sin + t2 * cos], axis=-1).astype(t.dtype)


def t5_bucket(rel):
    half = NUM_BUCKETS // 2
    max_exact = half // 2
    ret = jnp.where(rel > 0, half, 0)
    n = jnp.abs(rel)
    nf = jnp.maximum(n, max_exact).astype(jnp.float32)
    large = max_exact + (jnp.log(nf / max_exact) / math.log(T5_MAX_DISTANCE / max_exact)
                         * (half - max_exact)).astype(jnp.int32)
    large = jnp.minimum(large, half - 1)
    return ret + jnp.where(n < max_exact, n, large)


def mla_attention(z_cq, z_ckv, z_kr, cq_norm, ckv_norm, w_uq, w_ukv, cos, sin):
    B, S = z_cq.shape[:2]
    nb = S // BLOCK
    q = (rmsnorm(z_cq, cq_norm) @ w_uq).reshape(B, S, MLA_HEADS, MLA_NOPE + MLA_ROPE)
    q_nope = q[..., :MLA_NOPE]
    q_rope = apply_rope(q[..., MLA_NOPE:], cos[:, :, None], sin[:, :, None])
    kv = (rmsnorm(z_ckv, ckv_norm) @ w_ukv).reshape(B, S, MLA_HEADS, MLA_NOPE + MLA_V)
    k_nope, v = kv[..., :MLA_NOPE], kv[..., MLA_NOPE:]
    k_rope = apply_rope(z_kr, cos, sin)
    scale = 1.0 / math.sqrt(MLA_NOPE + MLA_ROPE)
    qn_b = jnp.moveaxis(q_nope.reshape(B, nb, BLOCK, MLA_HEADS, MLA_NOPE), 1, 0)
    qr_b = jnp.moveaxis(q_rope.reshape(B, nb, BLOCK, MLA_HEADS, MLA_ROPE), 1, 0)

    def one_block(args):
        qn, qr = args
        s = (jnp.einsum('bqhd,bkhd->bhqk', qn, k_nope)
             + jnp.einsum('bqhd,bkd->bhqk', qr, k_rope)).astype(jnp.float32) * scale
        pr = jax.nn.softmax(s, axis=-1).astype(v.dtype)
        return jnp.einsum('bhqk,bkhd->bqhd', pr, v)

    o = lax.map(one_block, (qn_b, qr_b))
    return jnp.moveaxis(o, 0, 1).reshape(B, S, MLA_WIDTH)


def _neighbours(t):
    pad = [(0, 0), (1, 1)] + [(0, 0)] * (t.ndim - 2)
    tp = jnp.pad(t, pad)
    return jnp.concatenate([tp[:, :-2], tp[:, 1:-1], tp[:, 2:]], axis=2)


def swa_attention(q, k, v, positions, sink, t5_bias):
    B, S = q.shape[:2]
    nb = S // BLOCK
    qb = q.reshape(B, nb, BLOCK, SWA_KV_HEADS, SWA_GROUP, HEAD_DIM)
    kn = _neighbours(k.reshape(B, nb, BLOCK, SWA_KV_HEADS, HEAD_DIM))
    vn = _neighbours(v.reshape(B, nb, BLOCK, SWA_KV_HEADS, HEAD_DIM))
    pq = positions.reshape(B, nb, BLOCK)
    pk = _neighbours(pq)
    kidx = (jnp.arange(nb)[:, None] - 1) * BLOCK + jnp.arange(3 * BLOCK)[None, :]
    valid = (kidx >= 0) & (kidx < S)
    rel = pk[:, :, None, :] - pq[:, :, :, None]
    mask = (jnp.abs(rel) <= WINDOW) & valid[None, :, None, :]
    bias = jnp.take(t5_bias.T, t5_bucket(rel), axis=1)
    bias = bias.reshape(SWA_KV_HEADS, SWA_GROUP, B, nb, BLOCK, 3 * BLOCK).transpose(2, 3, 0, 1, 4, 5)
    scale = 1.0 / math.sqrt(HEAD_DIM)
    s = jnp.einsum('bnqhgd,bnkhd->bnhgqk', qb, kn).astype(jnp.float32) * scale + bias.astype(jnp.float32)
    s = jnp.where(mask[:, :, None, None], s, NEG)
    sk = sink.astype(jnp.float32).reshape(SWA_KV_HEADS, SWA_GROUP)[None, None, :, :, None, None]
    m = jnp.maximum(jnp.max(s, axis=-1, keepdims=True), sk)
    e = jnp.exp(s - m)
    pr = e / (jnp.sum(e, axis=-1, keepdims=True) + jnp.exp(sk - m))
    o = jnp.einsum('bnhgqk,bnkhd->bnqhgd', pr.astype(v.dtype), vn)
    return o.reshape(B, S, SWA_WIDTH)


def depthwise_conv3(t, w, b):
    tp = jnp.pad(t, ((0, 0), (1, 1), (0, 0)))
    return tp[:, :-2] * w[0] + tp[:, 1:-1] * w[1] + tp[:, 2:] * w[2] + b


def setup_inputs(seed: int = 0) -> dict:
    key = jax.random.key(seed)
    ks = iter(jax.random.split(key, 32))
    f32 = jnp.float32

    def nrm(shape, fan_in):
        return jax.random.normal(next(ks), shape, f32) * (fan_in ** -0.5)

    def gain(shape):
        return 1.0 + 0.02 * jax.random.normal(next(ks), shape, f32)

    x = jax.random.normal(next(ks), (BATCH, SEQ, D_MODEL), f32)
    p = jax.random.normal(next(ks), (DEPTH, BATCH, SEQ, PLE_DIM), f32)
    offs = jax.random.randint(next(ks), (BATCH, 1), 0, 1024, dtype=jnp.int32)
    positions = (jnp.arange(SEQ, dtype=jnp.int32)[None, :] + offs).astype(jnp.int32)
    return {
        "x": x,
        "p": p,
        "positions": positions,
        "attn_norm": gain((DEPTH, D_MODEL)),
        "w_in": nrm((DEPTH, D_MODEL, IN_WIDTH), D_MODEL),
        "cq_norm": gain((DEPTH, Q_RANK)),
        "ckv_norm": gain((DEPTH, KV_RANK)),
        "w_uq": nrm((DEPTH, Q_RANK, MLA_HEADS * (MLA_NOPE + MLA_ROPE)), Q_RANK),
        "w_ukv": nrm((DEPTH, KV_RANK, MLA_HEADS * (MLA_NOPE + MLA_V)), KV_RANK),
        "swa_sink": 0.5 * jax.random.normal(next(ks), (DEPTH, SWA_HEADS), f32),
        "t5_bias": 0.5 * jax.random.normal(next(ks), (NUM_BUCKETS, SWA_HEADS), f32),
        "mla_out_norm": gain((DEPTH, MLA_WIDTH)),
        "swa_out_norm": gain((DEPTH, SWA_WIDTH)),
        "w_o": nrm((DEPTH, MIX_WIDTH, D_MODEL), MIX_WIDTH),
        "ffn_norm": gain((DEPTH, D_MODEL)),
        "w_gate": nrm((DEPTH, D_MODEL, D_FF), D_MODEL),
        "w_up": nrm((DEPTH, D_MODEL, D_FF), D_MODEL),
        "conv_w": nrm((DEPTH, CONV_W, D_FF), CONV_W),
        "conv_b": 0.01 * jax.random.normal(next(ks), (DEPTH, D_FF), f32),
        "w_down": nrm((DEPTH, D_FF, D_MODEL), D_FF),
        "ple_gate_w": nrm((DEPTH, D_MODEL, D_MODEL), D_MODEL),
        "ple_gate_b": 0.01 * jax.random.normal(next(ks), (DEPTH, D_MODEL), f32),
        "ple_proj": nrm((DEPTH, PLE_DIM, D_MODEL), PLE_DIM),
        "final_norm": gain((D_MODEL,)),
    }


def reference(x, p, positions, attn_norm, w_in, cq_norm, ckv_norm, w_uq, w_ukv, swa_sink, t5_bias,
              mla_out_norm, swa_out_norm, w_o, ffn_norm, w_gate, w_up, conv_w, conv_b, w_down,
              ple_gate_w, ple_gate_b, ple_proj, final_norm):
    B, S = x.shape[:2]
    cos, sin = rope_tables(positions)
    for i in range(DEPTH):
        h = rmsnorm(x, attn_norm[i])
        z = h @ w_in[i]
        z_cq, z_ckv, z_kr, z_q, z_k, z_v = jnp.split(z, IN_OFFSETS, axis=-1)
        mla_o = mla_attention(z_cq, z_ckv, z_kr, cq_norm[i], ckv_norm[i], w_uq[i], w_ukv[i], cos, sin)
        swa_o = swa_attention(z_q.reshape(B, S, SWA_HEADS, HEAD_DIM),
                              z_k.reshape(B, S, SWA_KV_HEADS, HEAD_DIM),
                              z_v.reshape(B, S, SWA_KV_HEADS, HEAD_DIM),
                              positions, swa_sink[i], t5_bias)
        mixed = jnp.concatenate([rmsnorm(mla_o, mla_out_norm[i]), rmsnorm(swa_o, swa_out_norm[i])], axis=-1)
        x = x + mixed @ w_o[i]
        h2 = rmsnorm(x, ffn_norm[i])
        g = depthwise_conv3(h2 @ w_gate[i], conv_w[i], conv_b[i])
        x = x + (jax.nn.silu(g) * (h2 @ w_up[i])) @ w_down[i]
        x = x + jax.nn.sigmoid(x @ ple_gate_w[i] + ple_gate_b[i]) * (p[i] @ ple_proj[i])
    return rmsnorm(x, final_norm)
```

```python
import functools
import math

import jax
import jax.numpy as jnp
from jax import lax
from jax.experimental import pallas as pl
from jax.experimental.pallas import tpu as pltpu

F32 = jnp.float32
BF16 = jnp.bfloat16

EPS = 1e-6
BLOCK = 128
MLA_HEADS = 8
MLA_NOPE = 128
MLA_ROPE = 64
MLA_V = 128
MLA_QK_PAD = 256
Q_RANK = 384
KV_RANK = 256
ROPE_THETA = 10000.0
SWA_HEADS = 8
SWA_KV_HEADS = 2
SWA_GROUP = SWA_HEADS // SWA_KV_HEADS
HEAD_DIM = 128
WINDOW = 128
NUM_BUCKETS = 32
NEG = -1e30
LOG2E = math.log2(math.e)

V7X_VMEM_BYTES = 64 * 1024 * 1024
VMEM_LIMIT_BYTES = V7X_VMEM_BYTES - 8 * 1024 * 1024
LANES = 128
BF16_SUBLANES = 16

_Z_CQ = 0
_Z_CKV = _Z_CQ + Q_RANK
_Z_KA = _Z_CKV + KV_RANK
_Z_KB = _Z_KA + LANES
_Z_SQ = _Z_KB + LANES
_Z_SK = _Z_SQ + SWA_HEADS * HEAD_DIM
_Z_SV = _Z_SK + SWA_KV_HEADS * HEAD_DIM
_Z_END = _Z_SV + SWA_KV_HEADS * HEAD_DIM


def _params(sem):
    return pltpu.CompilerParams(dimension_semantics=sem, vmem_limit_bytes=VMEM_LIMIT_BYTES)


def _const_spec(shape):
    nd = len(shape)
    return pl.BlockSpec(shape, lambda *_: (0,) * nd, pipeline_mode=pl.Buffered(1))


def _rms(x, g):
    ms = jnp.mean(x * x, axis=-1, keepdims=True)
    return x * lax.rsqrt(ms + EPS) * g


def _sigmoid(x):
    return 1.0 / (1.0 + jnp.exp(-x))


def _in_proj_kernel(x_ref, pos_ref, invf_ref, g_ref, win_ref, cqn_ref, ckvn_ref, wq_ref, wkn_ref,
                    wvt_ref, qt_ref, k_ref, vt_ref, sq_ref, sk_ref, sv_ref, *, q_scale, swa_scale):
    x = x_ref[0]
    h = _rms(x, g_ref[...]).astype(BF16)
    z = jnp.dot(h, win_ref[...], preferred_element_type=F32)
    cq = _rms(z[:, _Z_CQ:_Z_CKV], cqn_ref[...]).astype(BF16)
    ckv = _rms(z[:, _Z_CKV:_Z_KA], ckvn_ref[...]).astype(BF16)

    ang = invf_ref[...] * pos_ref[0].astype(F32)
    cos_t, sin_t = jnp.cos(ang), jnp.sin(ang)
    zpad = jnp.zeros((LANES - MLA_ROPE, ang.shape[1]), F32)
    cc_t = jnp.concatenate([cos_t, cos_t, zpad], axis=0)
    ss_t = jnp.concatenate([-sin_t, sin_t, zpad], axis=0)

    r = lax.dot_general(wq_ref[...], cq, (((1,), (1,)), ((), ())), preferred_element_type=F32)
    for hd in range(MLA_HEADS):
        b = hd * 3 * LANES
        qt_ref[0, hd, 0:LANES, :] = (r[b:b + LANES] * q_scale).astype(BF16)
        roped = r[b + LANES:b + 2 * LANES] * cc_t + r[b + 2 * LANES:b + 3 * LANES] * ss_t
        qt_ref[0, hd, LANES:2 * LANES, :] = (roped * q_scale).astype(BF16)

    knope = jnp.dot(ckv, wkn_ref[...], preferred_element_type=F32)
    cc, ss = cc_t.T, ss_t.T
    krope = (z[:, _Z_KA:_Z_KB] * cc + z[:, _Z_KB:_Z_SQ] * ss).astype(BF16)
    for hd in range(MLA_HEADS):
        k_ref[0, :, hd * MLA_QK_PAD:hd * MLA_QK_PAD + LANES] = knope[:, hd * LANES:(hd + 1) * LANES].astype(BF16)
        k_ref[0, :, hd * MLA_QK_PAD + LANES:(hd + 1) * MLA_QK_PAD] = krope

    vt = lax.dot_general(wvt_ref[...], ckv, (((1,), (1,)), ((), ())), preferred_element_type=F32)
    for hd in range(MLA_HEADS):
        vt_ref[0, hd] = vt[hd * MLA_V:(hd + 1) * MLA_V].astype(BF16)

    sq_ref[0] = (z[:, _Z_SQ:_Z_SK] * swa_scale).astype(BF16)
    sk_ref[0] = z[:, _Z_SK:_Z_SV].astype(BF16)
    sv_ref[0] = z[:, _Z_SV:_Z_END].astype(BF16)


def _in_proj(x, pos_row, invf, g, win, cqn, ckvn, wq, wkn, wvt, *, tm):
    B, S, D = x.shape
    grid = (B, S // tm)
    q_scale = LOG2E / math.sqrt(MLA_NOPE + MLA_ROPE)
    swa_scale = 1.0 / math.sqrt(HEAD_DIM)
    out_shape = (
        jax.ShapeDtypeStruct((B, MLA_HEADS, MLA_QK_PAD, S), BF16),
        jax.ShapeDtypeStruct((B, S, MLA_HEADS * MLA_QK_PAD), BF16),
        jax.ShapeDtypeStruct((B, MLA_HEADS, MLA_V, S), BF16),
        jax.ShapeDtypeStruct((B, S, SWA_HEADS * HEAD_DIM), BF16),
        jax.ShapeDtypeStruct((B, S, SWA_KV_HEADS * HEAD_DIM), BF16),
        jax.ShapeDtypeStruct((B, S, SWA_KV_HEADS * HEAD_DIM), BF16),
    )
    in_specs = [
        pl.BlockSpec((1, tm, D), lambda b, i: (b, i, 0)),
        pl.BlockSpec((1, 1, tm), lambda b, i: (b, 0, i)),
        _const_spec(invf.shape), _const_spec(g.shape), _const_spec(win.shape),
        _const_spec(cqn.shape), _const_spec(ckvn.shape), _const_spec(wq.shape),
        _const_spec(wkn.shape), _const_spec(wvt.shape),
    ]
    out_specs = (
        pl.BlockSpec((1, MLA_HEADS, MLA_QK_PAD, tm), lambda b, i: (b, 0, 0, i)),
        pl.BlockSpec((1, tm, MLA_HEADS * MLA_QK_PAD), lambda b, i: (b, i, 0)),
        pl.BlockSpec((1, MLA_HEADS, MLA_V, tm), lambda b, i: (b, 0, 0, i)),
        pl.BlockSpec((1, tm, SWA_HEADS * HEAD_DIM), lambda b, i: (b, i, 0)),
        pl.BlockSpec((1, tm, SWA_KV_HEADS * HEAD_DIM), lambda b, i: (b, i, 0)),
        pl.BlockSpec((1, tm, SWA_KV_HEADS * HEAD_DIM), lambda b, i: (b, i, 0)),
    )
    return pl.pallas_call(
        functools.partial(_in_proj_kernel, q_scale=q_scale, swa_scale=swa_scale),
        grid=grid, in_specs=in_specs, out_specs=out_specs, out_shape=out_shape,
        compiler_params=_params(("parallel", "parallel")), name="in_proj",
    )(x, pos_row, invf, g, win, cqn, ckvn, wq, wkn, wvt)


def _mla_kernel(qt_ref, k_ref, vt_ref, o_ref, m_ref, l_ref, acc_ref, *, tk):
    S = k_ref.shape[1]
    qt = qt_ref[0, 0]
    m_ref[...] = jnp.full(m_ref.shape, NEG, F32)
    l_ref[...] = jnp.zeros(l_ref.shape, F32)
    acc_ref[...] = jnp.zeros(acc_ref.shape, F32)

    def chunk(c, carry):
        start = pl.multiple_of(c * tk, tk)
        kc = k_ref[0, pl.ds(start, tk), :]
        s = jnp.dot(kc, qt, preferred_element_type=F32)
        m_prev = m_ref[...]
        m_new = jnp.maximum(m_prev, jnp.max(s, axis=0, keepdims=True))
        alpha = jnp.exp2(m_prev - m_new)
        p = jnp.exp2(s - m_new)
        l_ref[...] = alpha * l_ref[...] + jnp.sum(p, axis=0, keepdims=True)
        vc = vt_ref[0, 0, :, pl.ds(start, tk)]
        acc_ref[...] = alpha * acc_ref[...] + jnp.dot(vc, p.astype(BF16), preferred_element_type=F32)
        m_ref[...] = m_new
        return carry

    lax.fori_loop(0, S // tk, chunk, 0)
    o_ref[0] = (acc_ref[...] / l_ref[...]).T


def _mla_attn(qt, k, vt, *, tq, tk):
    B, H, _, S = qt.shape
    grid = (B, H, S // tq)
    return pl.pallas_call(
        functools.partial(_mla_kernel, tk=tk),
        grid=grid,
        in_specs=[
            pl.BlockSpec((1, 1, MLA_QK_PAD, tq), lambda b, h, i: (b, h, 0, i)),
            pl.BlockSpec((1, S, MLA_QK_PAD), lambda b, h, i: (b, 0, h)),
            pl.BlockSpec((1, 1, MLA_V, S), lambda b, h, i: (b, h, 0, 0)),
        ],
        out_specs=pl.BlockSpec((1, tq, MLA_V), lambda b, h, i: (b, i, h)),
        out_shape=jax.ShapeDtypeStruct((B, S, H * MLA_V), F32),
        scratch_shapes=[pltpu.VMEM((1, tq), F32), pltpu.VMEM((1, tq), F32), pltpu.VMEM((MLA_V, tq), F32)],
        compiler_params=_params(("parallel", "parallel", "arbitrary")), name="mla_attn",
    )(qt, k, vt)


_T5_STEPS = (12, 16, 23, 32, 46, 64, 91, 128)


def _swa_kernel(t5_ref, sink_ref, q_ref, kp_ref, kc_ref, kn_ref, vp_ref, vc_ref, vn_ref,
                pq_ref, pp_ref, pc_ref, pn_ref, o_ref, *, nb):
    n = pl.program_id(1)
    pq = pq_ref[0]
    pk = jnp.concatenate([pp_ref[0], pc_ref[0], pn_ref[0]], axis=1)
    rel = pk - pq
    col = lax.broadcasted_iota(jnp.int32, rel.shape, 1)
    lo = jnp.where(n > 0, 0, BLOCK)
    hi = jnp.where(n < nb - 1, 3 * BLOCK, 2 * BLOCK)
    na = jnp.abs(rel)
    mask = (na <= WINDOW) & (col >= lo) & (col < hi)
    large = jnp.full(na.shape, NUM_BUCKETS // 4, jnp.int32)
    for t in _T5_STEPS:
        large = large + (na >= t).astype(jnp.int32)
    large = jnp.minimum(large, NUM_BUCKETS // 2 - 1)
    bkt = jnp.where(rel > 0, NUM_BUCKETS // 2, 0) + jnp.where(na < NUM_BUCKETS // 4, na, large)
    eqs = [bkt == b for b in range(NUM_BUCKETS)]

    q = q_ref[0]
    for g in range(SWA_KV_HEADS):
        lo, hi = g * HEAD_DIM, (g + 1) * HEAD_DIM
        k3 = jnp.concatenate([kp_ref[0, :, lo:hi], kc_ref[0, :, lo:hi], kn_ref[0, :, lo:hi]], axis=0)
        v3 = jnp.concatenate([vp_ref[0, :, lo:hi], vc_ref[0, :, lo:hi], vn_ref[0, :, lo:hi]], axis=0)
        for j in range(SWA_GROUP):
            hd = g * SWA_GROUP + j
            bias = jnp.zeros(na.shape, F32)
            for b in range(NUM_BUCKETS):
                bias = jnp.where(eqs[b], t5_ref[b, hd], bias)
            qh = q[:, hd * HEAD_DIM:(hd + 1) * HEAD_DIM]
            s = lax.dot_general(qh, k3, (((1,), (1,)), ((), ())), preferred_element_type=F32) + bias
            s = jnp.where(mask, s, NEG)
            sk = sink_ref[0, hd]
            m = jnp.maximum(jnp.max(s, axis=-1, keepdims=True), sk)
            e = jnp.exp(s - m)
            denom = jnp.sum(e, axis=-1, keepdims=True) + jnp.exp(sk - m)
            o = jnp.dot(e.astype(BF16), v3, preferred_element_type=F32) / denom
            o_ref[0, :, hd * HEAD_DIM:(hd + 1) * HEAD_DIM] = o


def _swa_attn(t5, sink, sq, sk, sv, pos_col, pos_row):
    B, S, _ = sq.shape
    nb = S // BLOCK
    kvw = SWA_KV_HEADS * HEAD_DIM
    prev = lambda b, n: (b, jnp.maximum(n - 1, 0), 0)
    cur = lambda b, n: (b, n, 0)
    nxt = lambda b, n: (b, jnp.minimum(n + 1, nb - 1), 0)
    rprev = lambda b, n: (b, 0, jnp.maximum(n - 1, 0))
    rcur = lambda b, n: (b, 0, n)
    rnxt = lambda b, n: (b, 0, jnp.minimum(n + 1, nb - 1))
    smem = pl.BlockSpec(memory_space=pltpu.SMEM)
    return pl.pallas_call(
        functools.partial(_swa_kernel, nb=nb),
        grid=(B, nb),
        in_specs=[
            smem, smem,
            pl.BlockSpec((1, BLOCK, SWA_HEADS * HEAD_DIM), cur),
            pl.BlockSpec((1, BLOCK, kvw), prev), pl.BlockSpec((1, BLOCK, kvw), cur), pl.BlockSpec((1, BLOCK, kvw), nxt),
            pl.BlockSpec((1, BLOCK, kvw), prev), pl.BlockSpec((1, BLOCK, kvw), cur), pl.BlockSpec((1, BLOCK, kvw), nxt),
            pl.BlockSpec((1, BLOCK, 1), cur),
            pl.BlockSpec((1, 1, BLOCK), rprev), pl.BlockSpec((1, 1, BLOCK), rcur), pl.BlockSpec((1, 1, BLOCK), rnxt),
        ],
        out_specs=pl.BlockSpec((1, BLOCK, SWA_HEADS * HEAD_DIM), cur),
        out_shape=jax.ShapeDtypeStruct((B, S, SWA_HEADS * HEAD_DIM), F32),
        compiler_params=_params(("parallel", "parallel")), name="swa_attn",
    )(t5, sink, sq, sk, sk, sk, sv, sv, sv, pos_col, pos_row, pos_row, pos_row)


def _out_proj_kernel(x_ref, mla_ref, swa_ref, g1_ref, g2_ref, wo_ref, gf_ref, xo_ref, h2_ref):
    w = wo_ref.shape[0] // 2
    m1 = _rms(mla_ref[...], g1_ref[...]).astype(BF16)
    m2 = _rms(swa_ref[...], g2_ref[...]).astype(BF16)
    y = jnp.dot(m1, wo_ref[0:w, :], preferred_element_type=F32)
    y = y + jnp.dot(m2, wo_ref[w:2 * w, :], preferred_element_type=F32)
    xn = x_ref[...] + y
    xo_ref[...] = xn
    h2_ref[...] = _rms(xn, gf_ref[...]).astype(BF16)


def _out_proj(x, mla_o, swa_o, g1, g2, wo, gf, *, tm):
    T, D = x.shape
    W = mla_o.shape[1]
    row = lambda i: (i, 0)
    return pl.pallas_call(
        _out_proj_kernel,
        grid=(T // tm,),
        in_specs=[
            pl.BlockSpec((tm, D), row), pl.BlockSpec((tm, W), row), pl.BlockSpec((tm, W), row),
            _const_spec(g1.shape), _const_spec(g2.shape), _const_spec(wo.shape), _const_spec(gf.shape),
        ],
        out_specs=(pl.BlockSpec((tm, D), row), pl.BlockSpec((tm, D), row)),
        out_shape=(jax.ShapeDtypeStruct((T, D), F32), jax.ShapeDtypeStruct((T, D), BF16)),
        compiler_params=_params(("parallel",)), name="out_proj",
    )(x, mla_o, swa_o, g1, g2, wo, gf)


def _ffn_kernel(hp_ref, h_ref, hn_ref, wg_ref, wu_ref, cw_ref, cb_ref, wd_ref, y_ref, hext_ref, *,
                tm, tiles_per_seq):
    i = pl.program_id(0)
    j = pl.program_id(1)
    halo = BF16_SUBLANES

    @pl.when(j == 0)
    def _():
        t = i % tiles_per_seq
        hext_ref[0:halo, :] = jnp.where(t == 0, jnp.zeros_like(hp_ref[...]), hp_ref[...])
        hext_ref[halo:halo + tm, :] = h_ref[...]
        hext_ref[halo + tm:, :] = jnp.where(t == tiles_per_seq - 1, jnp.zeros_like(hn_ref[...]), hn_ref[...])

    gp = jnp.dot(hext_ref[...], wg_ref[...], preferred_element_type=F32)
    cw = cw_ref[...]
    g = (gp[halo - 1:halo - 1 + tm] * cw[0:1] + gp[halo:halo + tm] * cw[1:2]
         + gp[halo + 1:halo + 1 + tm] * cw[2:3] + cb_ref[...])
    u = jnp.dot(h_ref[...], wu_ref[...], preferred_element_type=F32)
    a = (g * _sigmoid(g) * u).astype(BF16)
    contrib = jnp.dot(a, wd_ref[...], preferred_element_type=F32)

    @pl.when(j == 0)
    def _():
        y_ref[...] = contrib

    @pl.when(j > 0)
    def _():
        y_ref[...] += contrib


def _ffn(h2, wg, wu, cw, cb, wd, *, tm, tf, seq):
    T, D = h2.shape
    Fp = wg.shape[1]
    halo = BF16_SUBLANES
    hb = tm // halo
    nhb = T // halo
    return pl.pallas_call(
        functools.partial(_ffn_kernel, tm=tm, tiles_per_seq=seq // tm),
        grid=(T // tm, Fp // tf),
        in_specs=[
            pl.BlockSpec((halo, D), lambda i, j: (jnp.maximum(i * hb - 1, 0), 0)),
            pl.BlockSpec((tm, D), lambda i, j: (i, 0)),
            pl.BlockSpec((halo, D), lambda i, j: (jnp.minimum((i + 1) * hb, nhb - 1), 0)),
            pl.BlockSpec((D, tf), lambda i, j: (0, j)),
            pl.BlockSpec((D, tf), lambda i, j: (0, j)),
            pl.BlockSpec((3, tf), lambda i, j: (0, j)),
            pl.BlockSpec((1, tf), lambda i, j: (0, j)),
            pl.BlockSpec((tf, D), lambda i, j: (j, 0)),
        ],
        out_specs=pl.BlockSpec((tm, D), lambda i, j: (i, 0)),
        out_shape=jax.ShapeDtypeStruct((T, D), F32),
        scratch_shapes=[pltpu.VMEM((tm + 2 * halo, D), BF16)],
        compiler_params=_params(("parallel", "arbitrary")), name="ffn",
    )(h2, h2, h2, wg, wu, cw, cb, wd)


def _ple_kernel(x_ref, y_ref, p_ref, wpg_ref, bpg_ref, wpp_ref, gfin_ref, o_ref, *, final):
    x2 = x_ref[...] + y_ref[...]
    gate = _sigmoid(jnp.dot(x2.astype(BF16), wpg_ref[...], preferred_element_type=F32) + bpg_ref[...])
    pp = jnp.dot(p_ref[...].astype(BF16), wpp_ref[...], preferred_element_type=F32)
    x3 = x2 + gate * pp
    if final:
        x3 = _rms(x3, gfin_ref[...])
    o_ref[...] = x3


def _ple(x, y, p, wpg, bpg, wpp, gfin, *, tm, final):
    T, D = x.shape
    P = p.shape[1]
    row = lambda i: (i, 0)
    return pl.pallas_call(
        functools.partial(_ple_kernel, final=final),
        grid=(T // tm,),
        in_specs=[
            pl.BlockSpec((tm, D), row), pl.BlockSpec((tm, D), row), pl.BlockSpec((tm, P), row),
            _const_spec(wpg.shape), _const_spec(bpg.shape), _const_spec(wpp.shape), _const_spec(gfin.shape),
        ],
        out_specs=pl.BlockSpec((tm, D), row),
        out_shape=jax.ShapeDtypeStruct((T, D), F32),
        compiler_params=_params(("parallel",)), name="ple",
    )(x, y, p, wpg, bpg, wpp, gfin)


def _prep_w_in(w_in):
    half = MLA_ROPE // 2
    kr0 = Q_RANK + KV_RANK
    t1, t2 = w_in[:, kr0:kr0 + half], w_in[:, kr0 + half:kr0 + MLA_ROPE]
    zpad = jnp.zeros((w_in.shape[0], LANES - MLA_ROPE), w_in.dtype)
    return jnp.concatenate(
        [w_in[:, :kr0], t1, t2, zpad, t2, t1, zpad, w_in[:, kr0 + MLA_ROPE:]], axis=1).astype(BF16)


def _prep_w_uq(w_uq):
    half = MLA_ROPE // 2
    w = w_uq.reshape(Q_RANK, MLA_HEADS, MLA_NOPE + MLA_ROPE)
    nope, t1, t2 = w[..., :MLA_NOPE], w[..., MLA_NOPE:MLA_NOPE + half], w[..., MLA_NOPE + half:]
    zpad = jnp.zeros((Q_RANK, MLA_HEADS, LANES - MLA_ROPE), w_uq.dtype)
    cols = jnp.concatenate([nope, t1, t2, zpad, t2, t1, zpad], axis=-1)
    return cols.reshape(Q_RANK, MLA_HEADS * 3 * LANES).T.astype(BF16)


def _prep_w_ukv(w_ukv):
    w = w_ukv.reshape(KV_RANK, MLA_HEADS, MLA_NOPE + MLA_V)
    wkn = w[..., :MLA_NOPE].reshape(KV_RANK, MLA_HEADS * MLA_NOPE).astype(BF16)
    wvt = w[..., MLA_NOPE:].reshape(KV_RANK, MLA_HEADS * MLA_V).T.astype(BF16)
    return wkn, wvt


def _pad_ff(w, axis, fp):
    pad = [(0, 0)] * w.ndim
    pad[axis] = (0, fp - w.shape[axis])
    return jnp.pad(w, pad)


def _pick_tile(n, pref):
    t = min(pref, n)
    while n % t:
        t //= 2
    return t


def kernel(x, p, positions, attn_norm, w_in, cq_norm, ckv_norm, w_uq, w_ukv, swa_sink, t5_bias,
           mla_out_norm, swa_out_norm, w_o, ffn_norm, w_gate, w_up, conv_w, conv_b, w_down,
           ple_gate_w, ple_gate_b, ple_proj, final_norm):
    B, S, D = x.shape
    depth = w_in.shape[0]
    T = B * S
    d_ff = w_gate.shape[-1]
    tf = 512
    fp = -(-d_ff // tf) * tf

    tm_in = _pick_tile(S, 512)
    tq = _pick_tile(S, 512)
    tk = _pick_tile(S, 512)
    tm_out = _pick_tile(S, 512)
    tm_ffn = _pick_tile(S, 1024)
    tm_ple = _pick_tile(S, 512)

    pos_row = positions.reshape(B, 1, S)
    pos_col = positions.reshape(B, S, 1)
    invf = (ROPE_THETA ** (-jnp.arange(0, MLA_ROPE, 2, dtype=F32) / MLA_ROPE)).reshape(MLA_ROPE // 2, 1)
    row = lambda v: v.reshape(1, -1)

    xf = x
    for i in range(depth):
        wkn, wvt = _prep_w_ukv(w_ukv[i])
        qt, k, vt, sq, sk, sv = _in_proj(
            xf.reshape(B, S, D), pos_row, invf, row(attn_norm[i]), _prep_w_in(w_in[i]),
            row(cq_norm[i]), row(ckv_norm[i]), _prep_w_uq(w_uq[i]), wkn, wvt, tm=tm_in)
        mla_o = _mla_attn(qt, k, vt, tq=tq, tk=tk)
        swa_o = _swa_attn(t5_bias, row(swa_sink[i]), sq, sk, sv, pos_col, pos_row)
        x1, h2 = _out_proj(
            xf.reshape(T, D), mla_o.reshape(T, -1), swa_o.reshape(T, -1), row(mla_out_norm[i]),
            row(swa_out_norm[i]), w_o[i].astype(BF16), row(ffn_norm[i]), tm=tm_out)
        y = _ffn(
            h2, _pad_ff(w_gate[i], 1, fp).astype(BF16), _pad_ff(w_up[i], 1, fp).astype(BF16),
            _pad_ff(conv_w[i], 1, fp), _pad_ff(row(conv_b[i]), 1, fp),
            _pad_ff(w_down[i], 0, fp).astype(BF16), tm=tm_ffn, tf=tf, seq=S)
        xf = _ple(
            x1, y, p[i].reshape(T, -1), ple_gate_w[i].astype(BF16), row(ple_gate_b[i]),
            ple_proj[i].astype(BF16), row(final_norm), tm=tm_ple, final=(i == depth - 1))
    return xf.reshape(B, S, D)
```

```python
import functools
import math

import jax
import jax.numpy as jnp
from jax import lax
from jax.experimental import pallas as pl
from jax.experimental.pallas import tpu as pltpu

F32 = jnp.float32
BF16 = jnp.bfloat16

EPS = 1e-6
BLOCK = 128
MLA_HEADS = 8
MLA_NOPE = 128
MLA_ROPE = 64
MLA_V = 128
MLA_QK_PAD = 256
Q_RANK = 384
KV_RANK = 256
ROPE_THETA = 10000.0
SWA_HEADS = 8
SWA_KV_HEADS = 2
SWA_GROUP = SWA_HEADS // SWA_KV_HEADS
HEAD_DIM = 128
WINDOW = 128
NUM_BUCKETS = 32
NEG = -1e30
LOG2E = math.log2(math.e)

V7X_VMEM_BYTES = 64 * 1024 * 1024
VMEM_LIMIT_BYTES = V7X_VMEM_BYTES - 8 * 1024 * 1024
LANES = 128
BF16_SUBLANES = 16
MLA_V_EXT = MLA_V + BF16_SUBLANES

_Z_CQ = 0
_Z_CKV = _Z_CQ + Q_RANK
_Z_KA = _Z_CKV + KV_RANK
_Z_KB = _Z_KA + LANES
_Z_SQ = _Z_KB + LANES
_Z_SK = _Z_SQ + SWA_HEADS * HEAD_DIM
_Z_SV = _Z_SK + SWA_KV_HEADS * HEAD_DIM
_Z_END = _Z_SV + SWA_KV_HEADS * HEAD_DIM


def _params(sem):
    return pltpu.CompilerParams(dimension_semantics=sem, vmem_limit_bytes=VMEM_LIMIT_BYTES)


def _const_spec(shape):
    nd = len(shape)
    return pl.BlockSpec(shape, lambda *_: (0,) * nd, pipeline_mode=pl.Buffered(1))


def _rms(x, g):
    ms = jnp.mean(x * x, axis=-1, keepdims=True)
    return x * lax.rsqrt(ms + EPS) * g


def _sigmoid(x):
    return 1.0 / (1.0 + jnp.exp(-x))


def _in_proj_kernel(x_ref, pos_ref, invf_ref, g_ref, win_ref, cqn_ref, ckvn_ref, wq_ref, wkn_ref,
                    wvt_ref, qt_ref, k_ref, vt_ref, sq_ref, sk_ref, sv_ref, *, q_scale, swa_scale):
    x = x_ref[0]
    h = _rms(x, g_ref[...]).astype(BF16)
    z = jnp.dot(h, win_ref[...], preferred_element_type=F32)
    cq = _rms(z[:, _Z_CQ:_Z_CKV], cqn_ref[...]).astype(BF16)
    ckv = _rms(z[:, _Z_CKV:_Z_KA], ckvn_ref[...]).astype(BF16)

    ang = invf_ref[...] * pos_ref[0].astype(F32)
    cos_t, sin_t = jnp.cos(ang), jnp.sin(ang)
    zpad = jnp.zeros((LANES - MLA_ROPE, ang.shape[1]), F32)
    cc_t = jnp.concatenate([cos_t, cos_t, zpad], axis=0)
    ss_t = jnp.concatenate([-sin_t, sin_t, zpad], axis=0)

    r = lax.dot_general(wq_ref[...], cq, (((1,), (1,)), ((), ())), preferred_element_type=F32)
    for hd in range(MLA_HEADS):
        b = hd * 3 * LANES
        qt_ref[0, hd, 0:LANES, :] = (r[b:b + LANES] * q_scale).astype(BF16)
        roped = r[b + LANES:b + 2 * LANES] * cc_t + r[b + 2 * LANES:b + 3 * LANES] * ss_t
        qt_ref[0, hd, LANES:2 * LANES, :] = (roped * q_scale).astype(BF16)

    knope = jnp.dot(ckv, wkn_ref[...], preferred_element_type=F32)
    cc, ss = cc_t.T, ss_t.T
    krope = (z[:, _Z_KA:_Z_KB] * cc + z[:, _Z_KB:_Z_SQ] * ss).astype(BF16)
    for hd in range(MLA_HEADS):
        k_ref[0, :, hd * MLA_QK_PAD:hd * MLA_QK_PAD + LANES] = knope[:, hd * LANES:(hd + 1) * LANES].astype(BF16)
        k_ref[0, :, hd * MLA_QK_PAD + LANES:(hd + 1) * MLA_QK_PAD] = krope

    vt = lax.dot_general(wvt_ref[...], ckv, (((1,), (1,)), ((), ())), preferred_element_type=F32)
    ones_row = (lax.broadcasted_iota(jnp.int32, (BF16_SUBLANES, vt.shape[1]), 0) == 0).astype(BF16)
    for hd in range(MLA_HEADS):
        vt_ref[0, hd, 0:MLA_V, :] = vt[hd * MLA_V:(hd + 1) * MLA_V].astype(BF16)
        vt_ref[0, hd, MLA_V:MLA_V_EXT, :] = ones_row

    sq_ref[0] = (z[:, _Z_SQ:_Z_SK] * swa_scale).astype(BF16)
    sk_ref[0] = z[:, _Z_SK:_Z_SV].astype(BF16)
    sv_ref[0] = z[:, _Z_SV:_Z_END].astype(BF16)


def _in_proj(x, pos_row, invf, g, win, cqn, ckvn, wq, wkn, wvt, *, tm):
    B, S, D = x.shape
    grid = (B, S // tm)
    q_scale = LOG2E / math.sqrt(MLA_NOPE + MLA_ROPE)
    swa_scale = 1.0 / math.sqrt(HEAD_DIM)
    out_shape = (
        jax.ShapeDtypeStruct((B, MLA_HEADS, MLA_QK_PAD, S), BF16),
        jax.ShapeDtypeStruct((B, S, MLA_HEADS * MLA_QK_PAD), BF16),
        jax.ShapeDtypeStruct((B, MLA_HEADS, MLA_V_EXT, S), BF16),
        jax.ShapeDtypeStruct((B, S, SWA_HEADS * HEAD_DIM), BF16),
        jax.ShapeDtypeStruct((B, S, SWA_KV_HEADS * HEAD_DIM), BF16),
        jax.ShapeDtypeStruct((B, S, SWA_KV_HEADS * HEAD_DIM), BF16),
    )
    in_specs = [
        pl.BlockSpec((1, tm, D), lambda b, i: (b, i, 0)),
        pl.BlockSpec((1, 1, tm), lambda b, i: (b, 0, i)),
        _const_spec(invf.shape), _const_spec(g.shape), _const_spec(win.shape),
        _const_spec(cqn.shape), _const_spec(ckvn.shape), _const_spec(wq.shape),
        _const_spec(wkn.shape), _const_spec(wvt.shape),
    ]
    out_specs = (
        pl.BlockSpec((1, MLA_HEADS, MLA_QK_PAD, tm), lambda b, i: (b, 0, 0, i)),
        pl.BlockSpec((1, tm, MLA_HEADS * MLA_QK_PAD), lambda b, i: (b, i, 0)),
        pl.BlockSpec((1, MLA_HEADS, MLA_V_EXT, tm), lambda b, i: (b, 0, 0, i)),
        pl.BlockSpec((1, tm, SWA_HEADS * HEAD_DIM), lambda b, i: (b, i, 0)),
        pl.BlockSpec((1, tm, SWA_KV_HEADS * HEAD_DIM), lambda b, i: (b, i, 0)),
        pl.BlockSpec((1, tm, SWA_KV_HEADS * HEAD_DIM), lambda b, i: (b, i, 0)),
    )
    return pl.pallas_call(
        functools.partial(_in_proj_kernel, q_scale=q_scale, swa_scale=swa_scale),
        grid=grid, in_specs=in_specs, out_specs=out_specs, out_shape=out_shape,
        compiler_params=_params(("parallel", "parallel")), name="in_proj",
    )(x, pos_row, invf, g, win, cqn, ckvn, wq, wkn, wvt)


def _mla_kernel(qt_ref, k_ref, vt_ref, o_ref, m_ref, acc_ref, sa_ref, sb_ref, *, tk):
    S = k_ref.shape[1]
    n = S // tk
    m_ref[...] = jnp.full(m_ref.shape, NEG, F32)
    acc_ref[...] = jnp.zeros(acc_ref.shape, F32)

    def scores(c, s_ref):
        start = pl.multiple_of(c * tk, tk)
        s_ref[...] = jnp.dot(k_ref[0, pl.ds(start, tk), :], qt_ref[0, 0], preferred_element_type=F32)

    def softmax_pv(c, s_ref):
        start = pl.multiple_of(c * tk, tk)
        s = s_ref[...]
        m_prev = m_ref[...]
        m_new = jnp.maximum(m_prev, jnp.max(s, axis=0, keepdims=True))
        alpha = jnp.exp2(m_prev - m_new)
        p = jnp.exp2(s - m_new).astype(BF16)
        vc = vt_ref[0, 0, :, pl.ds(start, tk)]
        acc_ref[...] = alpha * acc_ref[...] + jnp.dot(vc, p, preferred_element_type=F32)
        m_ref[...] = m_new

    scores(0, sa_ref)

    def pair(i, carry):
        scores(2 * i + 1, sb_ref)
        softmax_pv(2 * i, sa_ref)
        scores(2 * i + 2, sa_ref)
        softmax_pv(2 * i + 1, sb_ref)
        return carry

    lax.fori_loop(0, n // 2 - 1, pair, 0)
    scores(n - 1, sb_ref)
    softmax_pv(n - 2, sa_ref)
    softmax_pv(n - 1, sb_ref)
    acc = acc_ref[...]
    o_ref[0] = (acc[0:MLA_V] / acc[MLA_V:MLA_V + 1]).T


def _mla_attn(qt, k, vt, *, tq, tk):
    B, H, _, S = qt.shape
    vrows = vt.shape[2]
    assert (S // tk) % 2 == 0 and S // tk >= 2
    grid = (B, H, S // tq)
    return pl.pallas_call(
        functools.partial(_mla_kernel, tk=tk),
        grid=grid,
        in_specs=[
            pl.BlockSpec((1, 1, MLA_QK_PAD, tq), lambda b, h, i: (b, h, 0, i)),
            pl.BlockSpec((1, S, MLA_QK_PAD), lambda b, h, i: (b, 0, h)),
            pl.BlockSpec((1, 1, vrows, S), lambda b, h, i: (b, h, 0, 0)),
        ],
        out_specs=pl.BlockSpec((1, tq, MLA_V), lambda b, h, i: (b, i, h)),
        out_shape=jax.ShapeDtypeStruct((B, S, H * MLA_V), F32),
        scratch_shapes=[pltpu.VMEM((1, tq), F32), pltpu.VMEM((vrows, tq), F32),
                        pltpu.VMEM((tk, tq), F32), pltpu.VMEM((tk, tq), F32)],
        compiler_params=_params(("parallel", "parallel", "arbitrary")), name="mla_attn",
    )(qt, k, vt)


_T5_STEPS = (12, 16, 23, 32, 46, 64, 91, 128)


def _swa_kernel(t5_ref, sink_ref, q_ref, kp_ref, kc_ref, kn_ref, vp_ref, vc_ref, vn_ref,
                pq_ref, pp_ref, pc_ref, pn_ref, o_ref, *, nb):
    n = pl.program_id(1)
    pq = pq_ref[0]
    pk = jnp.concatenate([pp_ref[0], pc_ref[0], pn_ref[0]], axis=1)
    rel = pk - pq
    col = lax.broadcasted_iota(jnp.int32, rel.shape, 1)
    lo = jnp.where(n > 0, 0, BLOCK)
    hi = jnp.where(n < nb - 1, 3 * BLOCK, 2 * BLOCK)
    na = jnp.abs(rel)
    mask = (na <= WINDOW) & (col >= lo) & (col < hi)
    large = jnp.full(na.shape, NUM_BUCKETS // 4, jnp.int32)
    for t in _T5_STEPS:
        large = large + (na >= t).astype(jnp.int32)
    large = jnp.minimum(large, NUM_BUCKETS // 2 - 1)
    bkt = jnp.where(rel > 0, NUM_BUCKETS // 2, 0) + jnp.where(na < NUM_BUCKETS // 4, na, large)
    eqs = [bkt == b for b in range(NUM_BUCKETS)]

    q = q_ref[0]
    for g in range(SWA_KV_HEADS):
        lo, hi = g * HEAD_DIM, (g + 1) * HEAD_DIM
        k3 = jnp.concatenate([kp_ref[0, :, lo:hi], kc_ref[0, :, lo:hi], kn_ref[0, :, lo:hi]], axis=0)
        v3 = jnp.concatenate([vp_ref[0, :, lo:hi], vc_ref[0, :, lo:hi], vn_ref[0, :, lo:hi]], axis=0)
        for j in range(SWA_GROUP):
            hd = g * SWA_GROUP + j
            bias = jnp.zeros(na.shape, F32)
            for b in range(NUM_BUCKETS):
                bias = jnp.where(eqs[b], t5_ref[b, hd], bias)
            qh = q[:, hd * HEAD_DIM:(hd + 1) * HEAD_DIM]
            s = lax.dot_general(qh, k3, (((1,), (1,)), ((), ())), preferred_element_type=F32) + bias
            s = jnp.where(mask, s, NEG)
            sk = sink_ref[0, hd]
            m = jnp.maximum(jnp.max(s, axis=-1, keepdims=True), sk)
            e = jnp.exp(s - m)
            denom = jnp.sum(e, axis=-1, keepdims=True) + jnp.exp(sk - m)
            o = jnp.dot(e.astype(BF16), v3, preferred_element_type=F32) / denom
            o_ref[0, :, hd * HEAD_DIM:(hd + 1) * HEAD_DIM] = o


def _swa_attn(t5, sink, sq, sk, sv, pos_col, pos_row):
    B, S, _ = sq.shape
    nb = S // BLOCK
    kvw = SWA_KV_HEADS * HEAD_DIM
    prev = lambda b, n: (b, jnp.maximum(n - 1, 0), 0)
    cur = lambda b, n: (b, n, 0)
    nxt = lambda b, n: (b, jnp.minimum(n + 1, nb - 1), 0)
    rprev = lambda b, n: (b, 0, jnp.maximum(n - 1, 0))
    rcur = lambda b, n: (b, 0, n)
    rnxt = lambda b, n: (b, 0, jnp.minimum(n + 1, nb - 1))
    smem = pl.BlockSpec(memory_space=pltpu.SMEM)
    return pl.pallas_call(
        functools.partial(_swa_kernel, nb=nb),
        grid=(B, nb),
        in_specs=[
            smem, smem,
            pl.BlockSpec((1, BLOCK, SWA_HEADS * HEAD_DIM), cur),
            pl.BlockSpec((1, BLOCK, kvw), prev), pl.BlockSpec((1, BLOCK, kvw), cur), pl.BlockSpec((1, BLOCK, kvw), nxt),
            pl.BlockSpec((1, BLOCK, kvw), prev), pl.BlockSpec((1, BLOCK, kvw), cur), pl.BlockSpec((1, BLOCK, kvw), nxt),
            pl.BlockSpec((1, BLOCK, 1), cur),
            pl.BlockSpec((1, 1, BLOCK), rprev), pl.BlockSpec((1, 1, BLOCK), rcur), pl.BlockSpec((1, 1, BLOCK), rnxt),
        ],
        out_specs=pl.BlockSpec((1, BLOCK, SWA_HEADS * HEAD_DIM), cur),
        out_shape=jax.ShapeDtypeStruct((B, S, SWA_HEADS * HEAD_DIM), F32),
        compiler_params=_params(("parallel", "parallel")), name="swa_attn",
    )(t5, sink, sq, sk, sk, sk, sv, sv, sv, pos_col, pos_row, pos_row, pos_row)


def _out_proj_kernel(x_ref, mla_ref, swa_ref, g1_ref, g2_ref, wo_ref, gf_ref, xo_ref, h2_ref):
    w = wo_ref.shape[0] // 2
    m1 = _rms(mla_ref[...], g1_ref[...]).astype(BF16)
    m2 = _rms(swa_ref[...], g2_ref[...]).astype(BF16)
    y = jnp.dot(m1, wo_ref[0:w, :], preferred_element_type=F32)
    y = y + jnp.dot(m2, wo_ref[w:2 * w, :], preferred_element_type=F32)
    xn = x_ref[...] + y
    xo_ref[...] = xn
    h2_ref[...] = _rms(xn, gf_ref[...]).astype(BF16)


def _out_proj(x, mla_o, swa_o, g1, g2, wo, gf, *, tm):
    T, D = x.shape
    W = mla_o.shape[1]
    row = lambda i: (i, 0)
    return pl.pallas_call(
        _out_proj_kernel,
        grid=(T // tm,),
        in_specs=[
            pl.BlockSpec((tm, D), row), pl.BlockSpec((tm, W), row), pl.BlockSpec((tm, W), row),
            _const_spec(g1.shape), _const_spec(g2.shape), _const_spec(wo.shape), _const_spec(gf.shape),
        ],
        out_specs=(pl.BlockSpec((tm, D), row), pl.BlockSpec((tm, D), row)),
        out_shape=(jax.ShapeDtypeStruct((T, D), F32), jax.ShapeDtypeStruct((T, D), BF16)),
        compiler_params=_params(("parallel",)), name="out_proj",
    )(x, mla_o, swa_o, g1, g2, wo, gf)


def _ffn_kernel(hp_ref, h_ref, hn_ref, wg_ref, wu_ref, cw_ref, cb_ref, wd_ref, y_ref, hext_ref, *,
                tm, tiles_per_seq):
    i = pl.program_id(0)
    j = pl.program_id(1)
    halo = BF16_SUBLANES

    @pl.when(j == 0)
    def _():
        t = i % tiles_per_seq
        hext_ref[0:halo, :] = jnp.where(t == 0, jnp.zeros_like(hp_ref[...]), hp_ref[...])
        hext_ref[halo:halo + tm, :] = h_ref[...]
        hext_ref[halo + tm:, :] = jnp.where(t == tiles_per_seq - 1, jnp.zeros_like(hn_ref[...]), hn_ref[...])

    gp = jnp.dot(hext_ref[...], wg_ref[...], preferred_element_type=F32)
    cw = cw_ref[...]
    g = (gp[halo - 1:halo - 1 + tm] * cw[0:1] + gp[halo:halo + tm] * cw[1:2]
         + gp[halo + 1:halo + 1 + tm] * cw[2:3] + cb_ref[...])
    u = jnp.dot(h_ref[...], wu_ref[...], preferred_element_type=F32)
    a = (g * _sigmoid(g) * u).astype(BF16)
    contrib = jnp.dot(a, wd_ref[...], preferred_element_type=F32)

    @pl.when(j == 0)
    def _():
        y_ref[...] = contrib

    @pl.when(j > 0)
    def _():
        y_ref[...] += contrib


def _ffn(h2, wg, wu, cw, cb, wd, *, tm, tf, seq):
    T, D = h2.shape
    Fp = wg.shape[1]
    halo = BF16_SUBLANES
    hb = tm // halo
    nhb = T // halo
    return pl.pallas_call(
        functools.partial(_ffn_kernel, tm=tm, tiles_per_seq=seq // tm),
        grid=(T // tm, Fp // tf),
        in_specs=[
            pl.BlockSpec((halo, D), lambda i, j: (jnp.maximum(i * hb - 1, 0), 0)),
            pl.BlockSpec((tm, D), lambda i, j: (i, 0)),
            pl.BlockSpec((halo, D), lambda i, j: (jnp.minimum((i + 1) * hb, nhb - 1), 0)),
            pl.BlockSpec((D, tf), lambda i, j: (0, j)),
            pl.BlockSpec((D, tf), lambda i, j: (0, j)),
            pl.BlockSpec((3, tf), lambda i, j: (0, j)),
            pl.BlockSpec((1, tf), lambda i, j: (0, j)),
            pl.BlockSpec((tf, D), lambda i, j: (j, 0)),
        ],
        out_specs=pl.BlockSpec((tm, D), lambda i, j: (i, 0)),
        out_shape=jax.ShapeDtypeStruct((T, D), F32),
        scratch_shapes=[pltpu.VMEM((tm + 2 * halo, D), BF16)],
        compiler_params=_params(("parallel", "arbitrary")), name="ffn",
    )(h2, h2, h2, wg, wu, cw, cb, wd)


def _ple_kernel(x_ref, y_ref, p_ref, wpg_ref, bpg_ref, wpp_ref, gfin_ref, o_ref, *, final):
    x2 = x_ref[...] + y_ref[...]
    gate = _sigmoid(jnp.dot(x2.astype(BF16), wpg_ref[...], preferred_element_type=F32) + bpg_ref[...])
    pp = jnp.dot(p_ref[...].astype(BF16), wpp_ref[...], preferred_element_type=F32)
    x3 = x2 + gate * pp
    if final:
        x3 = _rms(x3, gfin_ref[...])
    o_ref[...] = x3


def _ple(x, y, p, wpg, bpg, wpp, gfin, *, tm, final):
    T, D = x.shape
    P = p.shape[1]
    row = lambda i: (i, 0)
    return pl.pallas_call(
        functools.partial(_ple_kernel, final=final),
        grid=(T // tm,),
        in_specs=[
            pl.BlockSpec((tm, D), row), pl.BlockSpec((tm, D), row), pl.BlockSpec((tm, P), row),
            _const_spec(wpg.shape), _const_spec(bpg.shape), _const_spec(wpp.shape), _const_spec(gfin.shape),
        ],
        out_specs=pl.BlockSpec((tm, D), row),
        out_shape=jax.ShapeDtypeStruct((T, D), F32),
        compiler_params=_params(("parallel",)), name="ple",
    )(x, y, p, wpg, bpg, wpp, gfin)


def _prep_w_in(w_in):
    half = MLA_ROPE // 2
    kr0 = Q_RANK + KV_RANK
    t1, t2 = w_in[:, kr0:kr0 + half], w_in[:, kr0 + half:kr0 + MLA_ROPE]
    zpad = jnp.zeros((w_in.shape[0], LANES - MLA_ROPE), w_in.dtype)
    return jnp.concatenate(
        [w_in[:, :kr0], t1, t2, zpad, t2, t1, zpad, w_in[:, kr0 + MLA_ROPE:]], axis=1).astype(BF16)


def _prep_w_uq(w_uq):
    half = MLA_ROPE // 2
    w = w_uq.reshape(Q_RANK, MLA_HEADS, MLA_NOPE + MLA_ROPE)
    nope, t1, t2 = w[..., :MLA_NOPE], w[..., MLA_NOPE:MLA_NOPE + half], w[..., MLA_NOPE + half:]
    zpad = jnp.zeros((Q_RANK, MLA_HEADS, LANES - MLA_ROPE), w_uq.dtype)
    cols = jnp.concatenate([nope, t1, t2, zpad, t2, t1, zpad], axis=-1)
    return cols.reshape(Q_RANK, MLA_HEADS * 3 * LANES).T.astype(BF16)


def _prep_w_ukv(w_ukv):
    w = w_ukv.reshape(KV_RANK, MLA_HEADS, MLA_NOPE + MLA_V)
    wkn = w[..., :MLA_NOPE].reshape(KV_RANK, MLA_HEADS * MLA_NOPE).astype(BF16)
    wvt = w[..., MLA_NOPE:].reshape(KV_RANK, MLA_HEADS * MLA_V).T.astype(BF16)
    return wkn, wvt


def _pad_ff(w, axis, fp):
    pad = [(0, 0)] * w.ndim
    pad[axis] = (0, fp - w.shape[axis])
    return jnp.pad(w, pad)


def _pick_tile(n, pref):
    t = min(pref, n)
    while n % t:
        t //= 2
    return t


def kernel(x, p, positions, attn_norm, w_in, cq_norm, ckv_norm, w_uq, w_ukv, swa_sink, t5_bias,
           mla_out_norm, swa_out_norm, w_o, ffn_norm, w_gate, w_up, conv_w, conv_b, w_down,
           ple_gate_w, ple_gate_b, ple_proj, final_norm):
    B, S, D = x.shape
    depth = w_in.shape[0]
    T = B * S
    d_ff = w_gate.shape[-1]
    tf = 512
    fp = -(-d_ff // tf) * tf

    tm_in = _pick_tile(S, 512)
    tq = _pick_tile(S, 512)
    tk = _pick_tile(S, 512)
    tm_out = _pick_tile(S, 512)
    tm_ffn = _pick_tile(S, 1024)
    tm_ple = _pick_tile(S, 512)

    pos_row = positions.reshape(B, 1, S)
    pos_col = positions.reshape(B, S, 1)
    invf = (ROPE_THETA ** (-jnp.arange(0, MLA_ROPE, 2, dtype=F32) / MLA_ROPE)).reshape(MLA_ROPE // 2, 1)
    row = lambda v: v.reshape(1, -1)

    xf = x
    for i in range(depth):
        wkn, wvt = _prep_w_ukv(w_ukv[i])
        qt, k, vt, sq, sk, sv = _in_proj(
            xf.reshape(B, S, D), pos_row, invf, row(attn_norm[i]), _prep_w_in(w_in[i]),
            row(cq_norm[i]), row(ckv_norm[i]), _prep_w_uq(w_uq[i]), wkn, wvt, tm=tm_in)
        mla_o = _mla_attn(qt, k, vt, tq=tq, tk=tk)
        swa_o = _swa_attn(t5_bias, row(swa_sink[i]), sq, sk, sv, pos_col, pos_row)
        x1, h2 = _out_proj(
            xf.reshape(T, D), mla_o.reshape(T, -1), swa_o.reshape(T, -1), row(mla_out_norm[i]),
            row(swa_out_norm[i]), w_o[i].astype(BF16), row(ffn_norm[i]), tm=tm_out)
        y = _ffn(
            h2, _pad_ff(w_gate[i], 1, fp).astype(BF16), _pad_ff(w_up[i], 1, fp).astype(BF16),
            _pad_ff(conv_w[i], 1, fp), _pad_ff(row(conv_b[i]), 1, fp),
            _pad_ff(w_down[i], 0, fp).astype(BF16), tm=tm_ffn, tf=tf, seq=S)
        xf = _ple(
            x1, y, p[i].reshape(T, -1), ple_gate_w[i].astype(BF16), row(ple_gate_b[i]),
            ple_proj[i].astype(BF16), row(final_norm), tm=tm_ple, final=(i == depth - 1))
    return xf.reshape(B, S, D)
```

```python
import functools
import math

import jax
import jax.numpy as jnp
from jax import lax
from jax.experimental import pallas as pl
from jax.experimental.pallas import tpu as pltpu

F32 = jnp.float32
BF16 = jnp.bfloat16

EPS = 1e-6
BLOCK = 128
MLA_HEADS = 8
MLA_NOPE = 128
MLA_ROPE = 64
MLA_V = 128
MLA_QK_PAD = 256
Q_RANK = 384
KV_RANK = 256
ROPE_THETA = 10000.0
SWA_HEADS = 8
SWA_KV_HEADS = 2
SWA_GROUP = SWA_HEADS // SWA_KV_HEADS
HEAD_DIM = 128
WINDOW = 128
NUM_BUCKETS = 32
NEG = -1e30
LOG2E = math.log2(math.e)

V7X_VMEM_BYTES = 64 * 1024 * 1024
VMEM_LIMIT_BYTES = V7X_VMEM_BYTES - 8 * 1024 * 1024
LANES = 128
BF16_SUBLANES = 16
MLA_V_EXT = MLA_V + BF16_SUBLANES

_Z_CQ = 0
_Z_CKV = _Z_CQ + Q_RANK
_Z_KA = _Z_CKV + KV_RANK
_Z_KB = _Z_KA + LANES
_Z_SQ = _Z_KB + LANES
_Z_SK = _Z_SQ + SWA_HEADS * HEAD_DIM
_Z_SV = _Z_SK + SWA_KV_HEADS * HEAD_DIM
_Z_END = _Z_SV + SWA_KV_HEADS * HEAD_DIM


def _params(sem):
    return pltpu.CompilerParams(dimension_semantics=sem, vmem_limit_bytes=VMEM_LIMIT_BYTES)


def _const_spec(shape):
    nd = len(shape)
    return pl.BlockSpec(shape, lambda *_: (0,) * nd, pipeline_mode=pl.Buffered(1))


def _rms(x, g):
    ms = jnp.mean(x * x, axis=-1, keepdims=True)
    return x * lax.rsqrt(ms + EPS) * g


def _sigmoid(x):
    return 1.0 / (1.0 + jnp.exp(-x))


def _in_proj_kernel(x_ref, pos_ref, invf_ref, g_ref, win_ref, cqn_ref, ckvn_ref, wq_ref, wkn_ref,
                    wvt_ref, qt_ref, k_ref, vt_ref, sq_ref, sk_ref, sv_ref, *, q_scale, swa_scale):
    x = x_ref[0]
    h = _rms(x, g_ref[...]).astype(BF16)
    z = jnp.dot(h, win_ref[...], preferred_element_type=F32)
    cq = _rms(z[:, _Z_CQ:_Z_CKV], cqn_ref[...]).astype(BF16)
    ckv = _rms(z[:, _Z_CKV:_Z_KA], ckvn_ref[...]).astype(BF16)

    ang = invf_ref[...] * pos_ref[0].astype(F32)
    cos_t, sin_t = jnp.cos(ang), jnp.sin(ang)
    zpad = jnp.zeros((LANES - MLA_ROPE, ang.shape[1]), F32)
    cc_t = jnp.concatenate([cos_t, cos_t, zpad], axis=0)
    ss_t = jnp.concatenate([-sin_t, sin_t, zpad], axis=0)

    r = lax.dot_general(wq_ref[...], cq, (((1,), (1,)), ((), ())), preferred_element_type=F32)
    for hd in range(MLA_HEADS):
        b = hd * 3 * LANES
        qt_ref[0, hd, 0:LANES, :] = (r[b:b + LANES] * q_scale).astype(BF16)
        roped = r[b + LANES:b + 2 * LANES] * cc_t + r[b + 2 * LANES:b + 3 * LANES] * ss_t
        qt_ref[0, hd, LANES:2 * LANES, :] = (roped * q_scale).astype(BF16)

    knope = jnp.dot(ckv, wkn_ref[...], preferred_element_type=F32)
    cc, ss = cc_t.T, ss_t.T
    krope = (z[:, _Z_KA:_Z_KB] * cc + z[:, _Z_KB:_Z_SQ] * ss).astype(BF16)
    for hd in range(MLA_HEADS):
        k_ref[0, :, hd * MLA_QK_PAD:hd * MLA_QK_PAD + LANES] = knope[:, hd * LANES:(hd + 1) * LANES].astype(BF16)
        k_ref[0, :, hd * MLA_QK_PAD + LANES:(hd + 1) * MLA_QK_PAD] = krope

    vt = lax.dot_general(wvt_ref[...], ckv, (((1,), (1,)), ((), ())), preferred_element_type=F32)
    ones_row = (lax.broadcasted_iota(jnp.int32, (BF16_SUBLANES, vt.shape[1]), 0) == 0).astype(BF16)
    for hd in range(MLA_HEADS):
        vt_ref[0, hd, 0:MLA_V, :] = vt[hd * MLA_V:(hd + 1) * MLA_V].astype(BF16)
        vt_ref[0, hd, MLA_V:MLA_V_EXT, :] = ones_row

    sq_ref[0] = (z[:, _Z_SQ:_Z_SK] * swa_scale).astype(BF16)
    sk_ref[0] = z[:, _Z_SK:_Z_SV].astype(BF16)
    sv_ref[0] = z[:, _Z_SV:_Z_END].astype(BF16)


def _in_proj(x, pos_row, invf, g, win, cqn, ckvn, wq, wkn, wvt, *, tm):
    B, S, D = x.shape
    grid = (B, S // tm)
    q_scale = LOG2E / math.sqrt(MLA_NOPE + MLA_ROPE)
    swa_scale = 1.0 / math.sqrt(HEAD_DIM)
    out_shape = (
        jax.ShapeDtypeStruct((B, MLA_HEADS, MLA_QK_PAD, S), BF16),
        jax.ShapeDtypeStruct((B, S, MLA_HEADS * MLA_QK_PAD), BF16),
        jax.ShapeDtypeStruct((B, MLA_HEADS, MLA_V_EXT, S), BF16),
        jax.ShapeDtypeStruct((B, S, SWA_HEADS * HEAD_DIM), BF16),
        jax.ShapeDtypeStruct((B, S, SWA_KV_HEADS * HEAD_DIM), BF16),
        jax.ShapeDtypeStruct((B, S, SWA_KV_HEADS * HEAD_DIM), BF16),
    )
    in_specs = [
        pl.BlockSpec((1, tm, D), lambda b, i: (b, i, 0)),
        pl.BlockSpec((1, 1, tm), lambda b, i: (b, 0, i)),
        _const_spec(invf.shape), _const_spec(g.shape), _const_spec(win.shape),
        _const_spec(cqn.shape), _const_spec(ckvn.shape), _const_spec(wq.shape),
        _const_spec(wkn.shape), _const_spec(wvt.shape),
    ]
    out_specs = (
        pl.BlockSpec((1, MLA_HEADS, MLA_QK_PAD, tm), lambda b, i: (b, 0, 0, i)),
        pl.BlockSpec((1, tm, MLA_HEADS * MLA_QK_PAD), lambda b, i: (b, i, 0)),
        pl.BlockSpec((1, MLA_HEADS, MLA_V_EXT, tm), lambda b, i: (b, 0, 0, i)),
        pl.BlockSpec((1, tm, SWA_HEADS * HEAD_DIM), lambda b, i: (b, i, 0)),
        pl.BlockSpec((1, tm, SWA_KV_HEADS * HEAD_DIM), lambda b, i: (b, i, 0)),
        pl.BlockSpec((1, tm, SWA_KV_HEADS * HEAD_DIM), lambda b, i: (b, i, 0)),
    )
    return pl.pallas_call(
        functools.partial(_in_proj_kernel, q_scale=q_scale, swa_scale=swa_scale),
        grid=grid, in_specs=in_specs, out_specs=out_specs, out_shape=out_shape,
        compiler_params=_params(("parallel", "parallel")), name="in_proj",
    )(x, pos_row, invf, g, win, cqn, ckvn, wq, wkn, wvt)


def _mla_kernel(qt_ref, k_ref, vt_ref, o_ref, m_ref, acc_ref, sa_ref, sb_ref, *, tk, unroll):
    S = k_ref.shape[1]
    n = S // tk
    m_ref[...] = jnp.full(m_ref.shape, NEG, F32)
    acc_ref[...] = jnp.zeros(acc_ref.shape, F32)

    def scores(c, s_ref):
        start = pl.multiple_of(c * tk, tk)
        s_ref[...] = jnp.dot(k_ref[0, pl.ds(start, tk), :], qt_ref[0, 0], preferred_element_type=F32)

    def softmax_pv(c, s_ref):
        start = pl.multiple_of(c * tk, tk)
        s = s_ref[...]
        m_prev = m_ref[...]
        m_new = jnp.maximum(m_prev, jnp.max(s, axis=0, keepdims=True))
        alpha = jnp.exp2(m_prev - m_new)
        p = jnp.exp2(s - m_new).astype(BF16)
        vc = vt_ref[0, 0, :, pl.ds(start, tk)]
        acc_ref[...] = alpha * acc_ref[...] + jnp.dot(vc, p, preferred_element_type=F32)
        m_ref[...] = m_new

    scores(0, sa_ref)
    bufs = (sa_ref, sb_ref)

    def group(i, carry):
        c0 = i * unroll
        for u in range(unroll):
            nxt = c0 + u + 1
            if u == unroll - 1:
                nxt = jnp.where(nxt == n, 0, nxt)
            scores(nxt, bufs[(u + 1) % 2])
            softmax_pv(c0 + u, bufs[u % 2])
        return carry

    lax.fori_loop(0, n // unroll, group, 0)
    acc = acc_ref[...]
    o_ref[0] = (acc[0:MLA_V] / acc[MLA_V:MLA_V + 1]).T


def _mla_attn(qt, k, vt, *, tq, tk):
    B, H, _, S = qt.shape
    vrows = vt.shape[2]
    n = S // tk
    unroll = max(u for u in (2, 4, 8) if n % u == 0)
    grid = (B, H, S // tq)
    return pl.pallas_call(
        functools.partial(_mla_kernel, tk=tk, unroll=unroll),
        grid=grid,
        in_specs=[
            pl.BlockSpec((1, 1, MLA_QK_PAD, tq), lambda b, h, i: (b, h, 0, i)),
            pl.BlockSpec((1, S, MLA_QK_PAD), lambda b, h, i: (b, 0, h)),
            pl.BlockSpec((1, 1, vrows, S), lambda b, h, i: (b, h, 0, 0)),
        ],
        out_specs=pl.BlockSpec((1, tq, MLA_V), lambda b, h, i: (b, i, h)),
        out_shape=jax.ShapeDtypeStruct((B, S, H * MLA_V), F32),
        scratch_shapes=[pltpu.VMEM((1, tq), F32), pltpu.VMEM((vrows, tq), F32),
                        pltpu.VMEM((tk, tq), F32), pltpu.VMEM((tk, tq), F32)],
        compiler_params=_params(("parallel", "parallel", "arbitrary")), name="mla_attn",
    )(qt, k, vt)


_T5_STEPS = (12, 16, 23, 32, 46, 64, 91, 128)


def _swa_kernel(t5_ref, sink_ref, q_ref, kp_ref, kc_ref, kn_ref, vp_ref, vc_ref, vn_ref,
                pq_ref, pp_ref, pc_ref, pn_ref, o_ref, *, nb):
    n = pl.program_id(1)
    pq = pq_ref[0]
    pk = jnp.concatenate([pp_ref[0], pc_ref[0], pn_ref[0]], axis=1)
    rel = pk - pq
    col = lax.broadcasted_iota(jnp.int32, rel.shape, 1)
    lo = jnp.where(n > 0, 0, BLOCK)
    hi = jnp.where(n < nb - 1, 3 * BLOCK, 2 * BLOCK)
    na = jnp.abs(rel)
    mask = (na <= WINDOW) & (col >= lo) & (col < hi)
    large = jnp.full(na.shape, NUM_BUCKETS // 4, jnp.int32)
    for t in _T5_STEPS:
        large = large + (na >= t).astype(jnp.int32)
    large = jnp.minimum(large, NUM_BUCKETS // 2 - 1)
    bkt = jnp.where(rel > 0, NUM_BUCKETS // 2, 0) + jnp.where(na < NUM_BUCKETS // 4, na, large)
    bkts = [bkt[:, j * BLOCK:(j + 1) * BLOCK] for j in range(3)]

    q = q_ref[0]
    for g in range(SWA_KV_HEADS):
        lo, hi = g * HEAD_DIM, (g + 1) * HEAD_DIM
        k3 = jnp.concatenate([kp_ref[0, :, lo:hi], kc_ref[0, :, lo:hi], kn_ref[0, :, lo:hi]], axis=0)
        v3 = jnp.concatenate([vp_ref[0, :, lo:hi], vc_ref[0, :, lo:hi], vn_ref[0, :, lo:hi]], axis=0)
        for j in range(SWA_GROUP):
            hd = g * SWA_GROUP + j
            tab = jnp.broadcast_to(t5_ref[hd:hd + 1, :], (BLOCK, LANES))
            bias = jnp.concatenate([jnp.take_along_axis(tab, b, axis=1) for b in bkts], axis=1)
            qh = q[:, hd * HEAD_DIM:(hd + 1) * HEAD_DIM]
            s = lax.dot_general(qh, k3, (((1,), (1,)), ((), ())), preferred_element_type=F32) + bias
            s = jnp.where(mask, s, NEG)
            sk = sink_ref[0, hd]
            m = jnp.maximum(jnp.max(s, axis=-1, keepdims=True), sk)
            e = jnp.exp(s - m)
            denom = jnp.sum(e, axis=-1, keepdims=True) + jnp.exp(sk - m)
            o = jnp.dot(e.astype(BF16), v3, preferred_element_type=F32) / denom
            o_ref[0, :, hd * HEAD_DIM:(hd + 1) * HEAD_DIM] = o


def _swa_attn(t5, sink, sq, sk, sv, pos_col, pos_row):
    B, S, _ = sq.shape
    nb = S // BLOCK
    kvw = SWA_KV_HEADS * HEAD_DIM
    prev = lambda b, n: (b, jnp.maximum(n - 1, 0), 0)
    cur = lambda b, n: (b, n, 0)
    nxt = lambda b, n: (b, jnp.minimum(n + 1, nb - 1), 0)
    rprev = lambda b, n: (b, 0, jnp.maximum(n - 1, 0))
    rcur = lambda b, n: (b, 0, n)
    rnxt = lambda b, n: (b, 0, jnp.minimum(n + 1, nb - 1))
    smem = pl.BlockSpec(memory_space=pltpu.SMEM)
    return pl.pallas_call(
        functools.partial(_swa_kernel, nb=nb),
        grid=(B, nb),
        in_specs=[
            _const_spec(t5.shape), smem,
            pl.BlockSpec((1, BLOCK, SWA_HEADS * HEAD_DIM), cur),
            pl.BlockSpec((1, BLOCK, kvw), prev), pl.BlockSpec((1, BLOCK, kvw), cur), pl.BlockSpec((1, BLOCK, kvw), nxt),
            pl.BlockSpec((1, BLOCK, kvw), prev), pl.BlockSpec((1, BLOCK, kvw), cur), pl.BlockSpec((1, BLOCK, kvw), nxt),
            pl.BlockSpec((1, BLOCK, 1), cur),
            pl.BlockSpec((1, 1, BLOCK), rprev), pl.BlockSpec((1, 1, BLOCK), rcur), pl.BlockSpec((1, 1, BLOCK), rnxt),
        ],
        out_specs=pl.BlockSpec((1, BLOCK, SWA_HEADS * HEAD_DIM), cur),
        out_shape=jax.ShapeDtypeStruct((B, S, SWA_HEADS * HEAD_DIM), F32),
        compiler_params=_params(("parallel", "parallel")), name="swa_attn",
    )(t5, sink, sq, sk, sk, sk, sv, sv, sv, pos_col, pos_row, pos_row, pos_row)


def _out_proj_kernel(x_ref, mla_ref, swa_ref, g1_ref, g2_ref, wo_ref, gf_ref, xo_ref, h2_ref):
    w = wo_ref.shape[0] // 2
    m1 = _rms(mla_ref[...], g1_ref[...]).astype(BF16)
    m2 = _rms(swa_ref[...], g2_ref[...]).astype(BF16)
    y = jnp.dot(m1, wo_ref[0:w, :], preferred_element_type=F32)
    y = y + jnp.dot(m2, wo_ref[w:2 * w, :], preferred_element_type=F32)
    xn = x_ref[...] + y
    xo_ref[...] = xn
    h2_ref[...] = _rms(xn, gf_ref[...]).astype(BF16)


def _out_proj(x, mla_o, swa_o, g1, g2, wo, gf, *, tm):
    T, D = x.shape
    W = mla_o.shape[1]
    row = lambda i: (i, 0)
    return pl.pallas_call(
        _out_proj_kernel,
        grid=(T // tm,),
        in_specs=[
            pl.BlockSpec((tm, D), row), pl.BlockSpec((tm, W), row), pl.BlockSpec((tm, W), row),
            _const_spec(g1.shape), _const_spec(g2.shape), _const_spec(wo.shape), _const_spec(gf.shape),
        ],
        out_specs=(pl.BlockSpec((tm, D), row), pl.BlockSpec((tm, D), row)),
        out_shape=(jax.ShapeDtypeStruct((T, D), F32), jax.ShapeDtypeStruct((T, D), BF16)),
        compiler_params=_params(("parallel",)), name="out_proj",
    )(x, mla_o, swa_o, g1, g2, wo, gf)


def _ffn_kernel(hp_ref, h_ref, hn_ref, wg_ref, wu_ref, cw_ref, cb_ref, wd_ref, y_ref, hext_ref, *,
                tm, tiles_per_seq):
    i = pl.program_id(0)
    j = pl.program_id(1)
    halo = BF16_SUBLANES

    @pl.when(j == 0)
    def _():
        t = i % tiles_per_seq
        hext_ref[0:halo, :] = jnp.where(t == 0, jnp.zeros_like(hp_ref[...]), hp_ref[...])
        hext_ref[halo:halo + tm, :] = h_ref[...]
        hext_ref[halo + tm:, :] = jnp.where(t == tiles_per_seq - 1, jnp.zeros_like(hn_ref[...]), hn_ref[...])

    gp = jnp.dot(hext_ref[...], wg_ref[...], preferred_element_type=F32)
    cw = cw_ref[...]
    g = (gp[halo - 1:halo - 1 + tm] * cw[0:1] + gp[halo:halo + tm] * cw[1:2]
         + gp[halo + 1:halo + 1 + tm] * cw[2:3] + cb_ref[...])
    u = jnp.dot(h_ref[...], wu_ref[...], preferred_element_type=F32)
    a = (g * _sigmoid(g) * u).astype(BF16)
    contrib = jnp.dot(a, wd_ref[...], preferred_element_type=F32)

    @pl.when(j == 0)
    def _():
        y_ref[...] = contrib

    @pl.when(j > 0)
    def _():
        y_ref[...] += contrib


def _ffn(h2, wg, wu, cw, cb, wd, *, tm, tf, seq):
    T, D = h2.shape
    Fp = wg.shape[1]
    halo = BF16_SUBLANES
    hb = tm // halo
    nhb = T // halo
    return pl.pallas_call(
        functools.partial(_ffn_kernel, tm=tm, tiles_per_seq=seq // tm),
        grid=(T // tm, Fp // tf),
        in_specs=[
            pl.BlockSpec((halo, D), lambda i, j: (jnp.maximum(i * hb - 1, 0), 0)),
            pl.BlockSpec((tm, D), lambda i, j: (i, 0)),
            pl.BlockSpec((halo, D), lambda i, j: (jnp.minimum((i + 1) * hb, nhb - 1), 0)),
            pl.BlockSpec((D, tf), lambda i, j: (0, j)),
            pl.BlockSpec((D, tf), lambda i, j: (0, j)),
            pl.BlockSpec((3, tf), lambda i, j: (0, j)),
            pl.BlockSpec((1, tf), lambda i, j: (0, j)),
            pl.BlockSpec((tf, D), lambda i, j: (j, 0)),
        ],
        out_specs=pl.BlockSpec((tm, D), lambda i, j: (i, 0)),
        out_shape=jax.ShapeDtypeStruct((T, D), F32),
        scratch_shapes=[pltpu.VMEM((tm + 2 * halo, D), BF16)],
        compiler_params=_params(("parallel", "arbitrary")), name="ffn",
    )(h2, h2, h2, wg, wu, cw, cb, wd)


def _ple_kernel(x_ref, y_ref, p_ref, wpg_ref, bpg_ref, wpp_ref, gfin_ref, o_ref, *, final):
    x2 = x_ref[...] + y_ref[...]
    gate = _sigmoid(jnp.dot(x2.astype(BF16), wpg_ref[...], preferred_element_type=F32) + bpg_ref[...])
    pp = jnp.dot(p_ref[...].astype(BF16), wpp_ref[...], preferred_element_type=F32)
    x3 = x2 + gate * pp
    if final:
        x3 = _rms(x3, gfin_ref[...])
    o_ref[...] = x3


def _ple(x, y, p, wpg, bpg, wpp, gfin, *, tm, final):
    T, D = x.shape
    P = p.shape[1]
    row = lambda i: (i, 0)
    return pl.pallas_call(
        functools.partial(_ple_kernel, final=final),
        grid=(T // tm,),
        in_specs=[
            pl.BlockSpec((tm, D), row), pl.BlockSpec((tm, D), row), pl.BlockSpec((tm, P), row),
            _const_spec(wpg.shape), _const_spec(bpg.shape), _const_spec(wpp.shape), _const_spec(gfin.shape),
        ],
        out_specs=pl.BlockSpec((tm, D), row),
        out_shape=jax.ShapeDtypeStruct((T, D), F32),
        compiler_params=_params(("parallel",)), name="ple",
    )(x, y, p, wpg, bpg, wpp, gfin)


def _prep_w_in(w_in):
    half = MLA_ROPE // 2
    kr0 = Q_RANK + KV_RANK
    t1, t2 = w_in[:, kr0:kr0 + half], w_in[:, kr0 + half:kr0 + MLA_ROPE]
    zpad = jnp.zeros((w_in.shape[0], LANES - MLA_ROPE), w_in.dtype)
    return jnp.concatenate(
        [w_in[:, :kr0], t1, t2, zpad, t2, t1, zpad, w_in[:, kr0 + MLA_ROPE:]], axis=1).astype(BF16)


def _prep_w_uq(w_uq):
    half = MLA_ROPE // 2
    w = w_uq.reshape(Q_RANK, MLA_HEADS, MLA_NOPE + MLA_ROPE)
    nope, t1, t2 = w[..., :MLA_NOPE], w[..., MLA_NOPE:MLA_NOPE + half], w[..., MLA_NOPE + half:]
    zpad = jnp.zeros((Q_RANK, MLA_HEADS, LANES - MLA_ROPE), w_uq.dtype)
    cols = jnp.concatenate([nope, t1, t2, zpad, t2, t1, zpad], axis=-1)
    return cols.reshape(Q_RANK, MLA_HEADS * 3 * LANES).T.astype(BF16)


def _prep_w_ukv(w_ukv):
    w = w_ukv.reshape(KV_RANK, MLA_HEADS, MLA_NOPE + MLA_V)
    wkn = w[..., :MLA_NOPE].reshape(KV_RANK, MLA_HEADS * MLA_NOPE).astype(BF16)
    wvt = w[..., MLA_NOPE:].reshape(KV_RANK, MLA_HEADS * MLA_V).T.astype(BF16)
    return wkn, wvt


def _pad_ff(w, axis, fp):
    pad = [(0, 0)] * w.ndim
    pad[axis] = (0, fp - w.shape[axis])
    return jnp.pad(w, pad)


def _pick_tile(n, pref):
    t = min(pref, n)
    while n % t:
        t //= 2
    return t


def kernel(x, p, positions, attn_norm, w_in, cq_norm, ckv_norm, w_uq, w_ukv, swa_sink, t5_bias,
           mla_out_norm, swa_out_norm, w_o, ffn_norm, w_gate, w_up, conv_w, conv_b, w_down,
           ple_gate_w, ple_gate_b, ple_proj, final_norm):
    B, S, D = x.shape
    depth = w_in.shape[0]
    T = B * S
    d_ff = w_gate.shape[-1]
    tf = 512
    fp = -(-d_ff // tf) * tf

    tm_in = _pick_tile(S, 512)
    tq = _pick_tile(S, 512)
    tk = _pick_tile(S, 512)
    tm_out = _pick_tile(S, 512)
    tm_ffn = _pick_tile(S, 1024)
    tm_ple = _pick_tile(S, 512)

    pos_row = positions.reshape(B, 1, S)
    pos_col = positions.reshape(B, S, 1)
    invf = (ROPE_THETA ** (-jnp.arange(0, MLA_ROPE, 2, dtype=F32) / MLA_ROPE)).reshape(MLA_ROPE // 2, 1)
    row = lambda v: v.reshape(1, -1)
    t5_tab = jnp.pad(t5_bias.T, ((0, 0), (0, LANES - NUM_BUCKETS)))

    xf = x
    for i in range(depth):
        wkn, wvt = _prep_w_ukv(w_ukv[i])
        qt, k, vt, sq, sk, sv = _in_proj(
            xf.reshape(B, S, D), pos_row, invf, row(attn_norm[i]), _prep_w_in(w_in[i]),
            row(cq_norm[i]), row(ckv_norm[i]), _prep_w_uq(w_uq[i]), wkn, wvt, tm=tm_in)
        mla_o = _mla_attn(qt, k, vt, tq=tq, tk=tk)
        swa_o = _swa_attn(t5_tab, row(swa_sink[i]), sq, sk, sv, pos_col, pos_row)
        x1, h2 = _out_proj(
            xf.reshape(T, D), mla_o.reshape(T, -1), swa_o.reshape(T, -1), row(mla_out_norm[i]),
            row(swa_out_norm[i]), w_o[i].astype(BF16), row(ffn_norm[i]), tm=tm_out)
        y = _ffn(
            h2, _pad_ff(w_gate[i], 1, fp).astype(BF16), _pad_ff(w_up[i], 1, fp).astype(BF16),
            _pad_ff(conv_w[i], 1, fp), _pad_ff(row(conv_b[i]), 1, fp),
            _pad_ff(w_down[i], 0, fp).astype(BF16), tm=tm_ffn, tf=tf, seq=S)
        xf = _ple(
            x1, y, p[i].reshape(T, -1), ple_gate_w[i].astype(BF16), row(ple_gate_b[i]),
            ple_proj[i].astype(BF16), row(final_norm), tm=tm_ple, final=(i == depth - 1))
    return xf.reshape(B, S, D)
```

```python
import functools
import math

import jax
import jax.numpy as jnp
from jax import lax
from jax.experimental import pallas as pl
from jax.experimental.pallas import tpu as pltpu

F32 = jnp.float32
BF16 = jnp.bfloat16

EPS = 1e-6
BLOCK = 128
MLA_HEADS = 8
MLA_NOPE = 128
MLA_ROPE = 64
MLA_V = 128
MLA_QK_PAD = 256
Q_RANK = 384
KV_RANK = 256
ROPE_THETA = 10000.0
SWA_HEADS = 8
SWA_KV_HEADS = 2
SWA_GROUP = SWA_HEADS // SWA_KV_HEADS
HEAD_DIM = 128
WINDOW = 128
NUM_BUCKETS = 32
NEG = -1e30
LOG2E = math.log2(math.e)

V7X_VMEM_BYTES = 64 * 1024 * 1024
VMEM_LIMIT_BYTES = V7X_VMEM_BYTES - 8 * 1024 * 1024
LANES = 128
BF16_SUBLANES = 16
MLA_V_EXT = MLA_V + BF16_SUBLANES

_Z_CQ = 0
_Z_CKV = _Z_CQ + Q_RANK
_Z_KA = _Z_CKV + KV_RANK
_Z_KB = _Z_KA + LANES
_Z_SQ = _Z_KB + LANES
_Z_SK = _Z_SQ + SWA_HEADS * HEAD_DIM
_Z_SV = _Z_SK + SWA_KV_HEADS * HEAD_DIM
_Z_END = _Z_SV + SWA_KV_HEADS * HEAD_DIM


def _params(sem):
    return pltpu.CompilerParams(dimension_semantics=sem, vmem_limit_bytes=VMEM_LIMIT_BYTES)


def _const_spec(shape):
    nd = len(shape)
    return pl.BlockSpec(shape, lambda *_: (0,) * nd, pipeline_mode=pl.Buffered(1))


def _rms(x, g):
    ms = jnp.mean(x * x, axis=-1, keepdims=True)
    return x * lax.rsqrt(ms + EPS) * g


def _sigmoid(x):
    return 1.0 / (1.0 + jnp.exp(-x))


def _in_proj_kernel(x_ref, pos_ref, invf_ref, g_ref, win_ref, cqn_ref, ckvn_ref, wq_ref, wkn_ref,
                    wvt_ref, qt_ref, k_ref, vt_ref, sq_ref, sk_ref, sv_ref, *, q_scale, swa_scale):
    x = x_ref[0]
    h = _rms(x, g_ref[...]).astype(BF16)
    z = jnp.dot(h, win_ref[...], preferred_element_type=F32)
    cq = _rms(z[:, _Z_CQ:_Z_CKV], cqn_ref[...]).astype(BF16)
    ckv = _rms(z[:, _Z_CKV:_Z_KA], ckvn_ref[...]).astype(BF16)

    ang = invf_ref[...] * pos_ref[0].astype(F32)
    cos_t, sin_t = jnp.cos(ang), jnp.sin(ang)
    zpad = jnp.zeros((LANES - MLA_ROPE, ang.shape[1]), F32)
    cc_t = jnp.concatenate([cos_t, cos_t, zpad], axis=0)
    ss_t = jnp.concatenate([-sin_t, sin_t, zpad], axis=0)

    r = lax.dot_general(wq_ref[...], cq, (((1,), (1,)), ((), ())), preferred_element_type=F32)
    for hd in range(MLA_HEADS):
        b = hd * 3 * LANES
        qt_ref[0, hd, 0:LANES, :] = (r[b:b + LANES] * q_scale).astype(BF16)
        roped = r[b + LANES:b + 2 * LANES] * cc_t + r[b + 2 * LANES:b + 3 * LANES] * ss_t
        qt_ref[0, hd, LANES:2 * LANES, :] = (roped * q_scale).astype(BF16)

    knope = jnp.dot(ckv, wkn_ref[...], preferred_element_type=F32)
    cc, ss = cc_t.T, ss_t.T
    krope = (z[:, _Z_KA:_Z_KB] * cc + z[:, _Z_KB:_Z_SQ] * ss).astype(BF16)
    for hd in range(MLA_HEADS):
        k_ref[0, hd, :, 0:LANES] = knope[:, hd * LANES:(hd + 1) * LANES].astype(BF16)
        k_ref[0, hd, :, LANES:MLA_QK_PAD] = krope

    vt = lax.dot_general(wvt_ref[...], ckv, (((1,), (1,)), ((), ())), preferred_element_type=F32)
    ones_row = (lax.broadcasted_iota(jnp.int32, (BF16_SUBLANES, vt.shape[1]), 0) == 0).astype(BF16)
    for hd in range(MLA_HEADS):
        vt_ref[0, hd, 0:MLA_V, :] = vt[hd * MLA_V:(hd + 1) * MLA_V].astype(BF16)
        vt_ref[0, hd, MLA_V:MLA_V_EXT, :] = ones_row

    sq_ref[0] = (z[:, _Z_SQ:_Z_SK] * swa_scale).astype(BF16)
    sk_ref[0] = z[:, _Z_SK:_Z_SV].astype(BF16)
    sv_ref[0] = z[:, _Z_SV:_Z_END].astype(BF16)


def _in_proj(x, pos_row, invf, g, win, cqn, ckvn, wq, wkn, wvt, *, tm):
    B, S, D = x.shape
    grid = (B, S // tm)
    q_scale = LOG2E / math.sqrt(MLA_NOPE + MLA_ROPE)
    swa_scale = 1.0 / math.sqrt(HEAD_DIM)
    out_shape = (
        jax.ShapeDtypeStruct((B, MLA_HEADS, MLA_QK_PAD, S), BF16),
        jax.ShapeDtypeStruct((B, MLA_HEADS, S, MLA_QK_PAD), BF16),
        jax.ShapeDtypeStruct((B, MLA_HEADS, MLA_V_EXT, S), BF16),
        jax.ShapeDtypeStruct((B, S, SWA_HEADS * HEAD_DIM), BF16),
        jax.ShapeDtypeStruct((B, S, SWA_KV_HEADS * HEAD_DIM), BF16),
        jax.ShapeDtypeStruct((B, S, SWA_KV_HEADS * HEAD_DIM), BF16),
    )
    in_specs = [
        pl.BlockSpec((1, tm, D), lambda b, i: (b, i, 0)),
        pl.BlockSpec((1, 1, tm), lambda b, i: (b, 0, i)),
        _const_spec(invf.shape), _const_spec(g.shape), _const_spec(win.shape),
        _const_spec(cqn.shape), _const_spec(ckvn.shape), _const_spec(wq.shape),
        _const_spec(wkn.shape), _const_spec(wvt.shape),
    ]
    out_specs = (
        pl.BlockSpec((1, MLA_HEADS, MLA_QK_PAD, tm), lambda b, i: (b, 0, 0, i)),
        pl.BlockSpec((1, MLA_HEADS, tm, MLA_QK_PAD), lambda b, i: (b, 0, i, 0)),
        pl.BlockSpec((1, MLA_HEADS, MLA_V_EXT, tm), lambda b, i: (b, 0, 0, i)),
        pl.BlockSpec((1, tm, SWA_HEADS * HEAD_DIM), lambda b, i: (b, i, 0)),
        pl.BlockSpec((1, tm, SWA_KV_HEADS * HEAD_DIM), lambda b, i: (b, i, 0)),
        pl.BlockSpec((1, tm, SWA_KV_HEADS * HEAD_DIM), lambda b, i: (b, i, 0)),
    )
    return pl.pallas_call(
        functools.partial(_in_proj_kernel, q_scale=q_scale, swa_scale=swa_scale),
        grid=grid, in_specs=in_specs, out_specs=out_specs, out_shape=out_shape,
        compiler_params=_params(("parallel", "parallel")), name="in_proj",
    )(x, pos_row, invf, g, win, cqn, ckvn, wq, wkn, wvt)


def _mla_kernel(qt_ref, k_ref, vt_ref, o_ref, m_ref, acc_ref, sa_ref, sb_ref, *, tq, tk, unroll):
    S = k_ref.shape[2]
    n = S // tk
    bufs = (sa_ref, sb_ref)

    def q_tile(t, carry):
        q0 = pl.multiple_of(t * tq, tq)
        m_ref[...] = jnp.full(m_ref.shape, NEG, F32)
        acc_ref[...] = jnp.zeros(acc_ref.shape, F32)

        def scores(c, s_ref):
            start = pl.multiple_of(c * tk, tk)
            s_ref[...] = jnp.dot(k_ref[0, 0, pl.ds(start, tk), :], qt_ref[0, 0, :, pl.ds(q0, tq)],
                                 preferred_element_type=F32)

        def softmax_pv(c, s_ref):
            start = pl.multiple_of(c * tk, tk)
            s = s_ref[...]
            m_prev = m_ref[...]
            m_new = jnp.maximum(m_prev, jnp.max(s, axis=0, keepdims=True))
            alpha = jnp.exp2(m_prev - m_new)
            p = jnp.exp2(s - m_new).astype(BF16)
            vc = vt_ref[0, 0, :, pl.ds(start, tk)]
            acc_ref[...] = alpha * acc_ref[...] + jnp.dot(vc, p, preferred_element_type=F32)
            m_ref[...] = m_new

        scores(0, sa_ref)

        def group(i, carry):
            c0 = i * unroll
            for u in range(unroll):
                nxt = c0 + u + 1
                if u == unroll - 1:
                    nxt = jnp.where(nxt == n, 0, nxt)
                scores(nxt, bufs[(u + 1) % 2])
                softmax_pv(c0 + u, bufs[u % 2])
            return carry

        lax.fori_loop(0, n // unroll, group, 0)
        acc = acc_ref[...]
        o_ref[0, pl.ds(q0, tq), :] = (acc[0:MLA_V] / acc[MLA_V:MLA_V + 1]).T
        return carry

    lax.fori_loop(0, qt_ref.shape[3] // tq, q_tile, 0)


def _mla_attn(qt, k, vt, *, tq, tk, q_tiles_per_step):
    B, H, _, S = qt.shape
    vrows = vt.shape[2]
    n = S // tk
    unroll = max(u for u in (2, 4, 8) if n % u == 0)
    tqs = tq * q_tiles_per_step
    grid = (B, H, S // tqs)
    return pl.pallas_call(
        functools.partial(_mla_kernel, tq=tq, tk=tk, unroll=unroll),
        grid=grid,
        in_specs=[
            pl.BlockSpec((1, 1, MLA_QK_PAD, tqs), lambda b, h, i: (b, h, 0, i)),
            pl.BlockSpec((1, 1, S, MLA_QK_PAD), lambda b, h, i: (b, h, 0, 0)),
            pl.BlockSpec((1, 1, vrows, S), lambda b, h, i: (b, h, 0, 0)),
        ],
        out_specs=pl.BlockSpec((1, tqs, MLA_V), lambda b, h, i: (b, i, h)),
        out_shape=jax.ShapeDtypeStruct((B, S, H * MLA_V), F32),
        scratch_shapes=[pltpu.VMEM((1, tq), F32), pltpu.VMEM((vrows, tq), F32),
                        pltpu.VMEM((tk, tq), F32), pltpu.VMEM((tk, tq), F32)],
        compiler_params=_params(("parallel", "parallel", "arbitrary")), name="mla_attn",
    )(qt, k, vt)


_T5_STEPS = (12, 16, 23, 32, 46, 64, 91, 128)


def _swa_kernel(t5_ref, sink_ref, q_ref, kp_ref, kc_ref, kn_ref, vp_ref, vc_ref, vn_ref,
                pq_ref, pp_ref, pc_ref, pn_ref, o_ref, *, nb):
    n = pl.program_id(1)
    pq = pq_ref[0]
    pk = jnp.concatenate([pp_ref[0], pc_ref[0], pn_ref[0]], axis=1)
    rel = pk - pq
    col = lax.broadcasted_iota(jnp.int32, rel.shape, 1)
    lo = jnp.where(n > 0, 0, BLOCK)
    hi = jnp.where(n < nb - 1, 3 * BLOCK, 2 * BLOCK)
    na = jnp.abs(rel)
    mask = (na <= WINDOW) & (col >= lo) & (col < hi)
    large = jnp.full(na.shape, NUM_BUCKETS // 4, jnp.int32)
    for t in _T5_STEPS:
        large = large + (na >= t).astype(jnp.int32)
    large = jnp.minimum(large, NUM_BUCKETS // 2 - 1)
    bkt = jnp.where(rel > 0, NUM_BUCKETS // 2, 0) + jnp.where(na < NUM_BUCKETS // 4, na, large)
    bkts = [bkt[:, j * BLOCK:(j + 1) * BLOCK] for j in range(3)]

    q = q_ref[0]
    for g in range(SWA_KV_HEADS):
        lo, hi = g * HEAD_DIM, (g + 1) * HEAD_DIM
        k3 = jnp.concatenate([kp_ref[0, :, lo:hi], kc_ref[0, :, lo:hi], kn_ref[0, :, lo:hi]], axis=0)
        v3 = jnp.concatenate([vp_ref[0, :, lo:hi], vc_ref[0, :, lo:hi], vn_ref[0, :, lo:hi]], axis=0)
        for j in range(SWA_GROUP):
            hd = g * SWA_GROUP + j
            tab = jnp.broadcast_to(t5_ref[hd:hd + 1, :], (BLOCK, LANES))
            bias = jnp.concatenate([jnp.take_along_axis(tab, b, axis=1) for b in bkts], axis=1)
            qh = q[:, hd * HEAD_DIM:(hd + 1) * HEAD_DIM]
            s = lax.dot_general(qh, k3, (((1,), (1,)), ((), ())), preferred_element_type=F32) + bias
            s = jnp.where(mask, s, NEG)
            sk = sink_ref[0, hd]
            m = jnp.maximum(jnp.max(s, axis=-1, keepdims=True), sk)
            e = jnp.exp(s - m)
            denom = jnp.sum(e, axis=-1, keepdims=True) + jnp.exp(sk - m)
            o = jnp.dot(e.astype(BF16), v3, preferred_element_type=F32) / denom
            o_ref[0, :, hd * HEAD_DIM:(hd + 1) * HEAD_DIM] = o


def _swa_attn(t5, sink, sq, sk, sv, pos_col, pos_row):
    B, S, _ = sq.shape
    nb = S // BLOCK
    kvw = SWA_KV_HEADS * HEAD_DIM
    prev = lambda b, n: (b, jnp.maximum(n - 1, 0), 0)
    cur = lambda b, n: (b, n, 0)
    nxt = lambda b, n: (b, jnp.minimum(n + 1, nb - 1), 0)
    rprev = lambda b, n: (b, 0, jnp.maximum(n - 1, 0))
    rcur = lambda b, n: (b, 0, n)
    rnxt = lambda b, n: (b, 0, jnp.minimum(n + 1, nb - 1))
    smem = pl.BlockSpec(memory_space=pltpu.SMEM)
    return pl.pallas_call(
        functools.partial(_swa_kernel, nb=nb),
        grid=(B, nb),
        in_specs=[
            _const_spec(t5.shape), smem,
            pl.BlockSpec((1, BLOCK, SWA_HEADS * HEAD_DIM), cur),
            pl.BlockSpec((1, BLOCK, kvw), prev), pl.BlockSpec((1, BLOCK, kvw), cur), pl.BlockSpec((1, BLOCK, kvw), nxt),
            pl.BlockSpec((1, BLOCK, kvw), prev), pl.BlockSpec((1, BLOCK, kvw), cur), pl.BlockSpec((1, BLOCK, kvw), nxt),
            pl.BlockSpec((1, BLOCK, 1), cur),
            pl.BlockSpec((1, 1, BLOCK), rprev), pl.BlockSpec((1, 1, BLOCK), rcur), pl.BlockSpec((1, 1, BLOCK), rnxt),
        ],
        out_specs=pl.BlockSpec((1, BLOCK, SWA_HEADS * HEAD_DIM), cur),
        out_shape=jax.ShapeDtypeStruct((B, S, SWA_HEADS * HEAD_DIM), F32),
        compiler_params=_params(("parallel", "parallel")), name="swa_attn",
    )(t5, sink, sq, sk, sk, sk, sv, sv, sv, pos_col, pos_row, pos_row, pos_row)


def _out_proj_kernel(x_ref, mla_ref, swa_ref, g1_ref, g2_ref, wo_ref, gf_ref, xo_ref, h2_ref):
    w = wo_ref.shape[0] // 2
    m1 = _rms(mla_ref[...], g1_ref[...]).astype(BF16)
    m2 = _rms(swa_ref[...], g2_ref[...]).astype(BF16)
    y = jnp.dot(m1, wo_ref[0:w, :], preferred_element_type=F32)
    y = y + jnp.dot(m2, wo_ref[w:2 * w, :], preferred_element_type=F32)
    xn = x_ref[...] + y
    xo_ref[...] = xn
    h2_ref[...] = _rms(xn, gf_ref[...]).astype(BF16)


def _out_proj(x, mla_o, swa_o, g1, g2, wo, gf, *, tm):
    T, D = x.shape
    W = mla_o.shape[1]
    row = lambda i: (i, 0)
    return pl.pallas_call(
        _out_proj_kernel,
        grid=(T // tm,),
        in_specs=[
            pl.BlockSpec((tm, D), row), pl.BlockSpec((tm, W), row), pl.BlockSpec((tm, W), row),
            _const_spec(g1.shape), _const_spec(g2.shape), _const_spec(wo.shape), _const_spec(gf.shape),
        ],
        out_specs=(pl.BlockSpec((tm, D), row), pl.BlockSpec((tm, D), row)),
        out_shape=(jax.ShapeDtypeStruct((T, D), F32), jax.ShapeDtypeStruct((T, D), BF16)),
        compiler_params=_params(("parallel",)), name="out_proj",
    )(x, mla_o, swa_o, g1, g2, wo, gf)


def _ffn_kernel(hp_ref, h_ref, hn_ref, wg_ref, wu_ref, cw_ref, cb_ref, wd_ref, y_ref, hext_ref, *,
                tm, tiles_per_seq):
    i = pl.program_id(0)
    j = pl.program_id(1)
    halo = BF16_SUBLANES

    @pl.when(j == 0)
    def _():
        t = i % tiles_per_seq
        hext_ref[0:halo, :] = jnp.where(t == 0, jnp.zeros_like(hp_ref[...]), hp_ref[...])
        hext_ref[halo:halo + tm, :] = h_ref[...]
        hext_ref[halo + tm:, :] = jnp.where(t == tiles_per_seq - 1, jnp.zeros_like(hn_ref[...]), hn_ref[...])
        y_ref[...] = jnp.zeros_like(y_ref)

    gp = jnp.dot(hext_ref[...], wg_ref[...], preferred_element_type=F32)
    cw = cw_ref[...]
    g = (gp[halo - 1:halo - 1 + tm] * cw[0:1] + gp[halo:halo + tm] * cw[1:2]
         + gp[halo + 1:halo + 1 + tm] * cw[2:3] + cb_ref[...])
    u = jnp.dot(h_ref[...], wu_ref[...], preferred_element_type=F32)
    a = (g * _sigmoid(g) * u).astype(BF16)
    y_ref[...] += jnp.dot(a, wd_ref[...], preferred_element_type=F32)


def _ffn(h2, wg, wu, cw, cb, wd, *, tm, tf, seq):
    T, D = h2.shape
    Fp = wg.shape[1]
    halo = BF16_SUBLANES
    hb = tm // halo
    nhb = T // halo
    return pl.pallas_call(
        functools.partial(_ffn_kernel, tm=tm, tiles_per_seq=seq // tm),
        grid=(T // tm, Fp // tf),
        in_specs=[
            pl.BlockSpec((halo, D), lambda i, j: (jnp.maximum(i * hb - 1, 0), 0)),
            pl.BlockSpec((tm, D), lambda i, j: (i, 0)),
            pl.BlockSpec((halo, D), lambda i, j: (jnp.minimum((i + 1) * hb, nhb - 1), 0)),
            pl.BlockSpec((D, tf), lambda i, j: (0, j)),
            pl.BlockSpec((D, tf), lambda i, j: (0, j)),
            pl.BlockSpec((3, tf), lambda i, j: (0, j)),
            pl.BlockSpec((1, tf), lambda i, j: (0, j)),
            pl.BlockSpec((tf, D), lambda i, j: (j, 0)),
        ],
        out_specs=pl.BlockSpec((tm, D), lambda i, j: (i, 0)),
        out_shape=jax.ShapeDtypeStruct((T, D), F32),
        scratch_shapes=[pltpu.VMEM((tm + 2 * halo, D), BF16)],
        compiler_params=_params(("parallel", "arbitrary")), name="ffn",
    )(h2, h2, h2, wg, wu, cw, cb, wd)


def _ple_kernel(x_ref, y_ref, p_ref, wpg_ref, bpg_ref, wpp_ref, gfin_ref, o_ref, *, final):
    x2 = x_ref[...] + y_ref[...]
    gate = _sigmoid(jnp.dot(x2.astype(BF16), wpg_ref[...], preferred_element_type=F32) + bpg_ref[...])
    pp = jnp.dot(p_ref[...].astype(BF16), wpp_ref[...], preferred_element_type=F32)
    x3 = x2 + gate * pp
    if final:
        x3 = _rms(x3, gfin_ref[...])
    o_ref[...] = x3


def _ple(x, y, p, wpg, bpg, wpp, gfin, *, tm, final):
    T, D = x.shape
    P = p.shape[1]
    row = lambda i: (i, 0)
    return pl.pallas_call(
        functools.partial(_ple_kernel, final=final),
        grid=(T // tm,),
        in_specs=[
            pl.BlockSpec((tm, D), row), pl.BlockSpec((tm, D), row), pl.BlockSpec((tm, P), row),
            _const_spec(wpg.shape), _const_spec(bpg.shape), _const_spec(wpp.shape), _const_spec(gfin.shape),
        ],
        out_specs=pl.BlockSpec((tm, D), row),
        out_shape=jax.ShapeDtypeStruct((T, D), F32),
        compiler_params=_params(("parallel",)), name="ple",
    )(x, y, p, wpg, bpg, wpp, gfin)


def _prep_w_in(w_in):
    half = MLA_ROPE // 2
    kr0 = Q_RANK + KV_RANK
    t1, t2 = w_in[:, kr0:kr0 + half], w_in[:, kr0 + half:kr0 + MLA_ROPE]
    zpad = jnp.zeros((w_in.shape[0], LANES - MLA_ROPE), w_in.dtype)
    return jnp.concatenate(
        [w_in[:, :kr0], t1, t2, zpad, t2, t1, zpad, w_in[:, kr0 + MLA_ROPE:]], axis=1).astype(BF16)


def _prep_w_uq(w_uq):
    half = MLA_ROPE // 2
    w = w_uq.reshape(Q_RANK, MLA_HEADS, MLA_NOPE + MLA_ROPE)
    nope, t1, t2 = w[..., :MLA_NOPE], w[..., MLA_NOPE:MLA_NOPE + half], w[..., MLA_NOPE + half:]
    zpad = jnp.zeros((Q_RANK, MLA_HEADS, LANES - MLA_ROPE), w_uq.dtype)
    cols = jnp.concatenate([nope, t1, t2, zpad, t2, t1, zpad], axis=-1)
    return cols.reshape(Q_RANK, MLA_HEADS * 3 * LANES).T.astype(BF16)


def _prep_w_ukv(w_ukv):
    w = w_ukv.reshape(KV_RANK, MLA_HEADS, MLA_NOPE + MLA_V)
    wkn = w[..., :MLA_NOPE].reshape(KV_RANK, MLA_HEADS * MLA_NOPE).astype(BF16)
    wvt = w[..., MLA_NOPE:].reshape(KV_RANK, MLA_HEADS * MLA_V).T.astype(BF16)
    return wkn, wvt


def _pad_ff(w, axis, fp):
    pad = [(0, 0)] * w.ndim
    pad[axis] = (0, fp - w.shape[axis])
    return jnp.pad(w, pad)


def _pick_tile(n, pref):
    t = min(pref, n)
    while n % t:
        t //= 2
    return t


def kernel(x, p, positions, attn_norm, w_in, cq_norm, ckv_norm, w_uq, w_ukv, swa_sink, t5_bias,
           mla_out_norm, swa_out_norm, w_o, ffn_norm, w_gate, w_up, conv_w, conv_b, w_down,
           ple_gate_w, ple_gate_b, ple_proj, final_norm):
    B, S, D = x.shape
    depth = w_in.shape[0]
    T = B * S
    d_ff = w_gate.shape[-1]
    tf = 512
    fp = -(-d_ff // tf) * tf

    tm_in = _pick_tile(S, 512)
    tq = _pick_tile(S, 512)
    tk = _pick_tile(S, 512)
    tm_out = _pick_tile(S, 512)
    tm_ffn = _pick_tile(S, 1024)
    tm_ple = _pick_tile(S, 512)

    pos_row = positions.reshape(B, 1, S)
    pos_col = positions.reshape(B, S, 1)
    invf = (ROPE_THETA ** (-jnp.arange(0, MLA_ROPE, 2, dtype=F32) / MLA_ROPE)).reshape(MLA_ROPE // 2, 1)
    row = lambda v: v.reshape(1, -1)
    t5_tab = jnp.pad(t5_bias.T, ((0, 0), (0, LANES - NUM_BUCKETS)))

    xf = x
    for i in range(depth):
        wkn, wvt = _prep_w_ukv(w_ukv[i])
        qt, k, vt, sq, sk, sv = _in_proj(
            xf.reshape(B, S, D), pos_row, invf, row(attn_norm[i]), _prep_w_in(w_in[i]),
            row(cq_norm[i]), row(ckv_norm[i]), _prep_w_uq(w_uq[i]), wkn, wvt, tm=tm_in)
        mla_o = _mla_attn(qt, k, vt, tq=tq, tk=tk, q_tiles_per_step=2 if S % (2 * tq) == 0 else 1)
        swa_o = _swa_attn(t5_tab, row(swa_sink[i]), sq, sk, sv, pos_col, pos_row)
        x1, h2 = _out_proj(
            xf.reshape(T, D), mla_o.reshape(T, -1), swa_o.reshape(T, -1), row(mla_out_norm[i]),
            row(swa_out_norm[i]), w_o[i].astype(BF16), row(ffn_norm[i]), tm=tm_out)
        y = _ffn(
            h2, _pad_ff(w_gate[i], 1, fp).astype(BF16), _pad_ff(w_up[i], 1, fp).astype(BF16),
            _pad_ff(conv_w[i], 1, fp), _pad_ff(row(conv_b[i]), 1, fp),
            _pad_ff(w_down[i], 0, fp).astype(BF16), tm=tm_ffn, tf=tf, seq=S)
        xf = _ple(
            x1, y, p[i].reshape(T, -1), ple_gate_w[i].astype(BF16), row(ple_gate_b[i]),
            ple_proj[i].astype(BF16), row(final_norm), tm=tm_ple, final=(i == depth - 1))
    return xf.reshape(B, S, D)
```

```python
import functools
import math

import jax
import jax.numpy as jnp
from jax import lax
from jax.experimental import pallas as pl
from jax.experimental.pallas import tpu as pltpu

F32 = jnp.float32
BF16 = jnp.bfloat16

EPS = 1e-6
BLOCK = 128
MLA_HEADS = 8
MLA_NOPE = 128
MLA_ROPE = 64
MLA_V = 128
MLA_QK_PAD = 256
Q_RANK = 384
KV_RANK = 256
ROPE_THETA = 10000.0
SWA_HEADS = 8
SWA_KV_HEADS = 2
SWA_GROUP = SWA_HEADS // SWA_KV_HEADS
HEAD_DIM = 128
WINDOW = 128
NUM_BUCKETS = 32
NEG = -1e30
LOG2E = math.log2(math.e)

V7X_VMEM_BYTES = 64 * 1024 * 1024
VMEM_LIMIT_BYTES = V7X_VMEM_BYTES - 8 * 1024 * 1024
LANES = 128
BF16_SUBLANES = 16
MLA_V_EXT = MLA_V + BF16_SUBLANES

_Z_CQ = 0
_Z_CKV = _Z_CQ + Q_RANK
_Z_KA = _Z_CKV + KV_RANK
_Z_KB = _Z_KA + LANES
_Z_SQ = _Z_KB + LANES
_Z_SK = _Z_SQ + SWA_HEADS * HEAD_DIM
_Z_SV = _Z_SK + SWA_KV_HEADS * HEAD_DIM
_Z_END = _Z_SV + SWA_KV_HEADS * HEAD_DIM


def _params(sem):
    return pltpu.CompilerParams(dimension_semantics=sem, vmem_limit_bytes=VMEM_LIMIT_BYTES)


def _const_spec(shape):
    nd = len(shape)
    return pl.BlockSpec(shape, lambda *_: (0,) * nd, pipeline_mode=pl.Buffered(1))


def _rms(x, g):
    ms = jnp.mean(x * x, axis=-1, keepdims=True)
    return x * lax.rsqrt(ms + EPS) * g


def _sigmoid(x):
    return 1.0 / (1.0 + jnp.exp(-x))


def _in_proj_kernel(x_ref, pos_ref, invf_ref, g_ref, win_ref, cqn_ref, ckvn_ref, wq_ref, wkn_ref,
                    wvt_ref, qt_ref, k_ref, vt_ref, sq_ref, sk_ref, sv_ref, *, q_scale, swa_scale):
    x = x_ref[0]
    h = _rms(x, g_ref[...]).astype(BF16)
    z = jnp.dot(h, win_ref[...], preferred_element_type=F32)
    cq = _rms(z[:, _Z_CQ:_Z_CKV], cqn_ref[...]).astype(BF16)
    ckv = _rms(z[:, _Z_CKV:_Z_KA], ckvn_ref[...]).astype(BF16)

    ang = invf_ref[...] * pos_ref[0].astype(F32)
    cos_t, sin_t = jnp.cos(ang), jnp.sin(ang)
    zpad = jnp.zeros((LANES - MLA_ROPE, ang.shape[1]), F32)
    cc_t = jnp.concatenate([cos_t, cos_t, zpad], axis=0)
    ss_t = jnp.concatenate([-sin_t, sin_t, zpad], axis=0)

    r = lax.dot_general(wq_ref[...], cq, (((1,), (1,)), ((), ())), preferred_element_type=F32)
    for hd in range(MLA_HEADS):
        b = hd * 3 * LANES
        qt_ref[0, hd, 0:LANES, :] = (r[b:b + LANES] * q_scale).astype(BF16)
        roped = r[b + LANES:b + 2 * LANES] * cc_t + r[b + 2 * LANES:b + 3 * LANES] * ss_t
        qt_ref[0, hd, LANES:2 * LANES, :] = (roped * q_scale).astype(BF16)

    knope = jnp.dot(ckv, wkn_ref[...], preferred_element_type=F32)
    cc, ss = cc_t.T, ss_t.T
    krope = (z[:, _Z_KA:_Z_KB] * cc + z[:, _Z_KB:_Z_SQ] * ss).astype(BF16)
    for hd in range(MLA_HEADS):
        k_ref[0, hd, :, 0:LANES] = knope[:, hd * LANES:(hd + 1) * LANES].astype(BF16)
        k_ref[0, hd, :, LANES:MLA_QK_PAD] = krope

    vt = lax.dot_general(wvt_ref[...], ckv, (((1,), (1,)), ((), ())), preferred_element_type=F32)
    ones_row = (lax.broadcasted_iota(jnp.int32, (BF16_SUBLANES, vt.shape[1]), 0) == 0).astype(BF16)
    for hd in range(MLA_HEADS):
        vt_ref[0, hd, 0:MLA_V, :] = vt[hd * MLA_V:(hd + 1) * MLA_V].astype(BF16)
        vt_ref[0, hd, MLA_V:MLA_V_EXT, :] = ones_row

    sq_ref[0] = (z[:, _Z_SQ:_Z_SK] * swa_scale).astype(BF16)
    sk_ref[0] = z[:, _Z_SK:_Z_SV].astype(BF16)
    sv_ref[0] = z[:, _Z_SV:_Z_END].astype(BF16)


def _in_proj(x, pos_row, invf, g, win, cqn, ckvn, wq, wkn, wvt, *, tm):
    B, S, D = x.shape
    grid = (B, S // tm)
    q_scale = LOG2E / math.sqrt(MLA_NOPE + MLA_ROPE)
    swa_scale = 1.0 / math.sqrt(HEAD_DIM)
    out_shape = (
        jax.ShapeDtypeStruct((B, MLA_HEADS, MLA_QK_PAD, S), BF16),
        jax.ShapeDtypeStruct((B, MLA_HEADS, S, MLA_QK_PAD), BF16),
        jax.ShapeDtypeStruct((B, MLA_HEADS, MLA_V_EXT, S), BF16),
        jax.ShapeDtypeStruct((B, S, SWA_HEADS * HEAD_DIM), BF16),
        jax.ShapeDtypeStruct((B, S, SWA_KV_HEADS * HEAD_DIM), BF16),
        jax.ShapeDtypeStruct((B, S, SWA_KV_HEADS * HEAD_DIM), BF16),
    )
    in_specs = [
        pl.BlockSpec((1, tm, D), lambda b, i: (b, i, 0)),
        pl.BlockSpec((1, 1, tm), lambda b, i: (b, 0, i)),
        _const_spec(invf.shape), _const_spec(g.shape), _const_spec(win.shape),
        _const_spec(cqn.shape), _const_spec(ckvn.shape), _const_spec(wq.shape),
        _const_spec(wkn.shape), _const_spec(wvt.shape),
    ]
    out_specs = (
        pl.BlockSpec((1, MLA_HEADS, MLA_QK_PAD, tm), lambda b, i: (b, 0, 0, i)),
        pl.BlockSpec((1, MLA_HEADS, tm, MLA_QK_PAD), lambda b, i: (b, 0, i, 0)),
        pl.BlockSpec((1, MLA_HEADS, MLA_V_EXT, tm), lambda b, i: (b, 0, 0, i)),
        pl.BlockSpec((1, tm, SWA_HEADS * HEAD_DIM), lambda b, i: (b, i, 0)),
        pl.BlockSpec((1, tm, SWA_KV_HEADS * HEAD_DIM), lambda b, i: (b, i, 0)),
        pl.BlockSpec((1, tm, SWA_KV_HEADS * HEAD_DIM), lambda b, i: (b, i, 0)),
    )
    return pl.pallas_call(
        functools.partial(_in_proj_kernel, q_scale=q_scale, swa_scale=swa_scale),
        grid=grid, in_specs=in_specs, out_specs=out_specs, out_shape=out_shape,
        compiler_params=_params(("parallel", "parallel")), name="in_proj",
    )(x, pos_row, invf, g, win, cqn, ckvn, wq, wkn, wvt)


def _mla_kernel(qt_ref, k_ref, vt_ref, o_ref, m_ref, acc_ref, sa_ref, sb_ref, mca_ref, mcb_ref, *,
                tq, tk, unroll):
    S = k_ref.shape[2]
    n = S // tk
    bufs = ((sa_ref, mca_ref), (sb_ref, mcb_ref))

    def q_tile(t, carry):
        q0 = pl.multiple_of(t * tq, tq)
        m_ref[...] = jnp.full(m_ref.shape, NEG, F32)
        acc_ref[...] = jnp.zeros(acc_ref.shape, F32)

        def scores(c, buf):
            s_ref, mc_ref = buf
            start = pl.multiple_of(c * tk, tk)
            s = jnp.dot(k_ref[0, 0, pl.ds(start, tk), :], qt_ref[0, 0, :, pl.ds(q0, tq)],
                        preferred_element_type=F32)
            s_ref[...] = s
            mc_ref[...] = jnp.max(s, axis=0, keepdims=True)

        def softmax_pv(c, buf):
            s_ref, mc_ref = buf
            start = pl.multiple_of(c * tk, tk)
            m_prev = m_ref[...]
            m_new = jnp.maximum(m_prev, mc_ref[...])
            alpha = jnp.exp2(m_prev - m_new)
            p = jnp.exp2(s_ref[...] - m_new).astype(BF16)
            vc = vt_ref[0, 0, :, pl.ds(start, tk)]
            acc_ref[...] = alpha * acc_ref[...] + jnp.dot(vc, p, preferred_element_type=F32)
            m_ref[...] = m_new

        scores(0, bufs[0])

        def group(i, carry):
            c0 = i * unroll
            for u in range(unroll):
                nxt = c0 + u + 1
                if u == unroll - 1:
                    nxt = jnp.where(nxt == n, 0, nxt)
                scores(nxt, bufs[(u + 1) % 2])
                softmax_pv(c0 + u, bufs[u % 2])
            return carry

        lax.fori_loop(0, n // unroll, group, 0)
        acc = acc_ref[...]
        o_ref[0, pl.ds(q0, tq), :] = (acc[0:MLA_V] / acc[MLA_V:MLA_V + 1]).T
        return carry

    lax.fori_loop(0, qt_ref.shape[3] // tq, q_tile, 0)


def _mla_attn(qt, k, vt, *, tq, tk, q_tiles_per_step):
    B, H, _, S = qt.shape
    vrows = vt.shape[2]
    n = S // tk
    unroll = max(u for u in (2, 4, 8) if n % u == 0)
    tqs = tq * q_tiles_per_step
    grid = (B, H, S // tqs)
    return pl.pallas_call(
        functools.partial(_mla_kernel, tq=tq, tk=tk, unroll=unroll),
        grid=grid,
        in_specs=[
            pl.BlockSpec((1, 1, MLA_QK_PAD, tqs), lambda b, h, i: (b, h, 0, i)),
            pl.BlockSpec((1, 1, S, MLA_QK_PAD), lambda b, h, i: (b, h, 0, 0)),
            pl.BlockSpec((1, 1, vrows, S), lambda b, h, i: (b, h, 0, 0)),
        ],
        out_specs=pl.BlockSpec((1, tqs, MLA_V), lambda b, h, i: (b, i, h)),
        out_shape=jax.ShapeDtypeStruct((B, S, H * MLA_V), F32),
        scratch_shapes=[pltpu.VMEM((1, tq), F32), pltpu.VMEM((vrows, tq), F32),
                        pltpu.VMEM((tk, tq), F32), pltpu.VMEM((tk, tq), F32),
                        pltpu.VMEM((1, tq), F32), pltpu.VMEM((1, tq), F32)],
        compiler_params=_params(("parallel", "parallel", "arbitrary")), name="mla_attn",
    )(qt, k, vt)


_T5_STEPS = (12, 16, 23, 32, 46, 64, 91, 128)


def _swa_kernel(t5_ref, sink_ref, q_ref, kp_ref, kc_ref, kn_ref, vp_ref, vc_ref, vn_ref,
                pq_ref, pp_ref, pc_ref, pn_ref, o_ref, *, nb):
    n = pl.program_id(1)
    pq = pq_ref[0]
    pk = jnp.concatenate([pp_ref[0], pc_ref[0], pn_ref[0]], axis=1)
    rel = pk - pq
    col = lax.broadcasted_iota(jnp.int32, rel.shape, 1)
    lo = jnp.where(n > 0, 0, BLOCK)
    hi = jnp.where(n < nb - 1, 3 * BLOCK, 2 * BLOCK)
    na = jnp.abs(rel)
    mask = (na <= WINDOW) & (col >= lo) & (col < hi)
    large = jnp.full(na.shape, NUM_BUCKETS // 4, jnp.int32)
    for t in _T5_STEPS:
        large = large + (na >= t).astype(jnp.int32)
    large = jnp.minimum(large, NUM_BUCKETS // 2 - 1)
    bkt = jnp.where(rel > 0, NUM_BUCKETS // 2, 0) + jnp.where(na < NUM_BUCKETS // 4, na, large)
    bkts = [bkt[:, j * BLOCK:(j + 1) * BLOCK] for j in range(3)]

    q = q_ref[0]
    for g in range(SWA_KV_HEADS):
        lo, hi = g * HEAD_DIM, (g + 1) * HEAD_DIM
        k3 = jnp.concatenate([kp_ref[0, :, lo:hi], kc_ref[0, :, lo:hi], kn_ref[0, :, lo:hi]], axis=0)
        v3 = jnp.concatenate([vp_ref[0, :, lo:hi], vc_ref[0, :, lo:hi], vn_ref[0, :, lo:hi]], axis=0)
        for j in range(SWA_GROUP):
            hd = g * SWA_GROUP + j
            tab = jnp.broadcast_to(t5_ref[hd:hd + 1, :], (BLOCK, LANES))
            bias = jnp.concatenate([jnp.take_along_axis(tab, b, axis=1) for b in bkts], axis=1)
            qh = q[:, hd * HEAD_DIM:(hd + 1) * HEAD_DIM]
            s = lax.dot_general(qh, k3, (((1,), (1,)), ((), ())), preferred_element_type=F32) + bias
            s = jnp.where(mask, s, NEG)
            sk = sink_ref[0, hd]
            m = jnp.maximum(jnp.max(s, axis=-1, keepdims=True), sk)
            e = jnp.exp(s - m)
            denom = jnp.sum(e, axis=-1, keepdims=True) + jnp.exp(sk - m)
            o = jnp.dot(e.astype(BF16), v3, preferred_element_type=F32) / denom
            o_ref[0, :, hd * HEAD_DIM:(hd + 1) * HEAD_DIM] = o


def _swa_attn(t5, sink, sq, sk, sv, pos_col, pos_row):
    B, S, _ = sq.shape
    nb = S // BLOCK
    kvw = SWA_KV_HEADS * HEAD_DIM
    prev = lambda b, n: (b, jnp.maximum(n - 1, 0), 0)
    cur = lambda b, n: (b, n, 0)
    nxt = lambda b, n: (b, jnp.minimum(n + 1, nb - 1), 0)
    rprev = lambda b, n: (b, 0, jnp.maximum(n - 1, 0))
    rcur = lambda b, n: (b, 0, n)
    rnxt = lambda b, n: (b, 0, jnp.minimum(n + 1, nb - 1))
    smem = pl.BlockSpec(memory_space=pltpu.SMEM)
    return pl.pallas_call(
        functools.partial(_swa_kernel, nb=nb),
        grid=(B, nb),
        in_specs=[
            _const_spec(t5.shape), smem,
            pl.BlockSpec((1, BLOCK, SWA_HEADS * HEAD_DIM), cur),
            pl.BlockSpec((1, BLOCK, kvw), prev), pl.BlockSpec((1, BLOCK, kvw), cur), pl.BlockSpec((1, BLOCK, kvw), nxt),
            pl.BlockSpec((1, BLOCK, kvw), prev), pl.BlockSpec((1, BLOCK, kvw), cur), pl.BlockSpec((1, BLOCK, kvw), nxt),
            pl.BlockSpec((1, BLOCK, 1), cur),
            pl.BlockSpec((1, 1, BLOCK), rprev), pl.BlockSpec((1, 1, BLOCK), rcur), pl.BlockSpec((1, 1, BLOCK), rnxt),
        ],
        out_specs=pl.BlockSpec((1, BLOCK, SWA_HEADS * HEAD_DIM), cur),
        out_shape=jax.ShapeDtypeStruct((B, S, SWA_HEADS * HEAD_DIM), F32),
        compiler_params=_params(("parallel", "parallel")), name="swa_attn",
    )(t5, sink, sq, sk, sk, sk, sv, sv, sv, pos_col, pos_row, pos_row, pos_row)


def _out_proj_kernel(x_ref, mla_ref, swa_ref, g1_ref, g2_ref, wo_ref, gf_ref, xo_ref, h2_ref):
    w = wo_ref.shape[0] // 2
    m1 = _rms(mla_ref[...], g1_ref[...]).astype(BF16)
    m2 = _rms(swa_ref[...], g2_ref[...]).astype(BF16)
    y = jnp.dot(m1, wo_ref[0:w, :], preferred_element_type=F32)
    y = y + jnp.dot(m2, wo_ref[w:2 * w, :], preferred_element_type=F32)
    xn = x_ref[...] + y
    xo_ref[...] = xn
    h2_ref[...] = _rms(xn, gf_ref[...]).astype(BF16)


def _out_proj(x, mla_o, swa_o, g1, g2, wo, gf, *, tm):
    T, D = x.shape
    W = mla_o.shape[1]
    row = lambda i: (i, 0)
    return pl.pallas_call(
        _out_proj_kernel,
        grid=(T // tm,),
        in_specs=[
            pl.BlockSpec((tm, D), row), pl.BlockSpec((tm, W), row), pl.BlockSpec((tm, W), row),
            _const_spec(g1.shape), _const_spec(g2.shape), _const_spec(wo.shape), _const_spec(gf.shape),
        ],
        out_specs=(pl.BlockSpec((tm, D), row), pl.BlockSpec((tm, D), row)),
        out_shape=(jax.ShapeDtypeStruct((T, D), F32), jax.ShapeDtypeStruct((T, D), BF16)),
        compiler_params=_params(("parallel",)), name="out_proj",
    )(x, mla_o, swa_o, g1, g2, wo, gf)


def _ffn_kernel(hp_ref, h_ref, hn_ref, wg_ref, wu_ref, cw_ref, cb_ref, wd_ref, y_ref, hext_ref, *,
                tm, tiles_per_seq):
    i = pl.program_id(0)
    j = pl.program_id(1)
    halo = BF16_SUBLANES

    @pl.when(j == 0)
    def _():
        t = i % tiles_per_seq
        hext_ref[0:halo, :] = jnp.where(t == 0, jnp.zeros_like(hp_ref[...]), hp_ref[...])
        hext_ref[halo:halo + tm, :] = h_ref[...]
        hext_ref[halo + tm:, :] = jnp.where(t == tiles_per_seq - 1, jnp.zeros_like(hn_ref[...]), hn_ref[...])
        y_ref[...] = jnp.zeros_like(y_ref)

    gp = jnp.dot(hext_ref[...], wg_ref[...], preferred_element_type=F32)
    cw = cw_ref[...]
    g = (gp[halo - 1:halo - 1 + tm] * cw[0:1] + gp[halo:halo + tm] * cw[1:2]
         + gp[halo + 1:halo + 1 + tm] * cw[2:3] + cb_ref[...])
    u = jnp.dot(h_ref[...], wu_ref[...], preferred_element_type=F32)
    a = (g * _sigmoid(g) * u).astype(BF16)
    y_ref[...] += jnp.dot(a, wd_ref[...], preferred_element_type=F32)


def _ffn(h2, wg, wu, cw, cb, wd, *, tm, tf, seq):
    T, D = h2.shape
    Fp = wg.shape[1]
    halo = BF16_SUBLANES
    hb = tm // halo
    nhb = T // halo
    return pl.pallas_call(
        functools.partial(_ffn_kernel, tm=tm, tiles_per_seq=seq // tm),
        grid=(T // tm, Fp // tf),
        in_specs=[
            pl.BlockSpec((halo, D), lambda i, j: (jnp.maximum(i * hb - 1, 0), 0)),
            pl.BlockSpec((tm, D), lambda i, j: (i, 0)),
            pl.BlockSpec((halo, D), lambda i, j: (jnp.minimum((i + 1) * hb, nhb - 1), 0)),
            pl.BlockSpec((D, tf), lambda i, j: (0, j)),
            pl.BlockSpec((D, tf), lambda i, j: (0, j)),
            pl.BlockSpec((3, tf), lambda i, j: (0, j)),
            pl.BlockSpec((1, tf), lambda i, j: (0, j)),
            pl.BlockSpec((tf, D), lambda i, j: (j, 0)),
        ],
        out_specs=pl.BlockSpec((tm, D), lambda i, j: (i, 0)),
        out_shape=jax.ShapeDtypeStruct((T, D), F32),
        scratch_shapes=[pltpu.VMEM((tm + 2 * halo, D), BF16)],
        compiler_params=_params(("parallel", "arbitrary")), name="ffn",
    )(h2, h2, h2, wg, wu, cw, cb, wd)


def _ple_kernel(x_ref, y_ref, p_ref, wpg_ref, bpg_ref, wpp_ref, gfin_ref, o_ref, *, final):
    x2 = x_ref[...] + y_ref[...]
    gate = _sigmoid(jnp.dot(x2.astype(BF16), wpg_ref[...], preferred_element_type=F32) + bpg_ref[...])
    pp = jnp.dot(p_ref[...].astype(BF16), wpp_ref[...], preferred_element_type=F32)
    x3 = x2 + gate * pp
    if final:
        x3 = _rms(x3, gfin_ref[...])
    o_ref[...] = x3


def _ple(x, y, p, wpg, bpg, wpp, gfin, *, tm, final):
    T, D = x.shape
    P = p.shape[1]
    row = lambda i: (i, 0)
    return pl.pallas_call(
        functools.partial(_ple_kernel, final=final),
        grid=(T // tm,),
        in_specs=[
            pl.BlockSpec((tm, D), row), pl.BlockSpec((tm, D), row), pl.BlockSpec((tm, P), row),
            _const_spec(wpg.shape), _const_spec(bpg.shape), _const_spec(wpp.shape), _const_spec(gfin.shape),
        ],
        out_specs=pl.BlockSpec((tm, D), row),
        out_shape=jax.ShapeDtypeStruct((T, D), F32),
        compiler_params=_params(("parallel",)), name="ple",
    )(x, y, p, wpg, bpg, wpp, gfin)


def _prep_w_in(w_in):
    half = MLA_ROPE // 2
    kr0 = Q_RANK + KV_RANK
    t1, t2 = w_in[:, kr0:kr0 + half], w_in[:, kr0 + half:kr0 + MLA_ROPE]
    zpad = jnp.zeros((w_in.shape[0], LANES - MLA_ROPE), w_in.dtype)
    return jnp.concatenate(
        [w_in[:, :kr0], t1, t2, zpad, t2, t1, zpad, w_in[:, kr0 + MLA_ROPE:]], axis=1).astype(BF16)


def _prep_w_uq(w_uq):
    half = MLA_ROPE // 2
    w = w_uq.reshape(Q_RANK, MLA_HEADS, MLA_NOPE + MLA_ROPE)
    nope, t1, t2 = w[..., :MLA_NOPE], w[..., MLA_NOPE:MLA_NOPE + half], w[..., MLA_NOPE + half:]
    zpad = jnp.zeros((Q_RANK, MLA_HEADS, LANES - MLA_ROPE), w_uq.dtype)
    cols = jnp.concatenate([nope, t1, t2, zpad, t2, t1, zpad], axis=-1)
    return cols.reshape(Q_RANK, MLA_HEADS * 3 * LANES).T.astype(BF16)


def _prep_w_ukv(w_ukv):
    w = w_ukv.reshape(KV_RANK, MLA_HEADS, MLA_NOPE + MLA_V)
    wkn = w[..., :MLA_NOPE].reshape(KV_RANK, MLA_HEADS * MLA_NOPE).astype(BF16)
    wvt = w[..., MLA_NOPE:].reshape(KV_RANK, MLA_HEADS * MLA_V).T.astype(BF16)
    return wkn, wvt


def _pad_ff(w, axis, fp):
    pad = [(0, 0)] * w.ndim
    pad[axis] = (0, fp - w.shape[axis])
    return jnp.pad(w, pad)


def _pick_tile(n, pref):
    t = min(pref, n)
    while n % t:
        t //= 2
    return t


def kernel(x, p, positions, attn_norm, w_in, cq_norm, ckv_norm, w_uq, w_ukv, swa_sink, t5_bias,
           mla_out_norm, swa_out_norm, w_o, ffn_norm, w_gate, w_up, conv_w, conv_b, w_down,
           ple_gate_w, ple_gate_b, ple_proj, final_norm):
    B, S, D = x.shape
    depth = w_in.shape[0]
    T = B * S
    d_ff = w_gate.shape[-1]
    tf = 512
    fp = -(-d_ff // tf) * tf

    tm_in = _pick_tile(S, 512)
    tq = _pick_tile(S, 512)
    tk = _pick_tile(S, 512)
    tm_out = _pick_tile(S, 512)
    tm_ffn = _pick_tile(S, 1024)
    tm_ple = _pick_tile(S, 512)

    pos_row = positions.reshape(B, 1, S)
    pos_col = positions.reshape(B, S, 1)
    invf = (ROPE_THETA ** (-jnp.arange(0, MLA_ROPE, 2, dtype=F32) / MLA_ROPE)).reshape(MLA_ROPE // 2, 1)
    row = lambda v: v.reshape(1, -1)
    t5_tab = jnp.pad(t5_bias.T, ((0, 0), (0, LANES - NUM_BUCKETS)))

    xf = x
    for i in range(depth):
        wkn, wvt = _prep_w_ukv(w_ukv[i])
        qt, k, vt, sq, sk, sv = _in_proj(
            xf.reshape(B, S, D), pos_row, invf, row(attn_norm[i]), _prep_w_in(w_in[i]),
            row(cq_norm[i]), row(ckv_norm[i]), _prep_w_uq(w_uq[i]), wkn, wvt, tm=tm_in)
        mla_o = _mla_attn(qt, k, vt, tq=tq, tk=tk, q_tiles_per_step=2 if S % (2 * tq) == 0 else 1)
        swa_o = _swa_attn(t5_tab, row(swa_sink[i]), sq, sk, sv, pos_col, pos_row)
        x1, h2 = _out_proj(
            xf.reshape(T, D), mla_o.reshape(T, -1), swa_o.reshape(T, -1), row(mla_out_norm[i]),
            row(swa_out_norm[i]), w_o[i].astype(BF16), row(ffn_norm[i]), tm=tm_out)
        y = _ffn(
            h2, _pad_ff(w_gate[i], 1, fp).astype(BF16), _pad_ff(w_up[i], 1, fp).astype(BF16),
            _pad_ff(conv_w[i], 1, fp), _pad_ff(row(conv_b[i]), 1, fp),
            _pad_ff(w_down[i], 0, fp).astype(BF16), tm=tm_ffn, tf=tf, seq=S)
        xf = _ple(
            x1, y, p[i].reshape(T, -1), ple_gate_w[i].astype(BF16), row(ple_gate_b[i]),
            ple_proj[i].astype(BF16), row(final_norm), tm=tm_ple, final=(i == depth - 1))
    return xf.reshape(B, S, D)
```

```python
import functools
import math

import jax
import jax.numpy as jnp
from jax import lax
from jax.experimental import pallas as pl
from jax.experimental.pallas import tpu as pltpu

F32 = jnp.float32
BF16 = jnp.bfloat16

EPS = 1e-6
BLOCK = 128
MLA_HEADS = 8
MLA_NOPE = 128
MLA_ROPE = 64
MLA_V = 128
MLA_QK_PAD = 256
Q_RANK = 384
KV_RANK = 256
ROPE_THETA = 10000.0
SWA_HEADS = 8
SWA_KV_HEADS = 2
SWA_GROUP = SWA_HEADS // SWA_KV_HEADS
HEAD_DIM = 128
WINDOW = 128
NUM_BUCKETS = 32
NEG = -1e30
LOG2E = math.log2(math.e)

V7X_VMEM_BYTES = 64 * 1024 * 1024
VMEM_LIMIT_BYTES = V7X_VMEM_BYTES - 8 * 1024 * 1024
LANES = 128
BF16_SUBLANES = 16
MLA_V_EXT = MLA_V + BF16_SUBLANES

_Z_CQ = 0
_Z_CKV = _Z_CQ + Q_RANK
_Z_KA = _Z_CKV + KV_RANK
_Z_KB = _Z_KA + LANES
_Z_SQ = _Z_KB + LANES
_Z_SK = _Z_SQ + SWA_HEADS * HEAD_DIM
_Z_SV = _Z_SK + SWA_KV_HEADS * HEAD_DIM
_Z_END = _Z_SV + SWA_KV_HEADS * HEAD_DIM


def _params(sem):
    return pltpu.CompilerParams(dimension_semantics=sem, vmem_limit_bytes=VMEM_LIMIT_BYTES)


def _const_spec(shape):
    nd = len(shape)
    return pl.BlockSpec(shape, lambda *_: (0,) * nd, pipeline_mode=pl.Buffered(1))


def _rms(x, g):
    ms = jnp.mean(x * x, axis=-1, keepdims=True)
    return x * lax.rsqrt(ms + EPS) * g


def _sigmoid(x):
    return 1.0 / (1.0 + jnp.exp(-x))


def _in_proj_kernel(x_ref, pos_ref, invf_ref, g_ref, win_ref, cqn_ref, ckvn_ref, wq_ref, wkn_ref,
                    wvt_ref, qt_ref, k_ref, vt_ref, sq_ref, sk_ref, sv_ref, *, q_scale, swa_scale):
    x = x_ref[0]
    h = _rms(x, g_ref[...]).astype(BF16)
    z = jnp.dot(h, win_ref[...], preferred_element_type=F32)
    cq = _rms(z[:, _Z_CQ:_Z_CKV], cqn_ref[...]).astype(BF16)
    ckv = _rms(z[:, _Z_CKV:_Z_KA], ckvn_ref[...]).astype(BF16)

    ang = invf_ref[...] * pos_ref[0].astype(F32)
    cos_t, sin_t = jnp.cos(ang), jnp.sin(ang)
    zpad = jnp.zeros((LANES - MLA_ROPE, ang.shape[1]), F32)
    cc_t = jnp.concatenate([cos_t, cos_t, zpad], axis=0)
    ss_t = jnp.concatenate([-sin_t, sin_t, zpad], axis=0)

    r = lax.dot_general(wq_ref[...], cq, (((1,), (1,)), ((), ())), preferred_element_type=F32)
    for hd in range(MLA_HEADS):
        b = hd * 3 * LANES
        qt_ref[0, hd, 0:LANES, :] = (r[b:b + LANES] * q_scale).astype(BF16)
        roped = r[b + LANES:b + 2 * LANES] * cc_t + r[b + 2 * LANES:b + 3 * LANES] * ss_t
        qt_ref[0, hd, LANES:2 * LANES, :] = (roped * q_scale).astype(BF16)

    knope = jnp.dot(ckv, wkn_ref[...], preferred_element_type=F32)
    cc, ss = cc_t.T, ss_t.T
    krope = (z[:, _Z_KA:_Z_KB] * cc + z[:, _Z_KB:_Z_SQ] * ss).astype(BF16)
    for hd in range(MLA_HEADS):
        k_ref[0, hd, :, 0:LANES] = knope[:, hd * LANES:(hd + 1) * LANES].astype(BF16)
        k_ref[0, hd, :, LANES:MLA_QK_PAD] = krope

    vt = lax.dot_general(wvt_ref[...], ckv, (((1,), (1,)), ((), ())), preferred_element_type=F32)
    ones_row = (lax.broadcasted_iota(jnp.int32, (BF16_SUBLANES, vt.shape[1]), 0) == 0).astype(BF16)
    for hd in range(MLA_HEADS):
        vt_ref[0, hd, 0:MLA_V, :] = vt[hd * MLA_V:(hd + 1) * MLA_V].astype(BF16)
        vt_ref[0, hd, MLA_V:MLA_V_EXT, :] = ones_row

    sq_ref[0] = (z[:, _Z_SQ:_Z_SK] * swa_scale).astype(BF16)
    sk_ref[0] = z[:, _Z_SK:_Z_SV].astype(BF16)
    sv_ref[0] = z[:, _Z_SV:_Z_END].astype(BF16)


def _in_proj(x, pos_row, invf, g, win, cqn, ckvn, wq, wkn, wvt, *, tm):
    B, S, D = x.shape
    grid = (B, S // tm)
    q_scale = LOG2E / math.sqrt(MLA_NOPE + MLA_ROPE)
    swa_scale = 1.0 / math.sqrt(HEAD_DIM)
    out_shape = (
        jax.ShapeDtypeStruct((B, MLA_HEADS, MLA_QK_PAD, S), BF16),
        jax.ShapeDtypeStruct((B, MLA_HEADS, S, MLA_QK_PAD), BF16),
        jax.ShapeDtypeStruct((B, MLA_HEADS, MLA_V_EXT, S), BF16),
        jax.ShapeDtypeStruct((B, S, SWA_HEADS * HEAD_DIM), BF16),
        jax.ShapeDtypeStruct((B, S, SWA_KV_HEADS * HEAD_DIM), BF16),
        jax.ShapeDtypeStruct((B, S, SWA_KV_HEADS * HEAD_DIM), BF16),
    )
    in_specs = [
        pl.BlockSpec((1, tm, D), lambda b, i: (b, i, 0)),
        pl.BlockSpec((1, 1, tm), lambda b, i: (b, 0, i)),
        _const_spec(invf.shape), _const_spec(g.shape), _const_spec(win.shape),
        _const_spec(cqn.shape), _const_spec(ckvn.shape), _const_spec(wq.shape),
        _const_spec(wkn.shape), _const_spec(wvt.shape),
    ]
    out_specs = (
        pl.BlockSpec((1, MLA_HEADS, MLA_QK_PAD, tm), lambda b, i: (b, 0, 0, i)),
        pl.BlockSpec((1, MLA_HEADS, tm, MLA_QK_PAD), lambda b, i: (b, 0, i, 0)),
        pl.BlockSpec((1, MLA_HEADS, MLA_V_EXT, tm), lambda b, i: (b, 0, 0, i)),
        pl.BlockSpec((1, tm, SWA_HEADS * HEAD_DIM), lambda b, i: (b, i, 0)),
        pl.BlockSpec((1, tm, SWA_KV_HEADS * HEAD_DIM), lambda b, i: (b, i, 0)),
        pl.BlockSpec((1, tm, SWA_KV_HEADS * HEAD_DIM), lambda b, i: (b, i, 0)),
    )
    return pl.pallas_call(
        functools.partial(_in_proj_kernel, q_scale=q_scale, swa_scale=swa_scale),
        grid=grid, in_specs=in_specs, out_specs=out_specs, out_shape=out_shape,
        compiler_params=_params(("parallel", "parallel")), name="in_proj",
    )(x, pos_row, invf, g, win, cqn, ckvn, wq, wkn, wvt)


def _mla_kernel(qt_ref, k_ref, vt_ref, o_ref, m_ref, acc_ref, s_ref, mc_ref, *, tq, tk, unroll, streams):
    S = k_ref.shape[2]
    n = S // tk

    def q_group(t, carry):
        q0s = [pl.multiple_of((t * streams + j) * tq, tq) for j in range(streams)]
        for j in range(streams):
            m_ref[j] = jnp.full(m_ref.shape[1:], NEG, F32)
            acc_ref[j] = jnp.zeros(acc_ref.shape[1:], F32)

        def scores(c, j, slot):
            start = pl.multiple_of(c * tk, tk)
            s = jnp.dot(k_ref[0, 0, pl.ds(start, tk), :], qt_ref[0, 0, :, pl.ds(q0s[j], tq)],
                        preferred_element_type=F32)
            s_ref[2 * j + slot] = s
            mc_ref[2 * j + slot] = jnp.max(s, axis=0, keepdims=True)

        def softmax_pv(c, j, slot):
            start = pl.multiple_of(c * tk, tk)
            m_prev = m_ref[j]
            m_new = jnp.maximum(m_prev, mc_ref[2 * j + slot])
            alpha = jnp.exp2(m_prev - m_new)
            p = jnp.exp2(s_ref[2 * j + slot] - m_new).astype(BF16)
            vc = vt_ref[0, 0, :, pl.ds(start, tk)]
            acc_ref[j] = alpha * acc_ref[j] + jnp.dot(vc, p, preferred_element_type=F32)
            m_ref[j] = m_new

        for j in range(streams):
            scores(0, j, 0)

        def group(i, carry):
            c0 = i * unroll
            for u in range(unroll):
                nxt = c0 + u + 1
                if u == unroll - 1:
                    nxt = jnp.where(nxt == n, 0, nxt)
                for j in range(streams):
                    scores(nxt, j, (u + 1) % 2)
                    softmax_pv(c0 + u, j, u % 2)
            return carry

        lax.fori_loop(0, n // unroll, group, 0)
        for j in range(streams):
            acc = acc_ref[j]
            o_ref[0, pl.ds(q0s[j], tq), :] = (acc[0:MLA_V] / acc[MLA_V:MLA_V + 1]).T
        return carry

    lax.fori_loop(0, qt_ref.shape[3] // (tq * streams), q_group, 0)


def _mla_attn(qt, k, vt, *, tq, tk, q_tiles_per_step):
    B, H, _, S = qt.shape
    vrows = vt.shape[2]
    n = S // tk
    streams = 2 if q_tiles_per_step % 2 == 0 else 1
    unroll = max(u for u in (2, 4) if n % u == 0)
    tqs = tq * q_tiles_per_step
    grid = (B, H, S // tqs)
    return pl.pallas_call(
        functools.partial(_mla_kernel, tq=tq, tk=tk, unroll=unroll, streams=streams),
        grid=grid,
        in_specs=[
            pl.BlockSpec((1, 1, MLA_QK_PAD, tqs), lambda b, h, i: (b, h, 0, i)),
            pl.BlockSpec((1, 1, S, MLA_QK_PAD), lambda b, h, i: (b, h, 0, 0)),
            pl.BlockSpec((1, 1, vrows, S), lambda b, h, i: (b, h, 0, 0)),
        ],
        out_specs=pl.BlockSpec((1, tqs, MLA_V), lambda b, h, i: (b, i, h)),
        out_shape=jax.ShapeDtypeStruct((B, S, H * MLA_V), F32),
        scratch_shapes=[pltpu.VMEM((streams, 1, tq), F32), pltpu.VMEM((streams, vrows, tq), F32),
                        pltpu.VMEM((2 * streams, tk, tq), F32), pltpu.VMEM((2 * streams, 1, tq), F32)],
        compiler_params=_params(("parallel", "parallel", "arbitrary")), name="mla_attn",
    )(qt, k, vt)


_T5_STEPS = (12, 16, 23, 32, 46, 64, 91, 128)


def _swa_kernel(t5_ref, sink_ref, q_ref, kp_ref, kc_ref, kn_ref, vp_ref, vc_ref, vn_ref,
                pq_ref, pp_ref, pc_ref, pn_ref, o_ref, *, nb):
    n = pl.program_id(1)
    pq = pq_ref[0]
    pk = jnp.concatenate([pp_ref[0], pc_ref[0], pn_ref[0]], axis=1)
    rel = pk - pq
    col = lax.broadcasted_iota(jnp.int32, rel.shape, 1)
    lo = jnp.where(n > 0, 0, BLOCK)
    hi = jnp.where(n < nb - 1, 3 * BLOCK, 2 * BLOCK)
    na = jnp.abs(rel)
    mask = (na <= WINDOW) & (col >= lo) & (col < hi)
    large = jnp.full(na.shape, NUM_BUCKETS // 4, jnp.int32)
    for t in _T5_STEPS:
        large = large + (na >= t).astype(jnp.int32)
    large = jnp.minimum(large, NUM_BUCKETS // 2 - 1)
    bkt = jnp.where(rel > 0, NUM_BUCKETS // 2, 0) + jnp.where(na < NUM_BUCKETS // 4, na, large)
    bkts = [bkt[:, j * BLOCK:(j + 1) * BLOCK] for j in range(3)]

    q = q_ref[0]
    for g in range(SWA_KV_HEADS):
        lo, hi = g * HEAD_DIM, (g + 1) * HEAD_DIM
        k3 = jnp.concatenate([kp_ref[0, :, lo:hi], kc_ref[0, :, lo:hi], kn_ref[0, :, lo:hi]], axis=0)
        v3 = jnp.concatenate([vp_ref[0, :, lo:hi], vc_ref[0, :, lo:hi], vn_ref[0, :, lo:hi]], axis=0)
        for j in range(SWA_GROUP):
            hd = g * SWA_GROUP + j
            tab = jnp.broadcast_to(t5_ref[hd:hd + 1, :], (BLOCK, LANES))
            bias = jnp.concatenate([jnp.take_along_axis(tab, b, axis=1) for b in bkts], axis=1)
            qh = q[:, hd * HEAD_DIM:(hd + 1) * HEAD_DIM]
            s = lax.dot_general(qh, k3, (((1,), (1,)), ((), ())), preferred_element_type=F32) + bias
            s = jnp.where(mask, s, NEG)
            sk = sink_ref[0, hd]
            m = jnp.maximum(jnp.max(s, axis=-1, keepdims=True), sk)
            e = jnp.exp(s - m)
            denom = jnp.sum(e, axis=-1, keepdims=True) + jnp.exp(sk - m)
            o = jnp.dot(e.astype(BF16), v3, preferred_element_type=F32) / denom
            o_ref[0, :, hd * HEAD_DIM:(hd + 1) * HEAD_DIM] = o


def _swa_attn(t5, sink, sq, sk, sv, pos_col, pos_row):
    B, S, _ = sq.shape
    nb = S // BLOCK
    kvw = SWA_KV_HEADS * HEAD_DIM
    prev = lambda b, n: (b, jnp.maximum(n - 1, 0), 0)
    cur = lambda b, n: (b, n, 0)
    nxt = lambda b, n: (b, jnp.minimum(n + 1, nb - 1), 0)
    rprev = lambda b, n: (b, 0, jnp.maximum(n - 1, 0))
    rcur = lambda b, n: (b, 0, n)
    rnxt = lambda b, n: (b, 0, jnp.minimum(n + 1, nb - 1))
    smem = pl.BlockSpec(memory_space=pltpu.SMEM)
    return pl.pallas_call(
        functools.partial(_swa_kernel, nb=nb),
        grid=(B, nb),
        in_specs=[
            _const_spec(t5.shape), smem,
            pl.BlockSpec((1, BLOCK, SWA_HEADS * HEAD_DIM), cur),
            pl.BlockSpec((1, BLOCK, kvw), prev), pl.BlockSpec((1, BLOCK, kvw), cur), pl.BlockSpec((1, BLOCK, kvw), nxt),
            pl.BlockSpec((1, BLOCK, kvw), prev), pl.BlockSpec((1, BLOCK, kvw), cur), pl.BlockSpec((1, BLOCK, kvw), nxt),
            pl.BlockSpec((1, BLOCK, 1), cur),
            pl.BlockSpec((1, 1, BLOCK), rprev), pl.BlockSpec((1, 1, BLOCK), rcur), pl.BlockSpec((1, 1, BLOCK), rnxt),
        ],
        out_specs=pl.BlockSpec((1, BLOCK, SWA_HEADS * HEAD_DIM), cur),
        out_shape=jax.ShapeDtypeStruct((B, S, SWA_HEADS * HEAD_DIM), F32),
        compiler_params=_params(("parallel", "parallel")), name="swa_attn",
    )(t5, sink, sq, sk, sk, sk, sv, sv, sv, pos_col, pos_row, pos_row, pos_row)


def _out_proj_kernel(x_ref, mla_ref, swa_ref, g1_ref, g2_ref, wo_ref, gf_ref, xo_ref, h2_ref):
    w = wo_ref.shape[0] // 2
    m1 = _rms(mla_ref[...], g1_ref[...]).astype(BF16)
    m2 = _rms(swa_ref[...], g2_ref[...]).astype(BF16)
    y = jnp.dot(m1, wo_ref[0:w, :], preferred_element_type=F32)
    y = y + jnp.dot(m2, wo_ref[w:2 * w, :], preferred_element_type=F32)
    xn = x_ref[...] + y
    xo_ref[...] = xn
    h2_ref[...] = _rms(xn, gf_ref[...]).astype(BF16)


def _out_proj(x, mla_o, swa_o, g1, g2, wo, gf, *, tm):
    T, D = x.shape
    W = mla_o.shape[1]
    row = lambda i: (i, 0)
    return pl.pallas_call(
        _out_proj_kernel,
        grid=(T // tm,),
        in_specs=[
            pl.BlockSpec((tm, D), row), pl.BlockSpec((tm, W), row), pl.BlockSpec((tm, W), row),
            _const_spec(g1.shape), _const_spec(g2.shape), _const_spec(wo.shape), _const_spec(gf.shape),
        ],
        out_specs=(pl.BlockSpec((tm, D), row), pl.BlockSpec((tm, D), row)),
        out_shape=(jax.ShapeDtypeStruct((T, D), F32), jax.ShapeDtypeStruct((T, D), BF16)),
        compiler_params=_params(("parallel",)), name="out_proj",
    )(x, mla_o, swa_o, g1, g2, wo, gf)


def _ffn_kernel(hp_ref, h_ref, hn_ref, wg_ref, wu_ref, cw_ref, cb_ref, wd_ref, y_ref, hext_ref, *,
                tm, tiles_per_seq):
    i = pl.program_id(0)
    j = pl.program_id(1)
    halo = BF16_SUBLANES

    @pl.when(j == 0)
    def _():
        t = i % tiles_per_seq
        hext_ref[0:halo, :] = jnp.where(t == 0, jnp.zeros_like(hp_ref[...]), hp_ref[...])
        hext_ref[halo:halo + tm, :] = h_ref[...]
        hext_ref[halo + tm:, :] = jnp.where(t == tiles_per_seq - 1, jnp.zeros_like(hn_ref[...]), hn_ref[...])
        y_ref[...] = jnp.zeros_like(y_ref)

    gp = jnp.dot(hext_ref[...], wg_ref[...], preferred_element_type=F32)
    cw = cw_ref[...]
    g = (gp[halo - 1:halo - 1 + tm] * cw[0:1] + gp[halo:halo + tm] * cw[1:2]
         + gp[halo + 1:halo + 1 + tm] * cw[2:3] + cb_ref[...])
    u = jnp.dot(h_ref[...], wu_ref[...], preferred_element_type=F32)
    a = (g * _sigmoid(g) * u).astype(BF16)
    y_ref[...] += jnp.dot(a, wd_ref[...], preferred_element_type=F32)


def _ffn(h2, wg, wu, cw, cb, wd, *, tm, tf, seq):
    T, D = h2.shape
    Fp = wg.shape[1]
    halo = BF16_SUBLANES
    hb = tm // halo
    nhb = T // halo
    return pl.pallas_call(
        functools.partial(_ffn_kernel, tm=tm, tiles_per_seq=seq // tm),
        grid=(T // tm, Fp // tf),
        in_specs=[
            pl.BlockSpec((halo, D), lambda i, j: (jnp.maximum(i * hb - 1, 0), 0)),
            pl.BlockSpec((tm, D), lambda i, j: (i, 0)),
            pl.BlockSpec((halo, D), lambda i, j: (jnp.minimum((i + 1) * hb, nhb - 1), 0)),
            pl.BlockSpec((D, tf), lambda i, j: (0, j)),
            pl.BlockSpec((D, tf), lambda i, j: (0, j)),
            pl.BlockSpec((3, tf), lambda i, j: (0, j)),
            pl.BlockSpec((1, tf), lambda i, j: (0, j)),
            pl.BlockSpec((tf, D), lambda i, j: (j, 0)),
        ],
        out_specs=pl.BlockSpec((tm, D), lambda i, j: (i, 0)),
        out_shape=jax.ShapeDtypeStruct((T, D), F32),
        scratch_shapes=[pltpu.VMEM((tm + 2 * halo, D), BF16)],
        compiler_params=_params(("parallel", "arbitrary")), name="ffn",
    )(h2, h2, h2, wg, wu, cw, cb, wd)


def _ple_kernel(x_ref, y_ref, p_ref, wpg_ref, bpg_ref, wpp_ref, gfin_ref, o_ref, *, final):
    x2 = x_ref[...] + y_ref[...]
    gate = _sigmoid(jnp.dot(x2.astype(BF16), wpg_ref[...], preferred_element_type=F32) + bpg_ref[...])
    pp = jnp.dot(p_ref[...].astype(BF16), wpp_ref[...], preferred_element_type=F32)
    x3 = x2 + gate * pp
    if final:
        x3 = _rms(x3, gfin_ref[...])
    o_ref[...] = x3


def _ple(x, y, p, wpg, bpg, wpp, gfin, *, tm, final):
    T, D = x.shape
    P = p.shape[1]
    row = lambda i: (i, 0)
    return pl.pallas_call(
        functools.partial(_ple_kernel, final=final),
        grid=(T // tm,),
        in_specs=[
            pl.BlockSpec((tm, D), row), pl.BlockSpec((tm, D), row), pl.BlockSpec((tm, P), row),
            _const_spec(wpg.shape), _const_spec(bpg.shape), _const_spec(wpp.shape), _const_spec(gfin.shape),
        ],
        out_specs=pl.BlockSpec((tm, D), row),
        out_shape=jax.ShapeDtypeStruct((T, D), F32),
        compiler_params=_params(("parallel",)), name="ple",
    )(x, y, p, wpg, bpg, wpp, gfin)


def _prep_w_in(w_in):
    half = MLA_ROPE // 2
    kr0 = Q_RANK + KV_RANK
    t1, t2 = w_in[:, kr0:kr0 + half], w_in[:, kr0 + half:kr0 + MLA_ROPE]
    zpad = jnp.zeros((w_in.shape[0], LANES - MLA_ROPE), w_in.dtype)
    return jnp.concatenate(
        [w_in[:, :kr0], t1, t2, zpad, t2, t1, zpad, w_in[:, kr0 + MLA_ROPE:]], axis=1).astype(BF16)


def _prep_w_uq(w_uq):
    half = MLA_ROPE // 2
    w = w_uq.reshape(Q_RANK, MLA_HEADS, MLA_NOPE + MLA_ROPE)
    nope, t1, t2 = w[..., :MLA_NOPE], w[..., MLA_NOPE:MLA_NOPE + half], w[..., MLA_NOPE + half:]
    zpad = jnp.zeros((Q_RANK, MLA_HEADS, LANES - MLA_ROPE), w_uq.dtype)
    cols = jnp.concatenate([nope, t1, t2, zpad, t2, t1, zpad], axis=-1)
    return cols.reshape(Q_RANK, MLA_HEADS * 3 * LANES).T.astype(BF16)


def _prep_w_ukv(w_ukv):
    w = w_ukv.reshape(KV_RANK, MLA_HEADS, MLA_NOPE + MLA_V)
    wkn = w[..., :MLA_NOPE].reshape(KV_RANK, MLA_HEADS * MLA_NOPE).astype(BF16)
    wvt = w[..., MLA_NOPE:].reshape(KV_RANK, MLA_HEADS * MLA_V).T.astype(BF16)
    return wkn, wvt


def _pad_ff(w, axis, fp):
    pad = [(0, 0)] * w.ndim
    pad[axis] = (0, fp - w.shape[axis])
    return jnp.pad(w, pad)


def _pick_tile(n, pref):
    t = min(pref, n)
    while n % t:
        t //= 2
    return t


def kernel(x, p, positions, attn_norm, w_in, cq_norm, ckv_norm, w_uq, w_ukv, swa_sink, t5_bias,
           mla_out_norm, swa_out_norm, w_o, ffn_norm, w_gate, w_up, conv_w, conv_b, w_down,
           ple_gate_w, ple_gate_b, ple_proj, final_norm):
    B, S, D = x.shape
    depth = w_in.shape[0]
    T = B * S
    d_ff = w_gate.shape[-1]
    tf = 512
    fp = -(-d_ff // tf) * tf

    tm_in = _pick_tile(S, 512)
    tq = _pick_tile(S, 512)
    tk = _pick_tile(S, 512)
    tm_out = _pick_tile(S, 512)
    tm_ffn = _pick_tile(S, 1024)
    tm_ple = _pick_tile(S, 512)

    pos_row = positions.reshape(B, 1, S)
    pos_col = positions.reshape(B, S, 1)
    invf = (ROPE_THETA ** (-jnp.arange(0, MLA_ROPE, 2, dtype=F32) / MLA_ROPE)).reshape(MLA_ROPE // 2, 1)
    row = lambda v: v.reshape(1, -1)
    t5_tab = jnp.pad(t5_bias.T, ((0, 0), (0, LANES - NUM_BUCKETS)))

    xf = x
    for i in range(depth):
        wkn, wvt = _prep_w_ukv(w_ukv[i])
        qt, k, vt, sq, sk, sv = _in_proj(
            xf.reshape(B, S, D), pos_row, invf, row(attn_norm[i]), _prep_w_in(w_in[i]),
            row(cq_norm[i]), row(ckv_norm[i]), _prep_w_uq(w_uq[i]), wkn, wvt, tm=tm_in)
        mla_o = _mla_attn(qt, k, vt, tq=tq, tk=tk, q_tiles_per_step=2 if S % (2 * tq) == 0 else 1)
        swa_o = _swa_attn(t5_tab, row(swa_sink[i]), sq, sk, sv, pos_col, pos_row)
        x1, h2 = _out_proj(
            xf.reshape(T, D), mla_o.reshape(T, -1), swa_o.reshape(T, -1), row(mla_out_norm[i]),
            row(swa_out_norm[i]), w_o[i].astype(BF16), row(ffn_norm[i]), tm=tm_out)
        y = _ffn(
            h2, _pad_ff(w_gate[i], 1, fp).astype(BF16), _pad_ff(w_up[i], 1, fp).astype(BF16),
            _pad_ff(conv_w[i], 1, fp), _pad_ff(row(conv_b[i]), 1, fp),
            _pad_ff(w_down[i], 0, fp).astype(BF16), tm=tm_ffn, tf=tf, seq=S)
        xf = _ple(
            x1, y, p[i].reshape(T, -1), ple_gate_w[i].astype(BF16), row(ple_gate_b[i]),
            ple_proj[i].astype(BF16), row(final_norm), tm=tm_ple, final=(i == depth - 1))
    return xf.reshape(B, S, D)
```

```python
import functools
import math

import jax
import jax.numpy as jnp
from jax import lax
from jax.experimental import pallas as pl
from jax.experimental.pallas import tpu as pltpu

F32 = jnp.float32
BF16 = jnp.bfloat16

EPS = 1e-6
BLOCK = 128
MLA_HEADS = 8
MLA_NOPE = 128
MLA_ROPE = 64
MLA_V = 128
MLA_QK_PAD = 256
Q_RANK = 384
KV_RANK = 256
ROPE_THETA = 10000.0
SWA_HEADS = 8
SWA_KV_HEADS = 2
SWA_GROUP = SWA_HEADS // SWA_KV_HEADS
HEAD_DIM = 128
WINDOW = 128
NUM_BUCKETS = 32
NEG = -1e30
LOG2E = math.log2(math.e)

V7X_VMEM_BYTES = 64 * 1024 * 1024
VMEM_LIMIT_BYTES = V7X_VMEM_BYTES - 8 * 1024 * 1024
LANES = 128
BF16_SUBLANES = 16
MLA_V_EXT = MLA_V + BF16_SUBLANES

_Z_CQ = 0
_Z_CKV = _Z_CQ + Q_RANK
_Z_KA = _Z_CKV + KV_RANK
_Z_KB = _Z_KA + LANES
_Z_SQ = _Z_KB + LANES
_Z_SK = _Z_SQ + SWA_HEADS * HEAD_DIM
_Z_SV = _Z_SK + SWA_KV_HEADS * HEAD_DIM
_Z_END = _Z_SV + SWA_KV_HEADS * HEAD_DIM


def _params(sem):
    return pltpu.CompilerParams(dimension_semantics=sem, vmem_limit_bytes=VMEM_LIMIT_BYTES)


def _const_spec(shape):
    nd = len(shape)
    return pl.BlockSpec(shape, lambda *_: (0,) * nd, pipeline_mode=pl.Buffered(1))


def _rms(x, g):
    ms = jnp.mean(x * x, axis=-1, keepdims=True)
    return x * lax.rsqrt(ms + EPS) * g


def _sigmoid(x):
    return 1.0 / (1.0 + jnp.exp(-x))


def _in_proj_kernel(x_ref, pos_ref, invf_ref, g_ref, win_ref, cqn_ref, ckvn_ref, wq_ref, wkn_ref,
                    wvt_ref, qt_ref, k_ref, vt_ref, sq_ref, sk_ref, sv_ref, *, q_scale, swa_scale):
    x = x_ref[0]
    h = _rms(x, g_ref[...]).astype(BF16)
    z = jnp.dot(h, win_ref[...], preferred_element_type=F32)
    cq = _rms(z[:, _Z_CQ:_Z_CKV], cqn_ref[...]).astype(BF16)
    ckv = _rms(z[:, _Z_CKV:_Z_KA], ckvn_ref[...]).astype(BF16)

    ang = invf_ref[...] * pos_ref[0].astype(F32)
    cos_t, sin_t = jnp.cos(ang), jnp.sin(ang)
    zpad = jnp.zeros((LANES - MLA_ROPE, ang.shape[1]), F32)
    cc_t = jnp.concatenate([cos_t, cos_t, zpad], axis=0)
    ss_t = jnp.concatenate([-sin_t, sin_t, zpad], axis=0)

    r = lax.dot_general(wq_ref[...], cq, (((1,), (1,)), ((), ())), preferred_element_type=F32)
    for hd in range(MLA_HEADS):
        b = hd * 3 * LANES
        qt_ref[0, hd, 0:LANES, :] = (r[b:b + LANES] * q_scale).astype(BF16)
        roped = r[b + LANES:b + 2 * LANES] * cc_t + r[b + 2 * LANES:b + 3 * LANES] * ss_t
        qt_ref[0, hd, LANES:2 * LANES, :] = (roped * q_scale).astype(BF16)

    knope = jnp.dot(ckv, wkn_ref[...], preferred_element_type=F32)
    cc, ss = cc_t.T, ss_t.T
    krope = (z[:, _Z_KA:_Z_KB] * cc + z[:, _Z_KB:_Z_SQ] * ss).astype(BF16)
    for hd in range(MLA_HEADS):
        k_ref[0, hd, :, 0:LANES] = knope[:, hd * LANES:(hd + 1) * LANES].astype(BF16)
        k_ref[0, hd, :, LANES:MLA_QK_PAD] = krope

    vt = lax.dot_general(wvt_ref[...], ckv, (((1,), (1,)), ((), ())), preferred_element_type=F32)
    ones_row = (lax.broadcasted_iota(jnp.int32, (BF16_SUBLANES, vt.shape[1]), 0) == 0).astype(BF16)
    for hd in range(MLA_HEADS):
        vt_ref[0, hd, 0:MLA_V, :] = vt[hd * MLA_V:(hd + 1) * MLA_V].astype(BF16)
        vt_ref[0, hd, MLA_V:MLA_V_EXT, :] = ones_row

    sq_ref[0] = (z[:, _Z_SQ:_Z_SK] * swa_scale).astype(BF16)
    sk_ref[0] = z[:, _Z_SK:_Z_SV].astype(BF16)
    sv_ref[0] = z[:, _Z_SV:_Z_END].astype(BF16)


def _in_proj(x, pos_row, invf, g, win, cqn, ckvn, wq, wkn, wvt, *, tm):
    B, S, D = x.shape
    grid = (B, S // tm)
    q_scale = LOG2E / math.sqrt(MLA_NOPE + MLA_ROPE)
    swa_scale = 1.0 / math.sqrt(HEAD_DIM)
    out_shape = (
        jax.ShapeDtypeStruct((B, MLA_HEADS, MLA_QK_PAD, S), BF16),
        jax.ShapeDtypeStruct((B, MLA_HEADS, S, MLA_QK_PAD), BF16),
        jax.ShapeDtypeStruct((B, MLA_HEADS, MLA_V_EXT, S), BF16),
        jax.ShapeDtypeStruct((B, S, SWA_HEADS * HEAD_DIM), BF16),
        jax.ShapeDtypeStruct((B, S, SWA_KV_HEADS * HEAD_DIM), BF16),
        jax.ShapeDtypeStruct((B, S, SWA_KV_HEADS * HEAD_DIM), BF16),
    )
    in_specs = [
        pl.BlockSpec((1, tm, D), lambda b, i: (b, i, 0)),
        pl.BlockSpec((1, 1, tm), lambda b, i: (b, 0, i)),
        _const_spec(invf.shape), _const_spec(g.shape), _const_spec(win.shape),
        _const_spec(cqn.shape), _const_spec(ckvn.shape), _const_spec(wq.shape),
        _const_spec(wkn.shape), _const_spec(wvt.shape),
    ]
    out_specs = (
        pl.BlockSpec((1, MLA_HEADS, MLA_QK_PAD, tm), lambda b, i: (b, 0, 0, i)),
        pl.BlockSpec((1, MLA_HEADS, tm, MLA_QK_PAD), lambda b, i: (b, 0, i, 0)),
        pl.BlockSpec((1, MLA_HEADS, MLA_V_EXT, tm), lambda b, i: (b, 0, 0, i)),
        pl.BlockSpec((1, tm, SWA_HEADS * HEAD_DIM), lambda b, i: (b, i, 0)),
        pl.BlockSpec((1, tm, SWA_KV_HEADS * HEAD_DIM), lambda b, i: (b, i, 0)),
        pl.BlockSpec((1, tm, SWA_KV_HEADS * HEAD_DIM), lambda b, i: (b, i, 0)),
    )
    return pl.pallas_call(
        functools.partial(_in_proj_kernel, q_scale=q_scale, swa_scale=swa_scale),
        grid=grid, in_specs=in_specs, out_specs=out_specs, out_shape=out_shape,
        compiler_params=_params(("parallel", "parallel")), name="in_proj",
    )(x, pos_row, invf, g, win, cqn, ckvn, wq, wkn, wvt)


def _mla_kernel(qt_ref, k_ref, vt_ref, o_ref, m_ref, acc_ref, s_ref, mc_ref, *, tq, tk, unroll, streams):
    S = k_ref.shape[2]
    n = S // tk

    def q_group(t, carry):
        q0s = [pl.multiple_of((t * streams + j) * tq, tq) for j in range(streams)]
        for j in range(streams):
            m_ref[j] = jnp.full(m_ref.shape[1:], NEG, F32)
            acc_ref[j] = jnp.zeros(acc_ref.shape[1:], F32)

        def scores(c, j, slot):
            start = pl.multiple_of(c * tk, tk)
            s = jnp.dot(k_ref[0, 0, pl.ds(start, tk), :], qt_ref[0, 0, :, pl.ds(q0s[j], tq)],
                        preferred_element_type=F32)
            s_ref[2 * j + slot] = s
            mc_ref[2 * j + slot] = jnp.max(s, axis=0, keepdims=True)

        def softmax_pv(c, j, slot):
            start = pl.multiple_of(c * tk, tk)
            m_prev = m_ref[j]
            m_new = jnp.maximum(m_prev, mc_ref[2 * j + slot])
            alpha = jnp.exp2(m_prev - m_new)
            p = jnp.exp2(s_ref[2 * j + slot] - m_new).astype(BF16)
            vc = vt_ref[0, 0, :, pl.ds(start, tk)]
            acc_ref[j] = alpha * acc_ref[j] + jnp.dot(vc, p, preferred_element_type=F32)
            m_ref[j] = m_new

        for j in range(streams):
            scores(0, j, 0)

        def group(i, carry):
            c0 = i * unroll
            for u in range(unroll):
                nxt = c0 + u + 1
                if u == unroll - 1:
                    nxt = jnp.where(nxt == n, 0, nxt)
                for j in range(streams):
                    scores(nxt, j, (u + 1) % 2)
                    softmax_pv(c0 + u, j, u % 2)
            return carry

        lax.fori_loop(0, n // unroll, group, 0)
        for j in range(streams):
            acc = acc_ref[j]
            o_ref[0, pl.ds(q0s[j], tq), :] = (acc[0:MLA_V] / acc[MLA_V:MLA_V + 1]).T
        return carry

    lax.fori_loop(0, qt_ref.shape[3] // (tq * streams), q_group, 0)


def _mla_attn(qt, k, vt, *, tq, tk, q_tiles_per_step):
    B, H, _, S = qt.shape
    vrows = vt.shape[2]
    n = S // tk
    streams = 2 if q_tiles_per_step % 2 == 0 else 1
    unroll = max(u for u in (2, 4, 8) if n % u == 0)
    tqs = tq * q_tiles_per_step
    grid = (B, H, S // tqs)
    return pl.pallas_call(
        functools.partial(_mla_kernel, tq=tq, tk=tk, unroll=unroll, streams=streams),
        grid=grid,
        in_specs=[
            pl.BlockSpec((1, 1, MLA_QK_PAD, tqs), lambda b, h, i: (b, h, 0, i)),
            pl.BlockSpec((1, 1, S, MLA_QK_PAD), lambda b, h, i: (b, h, 0, 0)),
            pl.BlockSpec((1, 1, vrows, S), lambda b, h, i: (b, h, 0, 0)),
        ],
        out_specs=pl.BlockSpec((1, tqs, MLA_V), lambda b, h, i: (b, i, h)),
        out_shape=jax.ShapeDtypeStruct((B, S, H * MLA_V), F32),
        scratch_shapes=[pltpu.VMEM((streams, 1, tq), F32), pltpu.VMEM((streams, vrows, tq), F32),
                        pltpu.VMEM((2 * streams, tk, tq), F32), pltpu.VMEM((2 * streams, 1, tq), F32)],
        compiler_params=_params(("parallel", "parallel", "arbitrary")), name="mla_attn",
    )(qt, k, vt)


_T5_STEPS = (12, 16, 23, 32, 46, 64, 91, 128)


def _swa_kernel(t5_ref, sink_ref, q_ref, kp_ref, kc_ref, kn_ref, vp_ref, vc_ref, vn_ref,
                pq_ref, pp_ref, pc_ref, pn_ref, o_ref, *, nb):
    n = pl.program_id(1)
    pq = pq_ref[0]
    pk = jnp.concatenate([pp_ref[0], pc_ref[0], pn_ref[0]], axis=0)
    rel = pk - pq
    row = lax.broadcasted_iota(jnp.int32, rel.shape, 0)
    lo = jnp.where(n > 0, 0, BLOCK)
    hi = jnp.where(n < nb - 1, 3 * BLOCK, 2 * BLOCK)
    na = jnp.abs(rel)
    mask = (na <= WINDOW) & (row >= lo) & (row < hi)
    large = jnp.full(na.shape, NUM_BUCKETS // 4, jnp.int32)
    for t in _T5_STEPS:
        large = large + (na >= t).astype(jnp.int32)
    large = jnp.minimum(large, NUM_BUCKETS // 2 - 1)
    bkt = jnp.where(rel > 0, NUM_BUCKETS // 2, 0) + jnp.where(na < NUM_BUCKETS // 4, na, large)
    bkts = [bkt[j * BLOCK:(j + 1) * BLOCK] for j in range(3)]
    mask_g = jnp.concatenate([mask] * SWA_GROUP, axis=1)

    def transpose_bf16(a):
        return a.astype(F32).T.astype(BF16)

    q = q_ref[0]
    for g in range(SWA_KV_HEADS):
        lo, hi = g * HEAD_DIM, (g + 1) * HEAD_DIM
        k3 = jnp.concatenate([kp_ref[0, :, lo:hi], kc_ref[0, :, lo:hi], kn_ref[0, :, lo:hi]], axis=0)
        v3t = jnp.concatenate([transpose_bf16(r[0, :, lo:hi]) for r in (vp_ref, vc_ref, vn_ref)], axis=1)
        heads = range(g * SWA_GROUP, (g + 1) * SWA_GROUP)
        qt = jnp.concatenate([transpose_bf16(q[:, hd * HEAD_DIM:(hd + 1) * HEAD_DIM]) for hd in heads], axis=1)
        biases = []
        for hd in heads:
            tab = jnp.broadcast_to(t5_ref[hd:hd + 1, :], (BLOCK, LANES))
            biases.append(jnp.concatenate([jnp.take_along_axis(tab, b, axis=1) for b in bkts], axis=0))
        s = jnp.dot(k3, qt, preferred_element_type=F32) + jnp.concatenate(biases, axis=1)
        s = jnp.where(mask_g, s, NEG)
        sk = jnp.concatenate([jnp.full((1, BLOCK), sink_ref[0, hd], F32) for hd in heads], axis=1)
        m = jnp.maximum(jnp.max(s, axis=0, keepdims=True), sk)
        e = jnp.exp(s - m)
        denom = jnp.sum(e, axis=0, keepdims=True) + jnp.exp(sk - m)
        ot = jnp.dot(v3t, e.astype(BF16), preferred_element_type=F32) / denom
        for j, hd in enumerate(heads):
            o_ref[0, :, hd * HEAD_DIM:(hd + 1) * HEAD_DIM] = ot[:, j * BLOCK:(j + 1) * BLOCK].T


def _swa_attn(t5, sink, sq, sk, sv, pos_col, pos_row):
    B, S, _ = sq.shape
    nb = S // BLOCK
    kvw = SWA_KV_HEADS * HEAD_DIM
    prev = lambda b, n: (b, jnp.maximum(n - 1, 0), 0)
    cur = lambda b, n: (b, n, 0)
    nxt = lambda b, n: (b, jnp.minimum(n + 1, nb - 1), 0)
    rcur = lambda b, n: (b, 0, n)
    smem = pl.BlockSpec(memory_space=pltpu.SMEM)
    return pl.pallas_call(
        functools.partial(_swa_kernel, nb=nb),
        grid=(B, nb),
        in_specs=[
            _const_spec(t5.shape), smem,
            pl.BlockSpec((1, BLOCK, SWA_HEADS * HEAD_DIM), cur),
            pl.BlockSpec((1, BLOCK, kvw), prev), pl.BlockSpec((1, BLOCK, kvw), cur), pl.BlockSpec((1, BLOCK, kvw), nxt),
            pl.BlockSpec((1, BLOCK, kvw), prev), pl.BlockSpec((1, BLOCK, kvw), cur), pl.BlockSpec((1, BLOCK, kvw), nxt),
            pl.BlockSpec((1, 1, BLOCK), rcur),
            pl.BlockSpec((1, BLOCK, 1), prev), pl.BlockSpec((1, BLOCK, 1), cur), pl.BlockSpec((1, BLOCK, 1), nxt),
        ],
        out_specs=pl.BlockSpec((1, BLOCK, SWA_HEADS * HEAD_DIM), cur),
        out_shape=jax.ShapeDtypeStruct((B, S, SWA_HEADS * HEAD_DIM), F32),
        compiler_params=_params(("parallel", "parallel")), name="swa_attn",
    )(t5, sink, sq, sk, sk, sk, sv, sv, sv, pos_row, pos_col, pos_col, pos_col)


def _out_proj_kernel(x_ref, mla_ref, swa_ref, g1_ref, g2_ref, wo_ref, gf_ref, xo_ref, h2_ref):
    w = wo_ref.shape[0] // 2
    m1 = _rms(mla_ref[...], g1_ref[...]).astype(BF16)
    m2 = _rms(swa_ref[...], g2_ref[...]).astype(BF16)
    y = jnp.dot(m1, wo_ref[0:w, :], preferred_element_type=F32)
    y = y + jnp.dot(m2, wo_ref[w:2 * w, :], preferred_element_type=F32)
    xn = x_ref[...] + y
    xo_ref[...] = xn
    h2_ref[...] = _rms(xn, gf_ref[...]).astype(BF16)


def _out_proj(x, mla_o, swa_o, g1, g2, wo, gf, *, tm):
    T, D = x.shape
    W = mla_o.shape[1]
    row = lambda i: (i, 0)
    return pl.pallas_call(
        _out_proj_kernel,
        grid=(T // tm,),
        in_specs=[
            pl.BlockSpec((tm, D), row), pl.BlockSpec((tm, W), row), pl.BlockSpec((tm, W), row),
            _const_spec(g1.shape), _const_spec(g2.shape), _const_spec(wo.shape), _const_spec(gf.shape),
        ],
        out_specs=(pl.BlockSpec((tm, D), row), pl.BlockSpec((tm, D), row)),
        out_shape=(jax.ShapeDtypeStruct((T, D), F32), jax.ShapeDtypeStruct((T, D), BF16)),
        compiler_params=_params(("parallel",)), name="out_proj",
    )(x, mla_o, swa_o, g1, g2, wo, gf)


def _ffn_kernel(hp_ref, h_ref, hn_ref, wg_ref, wu_ref, cw_ref, cb_ref, wd_ref, y_ref, hext_ref, *,
                tm, tiles_per_seq):
    i = pl.program_id(0)
    j = pl.program_id(1)
    halo = BF16_SUBLANES

    @pl.when(j == 0)
    def _():
        t = i % tiles_per_seq
        hext_ref[0:halo, :] = jnp.where(t == 0, jnp.zeros_like(hp_ref[...]), hp_ref[...])
        hext_ref[halo:halo + tm, :] = h_ref[...]
        hext_ref[halo + tm:, :] = jnp.where(t == tiles_per_seq - 1, jnp.zeros_like(hn_ref[...]), hn_ref[...])
        y_ref[...] = jnp.zeros_like(y_ref)

    gp = jnp.dot(hext_ref[...], wg_ref[...], preferred_element_type=F32)
    cw = cw_ref[...]
    g = (gp[halo - 1:halo - 1 + tm] * cw[0:1] + gp[halo:halo + tm] * cw[1:2]
         + gp[halo + 1:halo + 1 + tm] * cw[2:3] + cb_ref[...])
    u = jnp.dot(h_ref[...], wu_ref[...], preferred_element_type=F32)
    a = (g * _sigmoid(g) * u).astype(BF16)
    y_ref[...] += jnp.dot(a, wd_ref[...], preferred_element_type=F32)


def _ffn(h2, wg, wu, cw, cb, wd, *, tm, tf, seq):
    T, D = h2.shape
    Fp = wg.shape[1]
    halo = BF16_SUBLANES
    hb = tm // halo
    nhb = T // halo
    return pl.pallas_call(
        functools.partial(_ffn_kernel, tm=tm, tiles_per_seq=seq // tm),
        grid=(T // tm, Fp // tf),
        in_specs=[
            pl.BlockSpec((halo, D), lambda i, j: (jnp.maximum(i * hb - 1, 0), 0)),
            pl.BlockSpec((tm, D), lambda i, j: (i, 0)),
            pl.BlockSpec((halo, D), lambda i, j: (jnp.minimum((i + 1) * hb, nhb - 1), 0)),
            pl.BlockSpec((D, tf), lambda i, j: (0, j)),
            pl.BlockSpec((D, tf), lambda i, j: (0, j)),
            pl.BlockSpec((3, tf), lambda i, j: (0, j)),
            pl.BlockSpec((1, tf), lambda i, j: (0, j)),
            pl.BlockSpec((tf, D), lambda i, j: (j, 0)),
        ],
        out_specs=pl.BlockSpec((tm, D), lambda i, j: (i, 0)),
        out_shape=jax.ShapeDtypeStruct((T, D), F32),
        scratch_shapes=[pltpu.VMEM((tm + 2 * halo, D), BF16)],
        compiler_params=_params(("parallel", "arbitrary")), name="ffn",
    )(h2, h2, h2, wg, wu, cw, cb, wd)


def _ple_kernel(x_ref, y_ref, p_ref, wpg_ref, bpg_ref, wpp_ref, gfin_ref, o_ref, *, final):
    x2 = x_ref[...] + y_ref[...]
    gate = _sigmoid(jnp.dot(x2.astype(BF16), wpg_ref[...], preferred_element_type=F32) + bpg_ref[...])
    pp = jnp.dot(p_ref[...].astype(BF16), wpp_ref[...], preferred_element_type=F32)
    x3 = x2 + gate * pp
    if final:
        x3 = _rms(x3, gfin_ref[...])
    o_ref[...] = x3


def _ple(x, y, p, wpg, bpg, wpp, gfin, *, tm, final):
    T, D = x.shape
    P = p.shape[1]
    row = lambda i: (i, 0)
    return pl.pallas_call(
        functools.partial(_ple_kernel, final=final),
        grid=(T // tm,),
        in_specs=[
            pl.BlockSpec((tm, D), row), pl.BlockSpec((tm, D), row), pl.BlockSpec((tm, P), row),
            _const_spec(wpg.shape), _const_spec(bpg.shape), _const_spec(wpp.shape), _const_spec(gfin.shape),
        ],
        out_specs=pl.BlockSpec((tm, D), row),
        out_shape=jax.ShapeDtypeStruct((T, D), F32),
        compiler_params=_params(("parallel",)), name="ple",
    )(x, y, p, wpg, bpg, wpp, gfin)


def _prep_w_in(w_in):
    half = MLA_ROPE // 2
    kr0 = Q_RANK + KV_RANK
    t1, t2 = w_in[:, kr0:kr0 + half], w_in[:, kr0 + half:kr0 + MLA_ROPE]
    zpad = jnp.zeros((w_in.shape[0], LANES - MLA_ROPE), w_in.dtype)
    return jnp.concatenate(
        [w_in[:, :kr0], t1, t2, zpad, t2, t1, zpad, w_in[:, kr0 + MLA_ROPE:]], axis=1).astype(BF16)


def _prep_w_uq(w_uq):
    half = MLA_ROPE // 2
    w = w_uq.reshape(Q_RANK, MLA_HEADS, MLA_NOPE + MLA_ROPE)
    nope, t1, t2 = w[..., :MLA_NOPE], w[..., MLA_NOPE:MLA_NOPE + half], w[..., MLA_NOPE + half:]
    zpad = jnp.zeros((Q_RANK, MLA_HEADS, LANES - MLA_ROPE), w_uq.dtype)
    cols = jnp.concatenate([nope, t1, t2, zpad, t2, t1, zpad], axis=-1)
    return cols.reshape(Q_RANK, MLA_HEADS * 3 * LANES).T.astype(BF16)


def _prep_w_ukv(w_ukv):
    w = w_ukv.reshape(KV_RANK, MLA_HEADS, MLA_NOPE + MLA_V)
    wkn = w[..., :MLA_NOPE].reshape(KV_RANK, MLA_HEADS * MLA_NOPE).astype(BF16)
    wvt = w[..., MLA_NOPE:].reshape(KV_RANK, MLA_HEADS * MLA_V).T.astype(BF16)
    return wkn, wvt


def _pad_ff(w, axis, fp):
    pad = [(0, 0)] * w.ndim
    pad[axis] = (0, fp - w.shape[axis])
    return jnp.pad(w, pad)


def _pick_tile(n, pref):
    t = min(pref, n)
    while n % t:
        t //= 2
    return t


def kernel(x, p, positions, attn_norm, w_in, cq_norm, ckv_norm, w_uq, w_ukv, swa_sink, t5_bias,
           mla_out_norm, swa_out_norm, w_o, ffn_norm, w_gate, w_up, conv_w, conv_b, w_down,
           ple_gate_w, ple_gate_b, ple_proj, final_norm):
    B, S, D = x.shape
    depth = w_in.shape[0]
    T = B * S
    d_ff = w_gate.shape[-1]
    tf = 512
    fp = -(-d_ff // tf) * tf

    tm_in = _pick_tile(S, 512)
    tq = _pick_tile(S, 512)
    tk = _pick_tile(S, 512)
    tm_out = _pick_tile(S, 512)
    tm_ffn = _pick_tile(S, 1024)
    tm_ple = _pick_tile(S, 512)

    pos_row = positions.reshape(B, 1, S)
    pos_col = positions.reshape(B, S, 1)
    invf = (ROPE_THETA ** (-jnp.arange(0, MLA_ROPE, 2, dtype=F32) / MLA_ROPE)).reshape(MLA_ROPE // 2, 1)
    row = lambda v: v.reshape(1, -1)
    t5_tab = jnp.pad(t5_bias.T, ((0, 0), (0, LANES - NUM_BUCKETS)))

    xf = x
    for i in range(depth):
        wkn, wvt = _prep_w_ukv(w_ukv[i])
        qt, k, vt, sq, sk, sv = _in_proj(
            xf.reshape(B, S, D), pos_row, invf, row(attn_norm[i]), _prep_w_in(w_in[i]),
            row(cq_norm[i]), row(ckv_norm[i]), _prep_w_uq(w_uq[i]), wkn, wvt, tm=tm_in)
        mla_o = _mla_attn(qt, k, vt, tq=tq, tk=tk, q_tiles_per_step=2 if S % (2 * tq) == 0 else 1)
        swa_o = _swa_attn(t5_tab, row(swa_sink[i]), sq, sk, sv, pos_col, pos_row)
        x1, h2 = _out_proj(
            xf.reshape(T, D), mla_o.reshape(T, -1), swa_o.reshape(T, -1), row(mla_out_norm[i]),
            row(swa_out_norm[i]), w_o[i].astype(BF16), row(ffn_norm[i]), tm=tm_out)
        y = _ffn(
            h2, _pad_ff(w_gate[i], 1, fp).astype(BF16), _pad_ff(w_up[i], 1, fp).astype(BF16),
            _pad_ff(conv_w[i], 1, fp), _pad_ff(row(conv_b[i]), 1, fp),
            _pad_ff(w_down[i], 0, fp).astype(BF16), tm=tm_ffn, tf=tf, seq=S)
        xf = _ple(
            x1, y, p[i].reshape(T, -1), ple_gate_w[i].astype(BF16), row(ple_gate_b[i]),
            ple_proj[i].astype(BF16), row(final_norm), tm=tm_ple, final=(i == depth - 1))
    return xf.reshape(B, S, D)
```

```python
import functools
import math

import jax
import jax.numpy as jnp
from jax import lax
from jax.experimental import pallas as pl
from jax.experimental.pallas import tpu as pltpu

F32 = jnp.float32
BF16 = jnp.bfloat16

EPS = 1e-6
BLOCK = 128
MLA_HEADS = 8
MLA_NOPE = 128
MLA_ROPE = 64
MLA_V = 128
MLA_QK_PAD = 256
Q_RANK = 384
KV_RANK = 256
ROPE_THETA = 10000.0
SWA_HEADS = 8
SWA_KV_HEADS = 2
SWA_GROUP = SWA_HEADS // SWA_KV_HEADS
HEAD_DIM = 128
WINDOW = 128
NUM_BUCKETS = 32
NEG = -1e30
LOG2E = math.log2(math.e)

V7X_VMEM_BYTES = 64 * 1024 * 1024
VMEM_LIMIT_BYTES = V7X_VMEM_BYTES - 8 * 1024 * 1024
LANES = 128
BF16_SUBLANES = 16
MLA_V_EXT = MLA_V + BF16_SUBLANES

_Z_CQ = 0
_Z_CKV = _Z_CQ + Q_RANK
_Z_KA = _Z_CKV + KV_RANK
_Z_KB = _Z_KA + LANES
_Z_SQ = _Z_KB + LANES
_Z_SK = _Z_SQ + SWA_HEADS * HEAD_DIM
_Z_SV = _Z_SK + SWA_KV_HEADS * HEAD_DIM
_Z_END = _Z_SV + SWA_KV_HEADS * HEAD_DIM


def _params(sem):
    return pltpu.CompilerParams(dimension_semantics=sem, vmem_limit_bytes=VMEM_LIMIT_BYTES)


def _const_spec(shape):
    nd = len(shape)
    return pl.BlockSpec(shape, lambda *_: (0,) * nd, pipeline_mode=pl.Buffered(1))


def _rms(x, g):
    ms = jnp.mean(x * x, axis=-1, keepdims=True)
    return x * lax.rsqrt(ms + EPS) * g


def _sigmoid(x):
    return 1.0 / (1.0 + jnp.exp(-x))


def _in_proj_kernel(x_ref, pos_ref, invf_ref, g_ref, win_ref, cqn_ref, ckvn_ref, wq_ref, wkn_ref,
                    wvt_ref, qt_ref, k_ref, vt_ref, sq_ref, sk_ref, sv_ref, *, q_scale, swa_scale):
    x = x_ref[0]
    h = _rms(x, g_ref[...]).astype(BF16)
    z = jnp.dot(h, win_ref[...], preferred_element_type=F32)
    cq = _rms(z[:, _Z_CQ:_Z_CKV], cqn_ref[...]).astype(BF16)
    ckv = _rms(z[:, _Z_CKV:_Z_KA], ckvn_ref[...]).astype(BF16)

    ang = invf_ref[...] * pos_ref[0].astype(F32)
    cos_t, sin_t = jnp.cos(ang), jnp.sin(ang)
    zpad = jnp.zeros((LANES - MLA_ROPE, ang.shape[1]), F32)
    cc_t = jnp.concatenate([cos_t, cos_t, zpad], axis=0)
    ss_t = jnp.concatenate([-sin_t, sin_t, zpad], axis=0)

    r = lax.dot_general(wq_ref[...], cq, (((1,), (1,)), ((), ())), preferred_element_type=F32)
    for hd in range(MLA_HEADS):
        b = hd * 3 * LANES
        qt_ref[0, hd, 0:LANES, :] = (r[b:b + LANES] * q_scale).astype(BF16)
        roped = r[b + LANES:b + 2 * LANES] * cc_t + r[b + 2 * LANES:b + 3 * LANES] * ss_t
        qt_ref[0, hd, LANES:2 * LANES, :] = (roped * q_scale).astype(BF16)

    knope = jnp.dot(ckv, wkn_ref[...], preferred_element_type=F32)
    cc, ss = cc_t.T, ss_t.T
    krope = (z[:, _Z_KA:_Z_KB] * cc + z[:, _Z_KB:_Z_SQ] * ss).astype(BF16)
    for hd in range(MLA_HEADS):
        k_ref[0, hd, :, 0:LANES] = knope[:, hd * LANES:(hd + 1) * LANES].astype(BF16)
        k_ref[0, hd, :, LANES:MLA_QK_PAD] = krope

    vt = lax.dot_general(wvt_ref[...], ckv, (((1,), (1,)), ((), ())), preferred_element_type=F32)
    ones_row = (lax.broadcasted_iota(jnp.int32, (BF16_SUBLANES, vt.shape[1]), 0) == 0).astype(BF16)
    for hd in range(MLA_HEADS):
        vt_ref[0, hd, 0:MLA_V, :] = vt[hd * MLA_V:(hd + 1) * MLA_V].astype(BF16)
        vt_ref[0, hd, MLA_V:MLA_V_EXT, :] = ones_row

    sq_ref[0] = (z[:, _Z_SQ:_Z_SK] * swa_scale).astype(BF16)
    sk_ref[0] = z[:, _Z_SK:_Z_SV].astype(BF16)
    sv_ref[0] = z[:, _Z_SV:_Z_END].astype(BF16)


def _in_proj(x, pos_row, invf, g, win, cqn, ckvn, wq, wkn, wvt, *, tm):
    B, S, D = x.shape
    grid = (B, S // tm)
    q_scale = LOG2E / math.sqrt(MLA_NOPE + MLA_ROPE)
    swa_scale = 1.0 / math.sqrt(HEAD_DIM)
    out_shape = (
        jax.ShapeDtypeStruct((B, MLA_HEADS, MLA_QK_PAD, S), BF16),
        jax.ShapeDtypeStruct((B, MLA_HEADS, S, MLA_QK_PAD), BF16),
        jax.ShapeDtypeStruct((B, MLA_HEADS, MLA_V_EXT, S), BF16),
        jax.ShapeDtypeStruct((B, S, SWA_HEADS * HEAD_DIM), BF16),
        jax.ShapeDtypeStruct((B, S, SWA_KV_HEADS * HEAD_DIM), BF16),
        jax.ShapeDtypeStruct((B, S, SWA_KV_HEADS * HEAD_DIM), BF16),
    )
    in_specs = [
        pl.BlockSpec((1, tm, D), lambda b, i: (b, i, 0)),
        pl.BlockSpec((1, 1, tm), lambda b, i: (b, 0, i)),
        _const_spec(invf.shape), _const_spec(g.shape), _const_spec(win.shape),
        _const_spec(cqn.shape), _const_spec(ckvn.shape), _const_spec(wq.shape),
        _const_spec(wkn.shape), _const_spec(wvt.shape),
    ]
    out_specs = (
        pl.BlockSpec((1, MLA_HEADS, MLA_QK_PAD, tm), lambda b, i: (b, 0, 0, i)),
        pl.BlockSpec((1, MLA_HEADS, tm, MLA_QK_PAD), lambda b, i: (b, 0, i, 0)),
        pl.BlockSpec((1, MLA_HEADS, MLA_V_EXT, tm), lambda b, i: (b, 0, 0, i)),
        pl.BlockSpec((1, tm, SWA_HEADS * HEAD_DIM), lambda b, i: (b, i, 0)),
        pl.BlockSpec((1, tm, SWA_KV_HEADS * HEAD_DIM), lambda b, i: (b, i, 0)),
        pl.BlockSpec((1, tm, SWA_KV_HEADS * HEAD_DIM), lambda b, i: (b, i, 0)),
    )
    return pl.pallas_call(
        functools.partial(_in_proj_kernel, q_scale=q_scale, swa_scale=swa_scale),
        grid=grid, in_specs=in_specs, out_specs=out_specs, out_shape=out_shape,
        compiler_params=_params(("parallel", "parallel")), name="in_proj",
    )(x, pos_row, invf, g, win, cqn, ckvn, wq, wkn, wvt)


def _mla_kernel(qt_ref, k_ref, vt_ref, o_ref, m_ref, acc_ref, s_ref, mc_ref, *, tq, tk, unroll, streams):
    S = k_ref.shape[2]
    n = S // tk

    def q_group(t, carry):
        q0s = [pl.multiple_of((t * streams + j) * tq, tq) for j in range(streams)]
        for j in range(streams):
            m_ref[j] = jnp.full(m_ref.shape[1:], NEG, F32)
            acc_ref[j] = jnp.zeros(acc_ref.shape[1:], F32)

        def scores(c, j, slot):
            start = pl.multiple_of(c * tk, tk)
            s = jnp.dot(k_ref[0, 0, pl.ds(start, tk), :], qt_ref[0, 0, :, pl.ds(q0s[j], tq)],
                        preferred_element_type=F32)
            s_ref[2 * j + slot] = s
            mc_ref[2 * j + slot] = jnp.max(s, axis=0, keepdims=True)

        def softmax_pv(c, j, slot):
            start = pl.multiple_of(c * tk, tk)
            m_prev = m_ref[j]
            m_new = jnp.maximum(m_prev, mc_ref[2 * j + slot])
            alpha = jnp.exp2(m_prev - m_new)
            p = jnp.exp2(s_ref[2 * j + slot] - m_new).astype(BF16)
            vc = vt_ref[0, 0, :, pl.ds(start, tk)]
            acc_ref[j] = alpha * acc_ref[j] + jnp.dot(vc, p, preferred_element_type=F32)
            m_ref[j] = m_new

        for j in range(streams):
            scores(0, j, 0)

        def group(i, carry):
            c0 = i * unroll
            for u in range(unroll):
                nxt = c0 + u + 1
                if u == unroll - 1:
                    nxt = jnp.where(nxt == n, 0, nxt)
                for j in range(streams):
                    scores(nxt, j, (u + 1) % 2)
                    softmax_pv(c0 + u, j, u % 2)
            return carry

        lax.fori_loop(0, n // unroll, group, 0)
        for j in range(streams):
            acc = acc_ref[j]
            o_ref[0, pl.ds(q0s[j], tq), :] = (acc[0:MLA_V] / acc[MLA_V:MLA_V + 1]).T
        return carry

    lax.fori_loop(0, qt_ref.shape[3] // (tq * streams), q_group, 0)


def _mla_attn(qt, k, vt, *, tq, tk, q_tiles_per_step):
    B, H, _, S = qt.shape
    vrows = vt.shape[2]
    n = S // tk
    streams = 2 if q_tiles_per_step % 2 == 0 else 1
    unroll = max(u for u in (2, 4, 8, 16) if n % u == 0)
    tqs = tq * q_tiles_per_step
    grid = (B, H, S // tqs)
    return pl.pallas_call(
        functools.partial(_mla_kernel, tq=tq, tk=tk, unroll=unroll, streams=streams),
        grid=grid,
        in_specs=[
            pl.BlockSpec((1, 1, MLA_QK_PAD, tqs), lambda b, h, i: (b, h, 0, i)),
            pl.BlockSpec((1, 1, S, MLA_QK_PAD), lambda b, h, i: (b, h, 0, 0)),
            pl.BlockSpec((1, 1, vrows, S), lambda b, h, i: (b, h, 0, 0)),
        ],
        out_specs=pl.BlockSpec((1, tqs, MLA_V), lambda b, h, i: (b, i, h)),
        out_shape=jax.ShapeDtypeStruct((B, S, H * MLA_V), F32),
        scratch_shapes=[pltpu.VMEM((streams, 1, tq), F32), pltpu.VMEM((streams, vrows, tq), F32),
                        pltpu.VMEM((2 * streams, tk, tq), F32), pltpu.VMEM((2 * streams, 1, tq), F32)],
        compiler_params=_params(("parallel", "parallel", "arbitrary")), name="mla_attn",
    )(qt, k, vt)


_T5_STEPS = (12, 16, 23, 32, 46, 64, 91, 128)


def _swa_kernel(t5_ref, sink_ref, q_ref, kp_ref, kc_ref, kn_ref, vp_ref, vc_ref, vn_ref,
                pq_ref, pp_ref, pc_ref, pn_ref, o_ref, *, nb):
    n = pl.program_id(1)
    pq = pq_ref[0]
    pk = jnp.concatenate([pp_ref[0], pc_ref[0], pn_ref[0]], axis=0)
    rel = pk - pq
    row = lax.broadcasted_iota(jnp.int32, rel.shape, 0)
    lo = jnp.where(n > 0, 0, BLOCK)
    hi = jnp.where(n < nb - 1, 3 * BLOCK, 2 * BLOCK)
    na = jnp.abs(rel)
    mask = (na <= WINDOW) & (row >= lo) & (row < hi)
    large = jnp.full(na.shape, NUM_BUCKETS // 4, jnp.int32)
    for t in _T5_STEPS:
        large = large + (na >= t).astype(jnp.int32)
    large = jnp.minimum(large, NUM_BUCKETS // 2 - 1)
    bkt = jnp.where(rel > 0, NUM_BUCKETS // 2, 0) + jnp.where(na < NUM_BUCKETS // 4, na, large)
    bkts = [bkt[j * BLOCK:(j + 1) * BLOCK] for j in range(3)]
    mask_g = jnp.concatenate([mask] * SWA_GROUP, axis=1)

    def transpose_bf16(a):
        return a.astype(F32).T.astype(BF16)

    q = q_ref[0]
    for g in range(SWA_KV_HEADS):
        lo, hi = g * HEAD_DIM, (g + 1) * HEAD_DIM
        k3 = jnp.concatenate([kp_ref[0, :, lo:hi], kc_ref[0, :, lo:hi], kn_ref[0, :, lo:hi]], axis=0)
        v3t = jnp.concatenate([transpose_bf16(r[0, :, lo:hi]) for r in (vp_ref, vc_ref, vn_ref)], axis=1)
        heads = range(g * SWA_GROUP, (g + 1) * SWA_GROUP)
        qt = jnp.concatenate([transpose_bf16(q[:, hd * HEAD_DIM:(hd + 1) * HEAD_DIM]) for hd in heads], axis=1)
        biases = []
        for hd in heads:
            tab = jnp.broadcast_to(t5_ref[hd:hd + 1, :], (BLOCK, LANES))
            biases.append(jnp.concatenate([jnp.take_along_axis(tab, b, axis=1) for b in bkts], axis=0))
        s = jnp.dot(k3, qt, preferred_element_type=F32) + jnp.concatenate(biases, axis=1)
        s = jnp.where(mask_g, s, NEG)
        sk = jnp.concatenate([jnp.full((1, BLOCK), sink_ref[0, hd], F32) for hd in heads], axis=1)
        m = jnp.maximum(jnp.max(s, axis=0, keepdims=True), sk)
        e = jnp.exp(s - m)
        denom = jnp.sum(e, axis=0, keepdims=True) + jnp.exp(sk - m)
        ot = jnp.dot(v3t, e.astype(BF16), preferred_element_type=F32) / denom
        for j, hd in enumerate(heads):
            o_ref[0, :, hd * HEAD_DIM:(hd + 1) * HEAD_DIM] = ot[:, j * BLOCK:(j + 1) * BLOCK].T


def _swa_attn(t5, sink, sq, sk, sv, pos_col, pos_row):
    B, S, _ = sq.shape
    nb = S // BLOCK
    kvw = SWA_KV_HEADS * HEAD_DIM
    prev = lambda b, n: (b, jnp.maximum(n - 1, 0), 0)
    cur = lambda b, n: (b, n, 0)
    nxt = lambda b, n: (b, jnp.minimum(n + 1, nb - 1), 0)
    rcur = lambda b, n: (b, 0, n)
    smem = pl.BlockSpec(memory_space=pltpu.SMEM)
    return pl.pallas_call(
        functools.partial(_swa_kernel, nb=nb),
        grid=(B, nb),
        in_specs=[
            _const_spec(t5.shape), smem,
            pl.BlockSpec((1, BLOCK, SWA_HEADS * HEAD_DIM), cur),
            pl.BlockSpec((1, BLOCK, kvw), prev), pl.BlockSpec((1, BLOCK, kvw), cur), pl.BlockSpec((1, BLOCK, kvw), nxt),
            pl.BlockSpec((1, BLOCK, kvw), prev), pl.BlockSpec((1, BLOCK, kvw), cur), pl.BlockSpec((1, BLOCK, kvw), nxt),
            pl.BlockSpec((1, 1, BLOCK), rcur),
            pl.BlockSpec((1, BLOCK, 1), prev), pl.BlockSpec((1, BLOCK, 1), cur), pl.BlockSpec((1, BLOCK, 1), nxt),
        ],
        out_specs=pl.BlockSpec((1, BLOCK, SWA_HEADS * HEAD_DIM), cur),
        out_shape=jax.ShapeDtypeStruct((B, S, SWA_HEADS * HEAD_DIM), F32),
        compiler_params=_params(("parallel", "parallel")), name="swa_attn",
    )(t5, sink, sq, sk, sk, sk, sv, sv, sv, pos_row, pos_col, pos_col, pos_col)


def _out_proj_kernel(x_ref, mla_ref, swa_ref, g1_ref, g2_ref, wo_ref, gf_ref, xo_ref, h2_ref):
    w = wo_ref.shape[0] // 2
    m1 = _rms(mla_ref[...], g1_ref[...]).astype(BF16)
    m2 = _rms(swa_ref[...], g2_ref[...]).astype(BF16)
    y = jnp.dot(m1, wo_ref[0:w, :], preferred_element_type=F32)
    y = y + jnp.dot(m2, wo_ref[w:2 * w, :], preferred_element_type=F32)
    xn = x_ref[...] + y
    xo_ref[...] = xn
    h2_ref[...] = _rms(xn, gf_ref[...]).astype(BF16)


def _out_proj(x, mla_o, swa_o, g1, g2, wo, gf, *, tm):
    T, D = x.shape
    W = mla_o.shape[1]
    row = lambda i: (i, 0)
    return pl.pallas_call(
        _out_proj_kernel,
        grid=(T // tm,),
        in_specs=[
            pl.BlockSpec((tm, D), row), pl.BlockSpec((tm, W), row), pl.BlockSpec((tm, W), row),
            _const_spec(g1.shape), _const_spec(g2.shape), _const_spec(wo.shape), _const_spec(gf.shape),
        ],
        out_specs=(pl.BlockSpec((tm, D), row), pl.BlockSpec((tm, D), row)),
        out_shape=(jax.ShapeDtypeStruct((T, D), F32), jax.ShapeDtypeStruct((T, D), BF16)),
        compiler_params=_params(("parallel",)), name="out_proj",
    )(x, mla_o, swa_o, g1, g2, wo, gf)


def _ffn_kernel(hp_ref, h_ref, hn_ref, wg_ref, wu_ref, cw_ref, cb_ref, wd_ref, y_ref, hext_ref, *,
                tm, tiles_per_seq):
    i = pl.program_id(0)
    j = pl.program_id(1)
    halo = BF16_SUBLANES

    @pl.when(j == 0)
    def _():
        t = i % tiles_per_seq
        hext_ref[0:halo, :] = jnp.where(t == 0, jnp.zeros_like(hp_ref[...]), hp_ref[...])
        hext_ref[halo:halo + tm, :] = h_ref[...]
        hext_ref[halo + tm:, :] = jnp.where(t == tiles_per_seq - 1, jnp.zeros_like(hn_ref[...]), hn_ref[...])
        y_ref[...] = jnp.zeros_like(y_ref)

    gp = jnp.dot(hext_ref[...], wg_ref[...], preferred_element_type=F32)
    cw = cw_ref[...]
    g = (gp[halo - 1:halo - 1 + tm] * cw[0:1] + gp[halo:halo + tm] * cw[1:2]
         + gp[halo + 1:halo + 1 + tm] * cw[2:3] + cb_ref[...])
    u = jnp.dot(h_ref[...], wu_ref[...], preferred_element_type=F32)
    a = (g * _sigmoid(g) * u).astype(BF16)
    y_ref[...] += jnp.dot(a, wd_ref[...], preferred_element_type=F32)


def _ffn(h2, wg, wu, cw, cb, wd, *, tm, tf, seq):
    T, D = h2.shape
    Fp = wg.shape[1]
    halo = BF16_SUBLANES
    hb = tm // halo
    nhb = T // halo
    return pl.pallas_call(
        functools.partial(_ffn_kernel, tm=tm, tiles_per_seq=seq // tm),
        grid=(T // tm, Fp // tf),
        in_specs=[
            pl.BlockSpec((halo, D), lambda i, j: (jnp.maximum(i * hb - 1, 0), 0)),
            pl.BlockSpec((tm, D), lambda i, j: (i, 0)),
            pl.BlockSpec((halo, D), lambda i, j: (jnp.minimum((i + 1) * hb, nhb - 1), 0)),
            pl.BlockSpec((D, tf), lambda i, j: (0, j)),
            pl.BlockSpec((D, tf), lambda i, j: (0, j)),
            pl.BlockSpec((3, tf), lambda i, j: (0, j)),
            pl.BlockSpec((1, tf), lambda i, j: (0, j)),
            pl.BlockSpec((tf, D), lambda i, j: (j, 0)),
        ],
        out_specs=pl.BlockSpec((tm, D), lambda i, j: (i, 0)),
        out_shape=jax.ShapeDtypeStruct((T, D), F32),
        scratch_shapes=[pltpu.VMEM((tm + 2 * halo, D), BF16)],
        compiler_params=_params(("parallel", "arbitrary")), name="ffn",
    )(h2, h2, h2, wg, wu, cw, cb, wd)


def _ple_kernel(x_ref, y_ref, p_ref, wpg_ref, bpg_ref, wpp_ref, gfin_ref, o_ref, *, final):
    x2 = x_ref[...] + y_ref[...]
    gate = _sigmoid(jnp.dot(x2.astype(BF16), wpg_ref[...], preferred_element_type=F32) + bpg_ref[...])
    pp = jnp.dot(p_ref[...].astype(BF16), wpp_ref[...], preferred_element_type=F32)
    x3 = x2 + gate * pp
    if final:
        x3 = _rms(x3, gfin_ref[...])
    o_ref[...] = x3


def _ple(x, y, p, wpg, bpg, wpp, gfin, *, tm, final):
    T, D = x.shape
    P = p.shape[1]
    row = lambda i: (i, 0)
    return pl.pallas_call(
        functools.partial(_ple_kernel, final=final),
        grid=(T // tm,),
        in_specs=[
            pl.BlockSpec((tm, D), row), pl.BlockSpec((tm, D), row), pl.BlockSpec((tm, P), row),
            _const_spec(wpg.shape), _const_spec(bpg.shape), _const_spec(wpp.shape), _const_spec(gfin.shape),
        ],
        out_specs=pl.BlockSpec((tm, D), row),
        out_shape=jax.ShapeDtypeStruct((T, D), F32),
        compiler_params=_params(("parallel",)), name="ple",
    )(x, y, p, wpg, bpg, wpp, gfin)


def _prep_w_in(w_in):
    half = MLA_ROPE // 2
    kr0 = Q_RANK + KV_RANK
    t1, t2 = w_in[:, kr0:kr0 + half], w_in[:, kr0 + half:kr0 + MLA_ROPE]
    zpad = jnp.zeros((w_in.shape[0], LANES - MLA_ROPE), w_in.dtype)
    return jnp.concatenate(
        [w_in[:, :kr0], t1, t2, zpad, t2, t1, zpad, w_in[:, kr0 + MLA_ROPE:]], axis=1).astype(BF16)


def _prep_w_uq(w_uq):
    half = MLA_ROPE // 2
    w = w_uq.reshape(Q_RANK, MLA_HEADS, MLA_NOPE + MLA_ROPE)
    nope, t1, t2 = w[..., :MLA_NOPE], w[..., MLA_NOPE:MLA_NOPE + half], w[..., MLA_NOPE + half:]
    zpad = jnp.zeros((Q_RANK, MLA_HEADS, LANES - MLA_ROPE), w_uq.dtype)
    cols = jnp.concatenate([nope, t1, t2, zpad, t2, t1, zpad], axis=-1)
    return cols.reshape(Q_RANK, MLA_HEADS * 3 * LANES).T.astype(BF16)


def _prep_w_ukv(w_ukv):
    w = w_ukv.reshape(KV_RANK, MLA_HEADS, MLA_NOPE + MLA_V)
    wkn = w[..., :MLA_NOPE].reshape(KV_RANK, MLA_HEADS * MLA_NOPE).astype(BF16)
    wvt = w[..., MLA_NOPE:].reshape(KV_RANK, MLA_HEADS * MLA_V).T.astype(BF16)
    return wkn, wvt


def _pad_ff(w, axis, fp):
    pad = [(0, 0)] * w.ndim
    pad[axis] = (0, fp - w.shape[axis])
    return jnp.pad(w, pad)


def _pick_tile(n, pref):
    t = min(pref, n)
    while n % t:
        t //= 2
    return t


def kernel(x, p, positions, attn_norm, w_in, cq_norm, ckv_norm, w_uq, w_ukv, swa_sink, t5_bias,
           mla_out_norm, swa_out_norm, w_o, ffn_norm, w_gate, w_up, conv_w, conv_b, w_down,
           ple_gate_w, ple_gate_b, ple_proj, final_norm):
    B, S, D = x.shape
    depth = w_in.shape[0]
    T = B * S
    d_ff = w_gate.shape[-1]
    tf = 512
    fp = -(-d_ff // tf) * tf

    tm_in = _pick_tile(S, 512)
    tq = _pick_tile(S, 512)
    tk = _pick_tile(S, 512)
    tm_out = _pick_tile(S, 512)
    tm_ffn = _pick_tile(S, 1024)
    tm_ple = _pick_tile(S, 512)

    pos_row = positions.reshape(B, 1, S)
    pos_col = positions.reshape(B, S, 1)
    invf = (ROPE_THETA ** (-jnp.arange(0, MLA_ROPE, 2, dtype=F32) / MLA_ROPE)).reshape(MLA_ROPE // 2, 1)
    row = lambda v: v.reshape(1, -1)
    t5_tab = jnp.pad(t5_bias.T, ((0, 0), (0, LANES - NUM_BUCKETS)))

    xf = x
    for i in range(depth):
        wkn, wvt = _prep_w_ukv(w_ukv[i])
        qt, k, vt, sq, sk, sv = _in_proj(
            xf.reshape(B, S, D), pos_row, invf, row(attn_norm[i]), _prep_w_in(w_in[i]),
            row(cq_norm[i]), row(ckv_norm[i]), _prep_w_uq(w_uq[i]), wkn, wvt, tm=tm_in)
        mla_o = _mla_attn(qt, k, vt, tq=tq, tk=tk, q_tiles_per_step=2 if S % (2 * tq) == 0 else 1)
        swa_o = _swa_attn(t5_tab, row(swa_sink[i]), sq, sk, sv, pos_col, pos_row)
        x1, h2 = _out_proj(
            xf.reshape(T, D), mla_o.reshape(T, -1), swa_o.reshape(T, -1), row(mla_out_norm[i]),
            row(swa_out_norm[i]), w_o[i].astype(BF16), row(ffn_norm[i]), tm=tm_out)
        y = _ffn(
            h2, _pad_ff(w_gate[i].astype(BF16), 1, fp), _pad_ff(w_up[i].astype(BF16), 1, fp),
            _pad_ff(conv_w[i], 1, fp), _pad_ff(row(conv_b[i]), 1, fp),
            _pad_ff(w_down[i].astype(BF16), 0, fp), tm=tm_ffn, tf=tf, seq=S)
        xf = _ple(
            x1, y, p[i].reshape(T, -1), ple_gate_w[i].astype(BF16), row(ple_gate_b[i]),
            ple_proj[i].astype(BF16), row(final_norm), tm=tm_ple, final=(i == depth - 1))
    return xf.reshape(B, S, D)
```

```python
import functools
import math

import jax
import jax.numpy as jnp
from jax import lax
from jax.experimental import pallas as pl
from jax.experimental.pallas import tpu as pltpu

F32 = jnp.float32
BF16 = jnp.bfloat16

EPS = 1e-6
BLOCK = 128
MLA_HEADS = 8
MLA_NOPE = 128
MLA_ROPE = 64
MLA_V = 128
MLA_QK_PAD = 256
Q_RANK = 384
KV_RANK = 256
ROPE_THETA = 10000.0
SWA_HEADS = 8
SWA_KV_HEADS = 2
SWA_GROUP = SWA_HEADS // SWA_KV_HEADS
HEAD_DIM = 128
WINDOW = 128
NUM_BUCKETS = 32
NEG = -1e30
LOG2E = math.log2(math.e)

V7X_VMEM_BYTES = 64 * 1024 * 1024
VMEM_LIMIT_BYTES = V7X_VMEM_BYTES - 8 * 1024 * 1024
LANES = 128
BF16_SUBLANES = 16
MLA_V_EXT = MLA_V + BF16_SUBLANES

_Z_CQ = 0
_Z_CKV = _Z_CQ + Q_RANK
_Z_KA = _Z_CKV + KV_RANK
_Z_KB = _Z_KA + LANES
_Z_SQ = _Z_KB + LANES
_Z_SK = _Z_SQ + SWA_HEADS * HEAD_DIM
_Z_SV = _Z_SK + SWA_KV_HEADS * HEAD_DIM
_Z_END = _Z_SV + SWA_KV_HEADS * HEAD_DIM


def _params(sem):
    return pltpu.CompilerParams(dimension_semantics=sem, vmem_limit_bytes=VMEM_LIMIT_BYTES)


def _const_spec(shape):
    nd = len(shape)
    return pl.BlockSpec(shape, lambda *_: (0,) * nd, pipeline_mode=pl.Buffered(1))


def _rms(x, g):
    ms = jnp.mean(x * x, axis=-1, keepdims=True)
    return x * lax.rsqrt(ms + EPS) * g


def _sigmoid(x):
    return 1.0 / (1.0 + jnp.exp(-x))


def _in_proj_kernel(x_ref, pos_ref, invf_ref, g_ref, win_ref, cqn_ref, ckvn_ref, wq_ref, wkn_ref,
                    wvt_ref, qt_ref, k_ref, vt_ref, sq_ref, sk_ref, sv_ref, *, q_scale, swa_scale):
    x = x_ref[0]
    h = _rms(x, g_ref[...]).astype(BF16)
    z = jnp.dot(h, win_ref[...], preferred_element_type=F32)
    cq = _rms(z[:, _Z_CQ:_Z_CKV], cqn_ref[...]).astype(BF16)
    ckv = _rms(z[:, _Z_CKV:_Z_KA], ckvn_ref[...]).astype(BF16)

    ang = invf_ref[...] * pos_ref[0].astype(F32)
    cos_t, sin_t = jnp.cos(ang), jnp.sin(ang)
    zpad = jnp.zeros((LANES - MLA_ROPE, ang.shape[1]), F32)
    cc_t = jnp.concatenate([cos_t, cos_t, zpad], axis=0)
    ss_t = jnp.concatenate([-sin_t, sin_t, zpad], axis=0)

    r = lax.dot_general(wq_ref[...], cq, (((1,), (1,)), ((), ())), preferred_element_type=F32)
    for hd in range(MLA_HEADS):
        b = hd * 3 * LANES
        qt_ref[0, hd, 0:LANES, :] = (r[b:b + LANES] * q_scale).astype(BF16)
        roped = r[b + LANES:b + 2 * LANES] * cc_t + r[b + 2 * LANES:b + 3 * LANES] * ss_t
        qt_ref[0, hd, LANES:2 * LANES, :] = (roped * q_scale).astype(BF16)

    knope = jnp.dot(ckv, wkn_ref[...], preferred_element_type=F32)
    cc, ss = cc_t.T, ss_t.T
    krope = (z[:, _Z_KA:_Z_KB] * cc + z[:, _Z_KB:_Z_SQ] * ss).astype(BF16)
    for hd in range(MLA_HEADS):
        k_ref[0, hd, :, 0:LANES] = knope[:, hd * LANES:(hd + 1) * LANES].astype(BF16)
        k_ref[0, hd, :, LANES:MLA_QK_PAD] = krope

    vt = lax.dot_general(wvt_ref[...], ckv, (((1,), (1,)), ((), ())), preferred_element_type=F32)
    ones_row = (lax.broadcasted_iota(jnp.int32, (BF16_SUBLANES, vt.shape[1]), 0) == 0).astype(BF16)
    for hd in range(MLA_HEADS):
        vt_ref[0, hd, 0:MLA_V, :] = vt[hd * MLA_V:(hd + 1) * MLA_V].astype(BF16)
        vt_ref[0, hd, MLA_V:MLA_V_EXT, :] = ones_row

    sq_ref[0] = (z[:, _Z_SQ:_Z_SK] * swa_scale).astype(BF16)
    sk_ref[0] = z[:, _Z_SK:_Z_SV].astype(BF16)
    sv_ref[0] = z[:, _Z_SV:_Z_END].astype(BF16)


def _in_proj(x, pos_row, invf, g, win, cqn, ckvn, wq, wkn, wvt, *, tm):
    B, S, D = x.shape
    grid = (B, S // tm)
    q_scale = LOG2E / math.sqrt(MLA_NOPE + MLA_ROPE)
    swa_scale = 1.0 / math.sqrt(HEAD_DIM)
    out_shape = (
        jax.ShapeDtypeStruct((B, MLA_HEADS, MLA_QK_PAD, S), BF16),
        jax.ShapeDtypeStruct((B, MLA_HEADS, S, MLA_QK_PAD), BF16),
        jax.ShapeDtypeStruct((B, MLA_HEADS, MLA_V_EXT, S), BF16),
        jax.ShapeDtypeStruct((B, S, SWA_HEADS * HEAD_DIM), BF16),
        jax.ShapeDtypeStruct((B, S, SWA_KV_HEADS * HEAD_DIM), BF16),
        jax.ShapeDtypeStruct((B, S, SWA_KV_HEADS * HEAD_DIM), BF16),
    )
    in_specs = [
        pl.BlockSpec((1, tm, D), lambda b, i: (b, i, 0)),
        pl.BlockSpec((1, 1, tm), lambda b, i: (b, 0, i)),
        _const_spec(invf.shape), _const_spec(g.shape), _const_spec(win.shape),
        _const_spec(cqn.shape), _const_spec(ckvn.shape), _const_spec(wq.shape),
        _const_spec(wkn.shape), _const_spec(wvt.shape),
    ]
    out_specs = (
        pl.BlockSpec((1, MLA_HEADS, MLA_QK_PAD, tm), lambda b, i: (b, 0, 0, i)),
        pl.BlockSpec((1, MLA_HEADS, tm, MLA_QK_PAD), lambda b, i: (b, 0, i, 0)),
        pl.BlockSpec((1, MLA_HEADS, MLA_V_EXT, tm), lambda b, i: (b, 0, 0, i)),
        pl.BlockSpec((1, tm, SWA_HEADS * HEAD_DIM), lambda b, i: (b, i, 0)),
        pl.BlockSpec((1, tm, SWA_KV_HEADS * HEAD_DIM), lambda b, i: (b, i, 0)),
        pl.BlockSpec((1, tm, SWA_KV_HEADS * HEAD_DIM), lambda b, i: (b, i, 0)),
    )
    return pl.pallas_call(
        functools.partial(_in_proj_kernel, q_scale=q_scale, swa_scale=swa_scale),
        grid=grid, in_specs=in_specs, out_specs=out_specs, out_shape=out_shape,
        compiler_params=_params(("parallel", "parallel")), name="in_proj",
    )(x, pos_row, invf, g, win, cqn, ckvn, wq, wkn, wvt)


def _mla_kernel(qt_ref, qn_ref, k_ref, vt_ref, o_ref, m_ref, acc_ref, s_ref, mc_ref, *, tq, tk, unroll, streams):
    S = k_ref.shape[2]
    n = S // tk
    trips = n // unroll

    def scores(c, j, slot, from_next=None):
        start = pl.multiple_of(c * tk, tk)
        q = qt_ref[0, 0, :, j * tq:(j + 1) * tq]
        if from_next is not None:
            q = jnp.where(from_next, qn_ref[0, 0, :, j * tq:(j + 1) * tq], q)
        s = jnp.dot(k_ref[0, 0, pl.ds(start, tk), :], q, preferred_element_type=F32)
        s_ref[2 * j + slot] = s
        mc_ref[2 * j + slot] = jnp.max(s, axis=0, keepdims=True)

    def softmax_pv(c, j, slot):
        start = pl.multiple_of(c * tk, tk)
        m_prev = m_ref[j]
        m_new = jnp.maximum(m_prev, mc_ref[2 * j + slot])
        alpha = jnp.exp2(m_prev - m_new)
        p = jnp.exp2(s_ref[2 * j + slot] - m_new).astype(BF16)
        vc = vt_ref[0, 0, :, pl.ds(start, tk)]
        acc_ref[j] = alpha * acc_ref[j] + jnp.dot(vc, p, preferred_element_type=F32)
        m_ref[j] = m_new

    @pl.when(pl.program_id(2) == 0)
    def _():
        for j in range(streams):
            scores(0, j, 0)

    for j in range(streams):
        m_ref[j] = jnp.full(m_ref.shape[1:], NEG, F32)
        acc_ref[j] = jnp.zeros(acc_ref.shape[1:], F32)

    def group(i, carry):
        c0 = i * unroll
        last = i == trips - 1
        for u in range(unroll):
            for j in range(streams):
                if u == unroll - 1:
                    scores(jnp.where(last, 0, c0 + u + 1), j, (u + 1) % 2, from_next=last)
                else:
                    scores(c0 + u + 1, j, (u + 1) % 2)
                softmax_pv(c0 + u, j, u % 2)
        return carry

    lax.fori_loop(0, trips, group, 0)
    for j in range(streams):
        acc = acc_ref[j]
        o_ref[0, j * tq:(j + 1) * tq, :] = (acc[0:MLA_V] / acc[MLA_V:MLA_V + 1]).T


def _mla_attn(qt, k, vt, *, tq, tk, streams):
    B, H, _, S = qt.shape
    vrows = vt.shape[2]
    n = S // tk
    unroll = max(u for u in (2, 4, 8, 16) if n % u == 0 and n // u >= 2)
    tqs = tq * streams
    steps = S // tqs
    return pl.pallas_call(
        functools.partial(_mla_kernel, tq=tq, tk=tk, unroll=unroll, streams=streams),
        grid=(B, H, steps),
        in_specs=[
            pl.BlockSpec((1, 1, MLA_QK_PAD, tqs), lambda b, h, i: (b, h, 0, i)),
            pl.BlockSpec((1, 1, MLA_QK_PAD, tqs), lambda b, h, i: (b, h, 0, jnp.minimum(i + 1, steps - 1))),
            pl.BlockSpec((1, 1, S, MLA_QK_PAD), lambda b, h, i: (b, h, 0, 0)),
            pl.BlockSpec((1, 1, vrows, S), lambda b, h, i: (b, h, 0, 0)),
        ],
        out_specs=pl.BlockSpec((1, tqs, MLA_V), lambda b, h, i: (b, i, h)),
        out_shape=jax.ShapeDtypeStruct((B, S, H * MLA_V), F32),
        scratch_shapes=[pltpu.VMEM((streams, 1, tq), F32), pltpu.VMEM((streams, vrows, tq), F32),
                        pltpu.VMEM((2 * streams, tk, tq), F32), pltpu.VMEM((2 * streams, 1, tq), F32)],
        compiler_params=_params(("arbitrary", "arbitrary", "arbitrary")), name="mla_attn",
    )(qt, qt, k, vt)


_T5_STEPS = (12, 16, 23, 32, 46, 64, 91, 128)


def _swa_kernel(t5_ref, sink_ref, q_ref, kp_ref, kc_ref, kn_ref, vp_ref, vc_ref, vn_ref,
                pq_ref, pp_ref, pc_ref, pn_ref, o_ref, *, nb):
    n = pl.program_id(1)
    pq = pq_ref[0]
    pk = jnp.concatenate([pp_ref[0], pc_ref[0], pn_ref[0]], axis=0)
    rel = pk - pq
    row = lax.broadcasted_iota(jnp.int32, rel.shape, 0)
    lo = jnp.where(n > 0, 0, BLOCK)
    hi = jnp.where(n < nb - 1, 3 * BLOCK, 2 * BLOCK)
    na = jnp.abs(rel)
    mask = (na <= WINDOW) & (row >= lo) & (row < hi)
    large = jnp.full(na.shape, NUM_BUCKETS // 4, jnp.int32)
    for t in _T5_STEPS:
        large = large + (na >= t).astype(jnp.int32)
    large = jnp.minimum(large, NUM_BUCKETS // 2 - 1)
    bkt = jnp.where(rel > 0, NUM_BUCKETS // 2, 0) + jnp.where(na < NUM_BUCKETS // 4, na, large)
    bkts = [bkt[j * BLOCK:(j + 1) * BLOCK] for j in range(3)]
    mask_g = jnp.concatenate([mask] * SWA_GROUP, axis=1)

    def transpose_bf16(a):
        return a.astype(F32).T.astype(BF16)

    q = q_ref[0]
    for g in range(SWA_KV_HEADS):
        lo, hi = g * HEAD_DIM, (g + 1) * HEAD_DIM
        k3 = jnp.concatenate([kp_ref[0, :, lo:hi], kc_ref[0, :, lo:hi], kn_ref[0, :, lo:hi]], axis=0)
        v3t = jnp.concatenate([transpose_bf16(r[0, :, lo:hi]) for r in (vp_ref, vc_ref, vn_ref)], axis=1)
        heads = range(g * SWA_GROUP, (g + 1) * SWA_GROUP)
        qt = jnp.concatenate([transpose_bf16(q[:, hd * HEAD_DIM:(hd + 1) * HEAD_DIM]) for hd in heads], axis=1)
        biases = []
        for hd in heads:
            tab = jnp.broadcast_to(t5_ref[hd:hd + 1, :], (BLOCK, LANES))
            biases.append(jnp.concatenate([jnp.take_along_axis(tab, b, axis=1) for b in bkts], axis=0))
        s = jnp.dot(k3, qt, preferred_element_type=F32) + jnp.concatenate(biases, axis=1)
        s = jnp.where(mask_g, s, NEG)
        sk = jnp.concatenate([jnp.full((1, BLOCK), sink_ref[0, hd], F32) for hd in heads], axis=1)
        m = jnp.maximum(jnp.max(s, axis=0, keepdims=True), sk)
        e = jnp.exp(s - m)
        denom = jnp.sum(e, axis=0, keepdims=True) + jnp.exp(sk - m)
        ot = jnp.dot(v3t, e.astype(BF16), preferred_element_type=F32) / denom
        for j, hd in enumerate(heads):
            o_ref[0, :, hd * HEAD_DIM:(hd + 1) * HEAD_DIM] = ot[:, j * BLOCK:(j + 1) * BLOCK].T


def _swa_attn(t5, sink, sq, sk, sv, pos_col, pos_row):
    B, S, _ = sq.shape
    nb = S // BLOCK
    kvw = SWA_KV_HEADS * HEAD_DIM
    prev = lambda b, n: (b, jnp.maximum(n - 1, 0), 0)
    cur = lambda b, n: (b, n, 0)
    nxt = lambda b, n: (b, jnp.minimum(n + 1, nb - 1), 0)
    rcur = lambda b, n: (b, 0, n)
    smem = pl.BlockSpec(memory_space=pltpu.SMEM)
    return pl.pallas_call(
        functools.partial(_swa_kernel, nb=nb),
        grid=(B, nb),
        in_specs=[
            _const_spec(t5.shape), smem,
            pl.BlockSpec((1, BLOCK, SWA_HEADS * HEAD_DIM), cur),
            pl.BlockSpec((1, BLOCK, kvw), prev), pl.BlockSpec((1, BLOCK, kvw), cur), pl.BlockSpec((1, BLOCK, kvw), nxt),
            pl.BlockSpec((1, BLOCK, kvw), prev), pl.BlockSpec((1, BLOCK, kvw), cur), pl.BlockSpec((1, BLOCK, kvw), nxt),
            pl.BlockSpec((1, 1, BLOCK), rcur),
            pl.BlockSpec((1, BLOCK, 1), prev), pl.BlockSpec((1, BLOCK, 1), cur), pl.BlockSpec((1, BLOCK, 1), nxt),
        ],
        out_specs=pl.BlockSpec((1, BLOCK, SWA_HEADS * HEAD_DIM), cur),
        out_shape=jax.ShapeDtypeStruct((B, S, SWA_HEADS * HEAD_DIM), F32),
        compiler_params=_params(("parallel", "parallel")), name="swa_attn",
    )(t5, sink, sq, sk, sk, sk, sv, sv, sv, pos_row, pos_col, pos_col, pos_col)


def _out_proj_kernel(x_ref, mla_ref, swa_ref, g1_ref, g2_ref, wo_ref, gf_ref, xo_ref, h2_ref):
    w = wo_ref.shape[0] // 2
    m1 = _rms(mla_ref[...], g1_ref[...]).astype(BF16)
    m2 = _rms(swa_ref[...], g2_ref[...]).astype(BF16)
    y = jnp.dot(m1, wo_ref[0:w, :], preferred_element_type=F32)
    y = y + jnp.dot(m2, wo_ref[w:2 * w, :], preferred_element_type=F32)
    xn = x_ref[...] + y
    xo_ref[...] = xn
    h2_ref[...] = _rms(xn, gf_ref[...]).astype(BF16)


def _out_proj(x, mla_o, swa_o, g1, g2, wo, gf, *, tm):
    T, D = x.shape
    W = mla_o.shape[1]
    row = lambda i: (i, 0)
    return pl.pallas_call(
        _out_proj_kernel,
        grid=(T // tm,),
        in_specs=[
            pl.BlockSpec((tm, D), row), pl.BlockSpec((tm, W), row), pl.BlockSpec((tm, W), row),
            _const_spec(g1.shape), _const_spec(g2.shape), _const_spec(wo.shape), _const_spec(gf.shape),
        ],
        out_specs=(pl.BlockSpec((tm, D), row), pl.BlockSpec((tm, D), row)),
        out_shape=(jax.ShapeDtypeStruct((T, D), F32), jax.ShapeDtypeStruct((T, D), BF16)),
        compiler_params=_params(("parallel",)), name="out_proj",
    )(x, mla_o, swa_o, g1, g2, wo, gf)


def _ffn_kernel(hp_ref, h_ref, hn_ref, wg_ref, wu_ref, cw_ref, cb_ref, wd_ref, y_ref, hext_ref, *,
                tm, tiles_per_seq):
    i = pl.program_id(0)
    j = pl.program_id(1)
    halo = BF16_SUBLANES

    @pl.when(j == 0)
    def _():
        t = i % tiles_per_seq
        hext_ref[0:halo, :] = jnp.where(t == 0, jnp.zeros_like(hp_ref[...]), hp_ref[...])
        hext_ref[halo:halo + tm, :] = h_ref[...]
        hext_ref[halo + tm:, :] = jnp.where(t == tiles_per_seq - 1, jnp.zeros_like(hn_ref[...]), hn_ref[...])
        y_ref[...] = jnp.zeros_like(y_ref)

    gp = jnp.dot(hext_ref[...], wg_ref[...], preferred_element_type=F32)
    cw = cw_ref[...]
    g = (gp[halo - 1:halo - 1 + tm] * cw[0:1] + gp[halo:halo + tm] * cw[1:2]
         + gp[halo + 1:halo + 1 + tm] * cw[2:3] + cb_ref[...])
    u = jnp.dot(h_ref[...], wu_ref[...], preferred_element_type=F32)
    a = (g * _sigmoid(g) * u).astype(BF16)
    y_ref[...] += jnp.dot(a, wd_ref[...], preferred_element_type=F32)


def _ffn(h2, wg, wu, cw, cb, wd, *, layer, tm, tf, seq):
    T, D = h2.shape
    Fp = wg.shape[2]
    halo = BF16_SUBLANES
    hb = tm // halo
    nhb = T // halo
    return pl.pallas_call(
        functools.partial(_ffn_kernel, tm=tm, tiles_per_seq=seq // tm),
        grid=(T // tm, Fp // tf),
        in_specs=[
            pl.BlockSpec((halo, D), lambda i, j: (jnp.maximum(i * hb - 1, 0), 0)),
            pl.BlockSpec((tm, D), lambda i, j: (i, 0)),
            pl.BlockSpec((halo, D), lambda i, j: (jnp.minimum((i + 1) * hb, nhb - 1), 0)),
            pl.BlockSpec((None, D, tf), lambda i, j: (layer, 0, j)),
            pl.BlockSpec((None, D, tf), lambda i, j: (layer, 0, j)),
            pl.BlockSpec((None, 3, tf), lambda i, j: (layer, 0, j)),
            pl.BlockSpec((None, 1, tf), lambda i, j: (layer, 0, j)),
            pl.BlockSpec((None, tf, D), lambda i, j: (layer, j, 0)),
        ],
        out_specs=pl.BlockSpec((tm, D), lambda i, j: (i, 0)),
        out_shape=jax.ShapeDtypeStruct((T, D), F32),
        scratch_shapes=[pltpu.VMEM((tm + 2 * halo, D), BF16)],
        compiler_params=_params(("parallel", "arbitrary")), name="ffn",
    )(h2, h2, h2, wg, wu, cw, cb, wd)


def _ple_kernel(x_ref, y_ref, p_ref, wpg_ref, bpg_ref, wpp_ref, gfin_ref, o_ref, *, final):
    x2 = x_ref[...] + y_ref[...]
    gate = _sigmoid(jnp.dot(x2.astype(BF16), wpg_ref[...], preferred_element_type=F32) + bpg_ref[...])
    pp = jnp.dot(p_ref[...].astype(BF16), wpp_ref[...], preferred_element_type=F32)
    x3 = x2 + gate * pp
    if final:
        x3 = _rms(x3, gfin_ref[...])
    o_ref[...] = x3


def _ple(x, y, p, wpg, bpg, wpp, gfin, *, tm, final):
    T, D = x.shape
    P = p.shape[1]
    row = lambda i: (i, 0)
    return pl.pallas_call(
        functools.partial(_ple_kernel, final=final),
        grid=(T // tm,),
        in_specs=[
            pl.BlockSpec((tm, D), row), pl.BlockSpec((tm, D), row), pl.BlockSpec((tm, P), row),
            _const_spec(wpg.shape), _const_spec(bpg.shape), _const_spec(wpp.shape), _const_spec(gfin.shape),
        ],
        out_specs=pl.BlockSpec((tm, D), row),
        out_shape=jax.ShapeDtypeStruct((T, D), F32),
        compiler_params=_params(("parallel",)), name="ple",
    )(x, y, p, wpg, bpg, wpp, gfin)


def _prep_w_in(w_in):
    half = MLA_ROPE // 2
    kr0 = Q_RANK + KV_RANK
    t1, t2 = w_in[:, kr0:kr0 + half], w_in[:, kr0 + half:kr0 + MLA_ROPE]
    zpad = jnp.zeros((w_in.shape[0], LANES - MLA_ROPE), w_in.dtype)
    return jnp.concatenate(
        [w_in[:, :kr0], t1, t2, zpad, t2, t1, zpad, w_in[:, kr0 + MLA_ROPE:]], axis=1).astype(BF16)


def _prep_w_uq(w_uq):
    half = MLA_ROPE // 2
    w = w_uq.reshape(Q_RANK, MLA_HEADS, MLA_NOPE + MLA_ROPE)
    nope, t1, t2 = w[..., :MLA_NOPE], w[..., MLA_NOPE:MLA_NOPE + half], w[..., MLA_NOPE + half:]
    zpad = jnp.zeros((Q_RANK, MLA_HEADS, LANES - MLA_ROPE), w_uq.dtype)
    cols = jnp.concatenate([nope, t1, t2, zpad, t2, t1, zpad], axis=-1)
    return cols.reshape(Q_RANK, MLA_HEADS * 3 * LANES).T.astype(BF16)


def _prep_w_ukv(w_ukv):
    w = w_ukv.reshape(KV_RANK, MLA_HEADS, MLA_NOPE + MLA_V)
    wkn = w[..., :MLA_NOPE].reshape(KV_RANK, MLA_HEADS * MLA_NOPE).astype(BF16)
    wvt = w[..., MLA_NOPE:].reshape(KV_RANK, MLA_HEADS * MLA_V).T.astype(BF16)
    return wkn, wvt


def _pad_ff(w, axis, fp):
    pad = [(0, 0)] * w.ndim
    pad[axis] = (0, fp - w.shape[axis])
    return jnp.pad(w, pad)


def _pick_tile(n, pref):
    t = min(pref, n)
    while n % t:
        t //= 2
    return t


def kernel(x, p, positions, attn_norm, w_in, cq_norm, ckv_norm, w_uq, w_ukv, swa_sink, t5_bias,
           mla_out_norm, swa_out_norm, w_o, ffn_norm, w_gate, w_up, conv_w, conv_b, w_down,
           ple_gate_w, ple_gate_b, ple_proj, final_norm):
    B, S, D = x.shape
    depth = w_in.shape[0]
    T = B * S
    d_ff = w_gate.shape[-1]
    tf = 512
    fp = -(-d_ff // tf) * tf

    tm_in = _pick_tile(S, 512)
    tq = _pick_tile(S, 512)
    tk = _pick_tile(S, 512)
    tm_out = _pick_tile(S, 512)
    tm_ffn = _pick_tile(S, 1024)
    tm_ple = _pick_tile(S, 512)

    pos_row = positions.reshape(B, 1, S)
    pos_col = positions.reshape(B, S, 1)
    invf = (ROPE_THETA ** (-jnp.arange(0, MLA_ROPE, 2, dtype=F32) / MLA_ROPE)).reshape(MLA_ROPE // 2, 1)
    row = lambda v: v.reshape(1, -1)
    t5_tab = jnp.pad(t5_bias.T, ((0, 0), (0, LANES - NUM_BUCKETS)))
    wg_all = _pad_ff(w_gate.astype(BF16), 2, fp)
    wu_all = _pad_ff(w_up.astype(BF16), 2, fp)
    cw_all = _pad_ff(conv_w, 2, fp)
    cb_all = _pad_ff(conv_b[:, None, :], 2, fp)
    wd_all = _pad_ff(w_down.astype(BF16), 1, fp)

    xf = x
    for i in range(depth):
        wkn, wvt = _prep_w_ukv(w_ukv[i])
        qt, k, vt, sq, sk, sv = _in_proj(
            xf.reshape(B, S, D), pos_row, invf, row(attn_norm[i]), _prep_w_in(w_in[i]),
            row(cq_norm[i]), row(ckv_norm[i]), _prep_w_uq(w_uq[i]), wkn, wvt, tm=tm_in)
        mla_o = _mla_attn(qt, k, vt, tq=tq, tk=tk, streams=2 if S % (2 * tq) == 0 else 1)
        swa_o = _swa_attn(t5_tab, row(swa_sink[i]), sq, sk, sv, pos_col, pos_row)
        x1, h2 = _out_proj(
            xf.reshape(T, D), mla_o.reshape(T, -1), swa_o.reshape(T, -1), row(mla_out_norm[i]),
            row(swa_out_norm[i]), w_o[i].astype(BF16), row(ffn_norm[i]), tm=tm_out)
        y = _ffn(h2, wg_all, wu_all, cw_all, cb_all, wd_all, layer=i, tm=tm_ffn, tf=tf, seq=S)
        xf = _ple(
            x1, y, p[i].reshape(T, -1), ple_gate_w[i].astype(BF16), row(ple_gate_b[i]),
            ple_proj[i].astype(BF16), row(final_norm), tm=tm_ple, final=(i == depth - 1))
    return xf.reshape(B, S, D)
```

```python
import functools
import math

import jax
import jax.numpy as jnp
from jax import lax
from jax.experimental import pallas as pl
from jax.experimental.pallas import tpu as pltpu

F32 = jnp.float32
BF16 = jnp.bfloat16

EPS = 1e-6
BLOCK = 128
MLA_HEADS = 8
MLA_NOPE = 128
MLA_ROPE = 64
MLA_V = 128
MLA_QK_PAD = 256
Q_RANK = 384
KV_RANK = 256
ROPE_THETA = 10000.0
SWA_HEADS = 8
SWA_KV_HEADS = 2
SWA_GROUP = SWA_HEADS // SWA_KV_HEADS
HEAD_DIM = 128
WINDOW = 128
NUM_BUCKETS = 32
NEG = -1e30
LOG2E = math.log2(math.e)

V7X_VMEM_BYTES = 64 * 1024 * 1024
VMEM_LIMIT_BYTES = V7X_VMEM_BYTES - 8 * 1024 * 1024
LANES = 128
BF16_SUBLANES = 16
MLA_V_EXT = MLA_V + BF16_SUBLANES

_Z_CQ = 0
_Z_CKV = _Z_CQ + Q_RANK
_Z_KA = _Z_CKV + KV_RANK
_Z_KB = _Z_KA + LANES
_Z_SQ = _Z_KB + LANES
_Z_SK = _Z_SQ + SWA_HEADS * HEAD_DIM
_Z_SV = _Z_SK + SWA_KV_HEADS * HEAD_DIM
_Z_END = _Z_SV + SWA_KV_HEADS * HEAD_DIM


def _params(sem):
    return pltpu.CompilerParams(dimension_semantics=sem, vmem_limit_bytes=VMEM_LIMIT_BYTES)


def _const_spec(shape):
    nd = len(shape)
    return pl.BlockSpec(shape, lambda *_: (0,) * nd, pipeline_mode=pl.Buffered(1))


def _rms(x, g):
    ms = jnp.mean(x * x, axis=-1, keepdims=True)
    return x * lax.rsqrt(ms + EPS) * g


def _sigmoid(x):
    return 1.0 / (1.0 + jnp.exp(-x))


def _in_proj_kernel(x_ref, pos_ref, invf_ref, g_ref, win_ref, cqn_ref, ckvn_ref, wq_ref, wkn_ref,
                    wvt_ref, qt_ref, k_ref, vt_ref, sq_ref, sk_ref, sv_ref, *, q_scale, swa_scale):
    x = x_ref[0]
    h = _rms(x, g_ref[...]).astype(BF16)
    z = jnp.dot(h, win_ref[...], preferred_element_type=F32)
    cq = _rms(z[:, _Z_CQ:_Z_CKV], cqn_ref[...]).astype(BF16)
    ckv = _rms(z[:, _Z_CKV:_Z_KA], ckvn_ref[...]).astype(BF16)

    ang = invf_ref[...] * pos_ref[0].astype(F32)
    cos_t, sin_t = jnp.cos(ang), jnp.sin(ang)
    zpad = jnp.zeros((LANES - MLA_ROPE, ang.shape[1]), F32)
    cc_t = jnp.concatenate([cos_t, cos_t, zpad], axis=0)
    ss_t = jnp.concatenate([-sin_t, sin_t, zpad], axis=0)

    r = lax.dot_general(wq_ref[...], cq, (((1,), (1,)), ((), ())), preferred_element_type=F32)
    for hd in range(MLA_HEADS):
        b = hd * 3 * LANES
        qt_ref[0, hd, 0:LANES, :] = (r[b:b + LANES] * q_scale).astype(BF16)
        roped = r[b + LANES:b + 2 * LANES] * cc_t + r[b + 2 * LANES:b + 3 * LANES] * ss_t
        qt_ref[0, hd, LANES:2 * LANES, :] = (roped * q_scale).astype(BF16)

    knope = jnp.dot(ckv, wkn_ref[...], preferred_element_type=F32)
    cc, ss = cc_t.T, ss_t.T
    krope = (z[:, _Z_KA:_Z_KB] * cc + z[:, _Z_KB:_Z_SQ] * ss).astype(BF16)
    for hd in range(MLA_HEADS):
        k_ref[0, hd, :, 0:LANES] = knope[:, hd * LANES:(hd + 1) * LANES].astype(BF16)
        k_ref[0, hd, :, LANES:MLA_QK_PAD] = krope

    vt = lax.dot_general(wvt_ref[...], ckv, (((1,), (1,)), ((), ())), preferred_element_type=F32)
    ones_row = (lax.broadcasted_iota(jnp.int32, (BF16_SUBLANES, vt.shape[1]), 0) == 0).astype(BF16)
    for hd in range(MLA_HEADS):
        vt_ref[0, hd, 0:MLA_V, :] = vt[hd * MLA_V:(hd + 1) * MLA_V].astype(BF16)
        vt_ref[0, hd, MLA_V:MLA_V_EXT, :] = ones_row

    sq_ref[0] = (z[:, _Z_SQ:_Z_SK] * swa_scale).astype(BF16)
    sk_ref[0] = z[:, _Z_SK:_Z_SV].astype(BF16)
    sv_ref[0] = z[:, _Z_SV:_Z_END].astype(BF16)


def _in_proj(x, pos_row, invf, g, win, cqn, ckvn, wq, wkn, wvt, *, tm):
    B, S, D = x.shape
    grid = (B, S // tm)
    q_scale = LOG2E / math.sqrt(MLA_NOPE + MLA_ROPE)
    swa_scale = 1.0 / math.sqrt(HEAD_DIM)
    out_shape = (
        jax.ShapeDtypeStruct((B, MLA_HEADS, MLA_QK_PAD, S), BF16),
        jax.ShapeDtypeStruct((B, MLA_HEADS, S, MLA_QK_PAD), BF16),
        jax.ShapeDtypeStruct((B, MLA_HEADS, MLA_V_EXT, S), BF16),
        jax.ShapeDtypeStruct((B, S, SWA_HEADS * HEAD_DIM), BF16),
        jax.ShapeDtypeStruct((B, S, SWA_KV_HEADS * HEAD_DIM), BF16),
        jax.ShapeDtypeStruct((B, S, SWA_KV_HEADS * HEAD_DIM), BF16),
    )
    in_specs = [
        pl.BlockSpec((1, tm, D), lambda b, i: (b, i, 0)),
        pl.BlockSpec((1, 1, tm), lambda b, i: (b, 0, i)),
        _const_spec(invf.shape), _const_spec(g.shape), _const_spec(win.shape),
        _const_spec(cqn.shape), _const_spec(ckvn.shape), _const_spec(wq.shape),
        _const_spec(wkn.shape), _const_spec(wvt.shape),
    ]
    out_specs = (
        pl.BlockSpec((1, MLA_HEADS, MLA_QK_PAD, tm), lambda b, i: (b, 0, 0, i)),
        pl.BlockSpec((1, MLA_HEADS, tm, MLA_QK_PAD), lambda b, i: (b, 0, i, 0)),
        pl.BlockSpec((1, MLA_HEADS, MLA_V_EXT, tm), lambda b, i: (b, 0, 0, i)),
        pl.BlockSpec((1, tm, SWA_HEADS * HEAD_DIM), lambda b, i: (b, i, 0)),
        pl.BlockSpec((1, tm, SWA_KV_HEADS * HEAD_DIM), lambda b, i: (b, i, 0)),
        pl.BlockSpec((1, tm, SWA_KV_HEADS * HEAD_DIM), lambda b, i: (b, i, 0)),
    )
    return pl.pallas_call(
        functools.partial(_in_proj_kernel, q_scale=q_scale, swa_scale=swa_scale),
        grid=grid, in_specs=in_specs, out_specs=out_specs, out_shape=out_shape,
        compiler_params=_params(("parallel", "parallel")), name="in_proj",
    )(x, pos_row, invf, g, win, cqn, ckvn, wq, wkn, wvt)


def _mla_kernel(qt_ref, qn_ref, k_ref, vt_ref, o_ref, m_ref, acc_ref, s_ref, mc_ref, *, tq, tk, unroll, streams):
    S = k_ref.shape[2]
    n = S // tk
    trips = n // unroll

    def scores(c, j, slot, from_next=None):
        start = pl.multiple_of(c * tk, tk)
        q = qt_ref[0, 0, :, j * tq:(j + 1) * tq]
        if from_next is not None:
            q = jnp.where(from_next, qn_ref[0, 0, :, j * tq:(j + 1) * tq], q)
        s = jnp.dot(k_ref[0, 0, pl.ds(start, tk), :], q, preferred_element_type=F32)
        s_ref[2 * j + slot] = s
        mc_ref[2 * j + slot] = jnp.max(s, axis=0, keepdims=True)

    def softmax_pv(c, j, slot):
        start = pl.multiple_of(c * tk, tk)
        m_prev = m_ref[j]
        m_new = jnp.maximum(m_prev, mc_ref[2 * j + slot])
        alpha = jnp.exp2(m_prev - m_new)
        p = jnp.exp2(s_ref[2 * j + slot] - m_new).astype(BF16)
        vc = vt_ref[0, 0, :, pl.ds(start, tk)]
        acc_ref[j] = alpha * acc_ref[j] + jnp.dot(vc, p, preferred_element_type=F32)
        m_ref[j] = m_new

    @pl.when(pl.program_id(2) == 0)
    def _():
        for j in range(streams):
            scores(0, j, 0)

    for j in range(streams):
        m_ref[j] = jnp.full(m_ref.shape[1:], NEG, F32)
        acc_ref[j] = jnp.zeros(acc_ref.shape[1:], F32)

    def group(i, carry):
        c0 = i * unroll
        last = i == trips - 1
        for u in range(unroll):
            for j in range(streams):
                if u == unroll - 1:
                    scores(jnp.where(last, 0, c0 + u + 1), j, (u + 1) % 2, from_next=last)
                else:
                    scores(c0 + u + 1, j, (u + 1) % 2)
                softmax_pv(c0 + u, j, u % 2)
        return carry

    lax.fori_loop(0, trips, group, 0)
    for j in range(streams):
        acc = acc_ref[j]
        o_ref[0, j * tq:(j + 1) * tq, :] = (acc[0:MLA_V] / acc[MLA_V:MLA_V + 1]).T


def _mla_attn(qt, k, vt, *, tq, tk, streams):
    B, H, _, S = qt.shape
    vrows = vt.shape[2]
    n = S // tk
    unroll = max(u for u in (2, 4, 8) if n % u == 0 and n // u >= 2)
    tqs = tq * streams
    steps = S // tqs
    return pl.pallas_call(
        functools.partial(_mla_kernel, tq=tq, tk=tk, unroll=unroll, streams=streams),
        grid=(B, H, steps),
        in_specs=[
            pl.BlockSpec((1, 1, MLA_QK_PAD, tqs), lambda b, h, i: (b, h, 0, i)),
            pl.BlockSpec((1, 1, MLA_QK_PAD, tqs), lambda b, h, i: (b, h, 0, jnp.minimum(i + 1, steps - 1))),
            pl.BlockSpec((1, 1, S, MLA_QK_PAD), lambda b, h, i: (b, h, 0, 0)),
            pl.BlockSpec((1, 1, vrows, S), lambda b, h, i: (b, h, 0, 0)),
        ],
        out_specs=pl.BlockSpec((1, tqs, MLA_V), lambda b, h, i: (b, i, h)),
        out_shape=jax.ShapeDtypeStruct((B, S, H * MLA_V), F32),
        scratch_shapes=[pltpu.VMEM((streams, 1, tq), F32), pltpu.VMEM((streams, vrows, tq), F32),
                        pltpu.VMEM((2 * streams, tk, tq), F32), pltpu.VMEM((2 * streams, 1, tq), F32)],
        compiler_params=_params(("arbitrary", "arbitrary", "arbitrary")), name="mla_attn",
    )(qt, qt, k, vt)


_T5_STEPS = (12, 16, 23, 32, 46, 64, 91, 128)


def _swa_kernel(t5_ref, sink_ref, q_ref, kp_ref, kc_ref, kn_ref, vp_ref, vc_ref, vn_ref,
                pq_ref, pp_ref, pc_ref, pn_ref, o_ref, *, nb):
    n = pl.program_id(1)
    pq = pq_ref[0]
    pk = jnp.concatenate([pp_ref[0], pc_ref[0], pn_ref[0]], axis=0)
    rel = pk - pq
    row = lax.broadcasted_iota(jnp.int32, rel.shape, 0)
    lo = jnp.where(n > 0, 0, BLOCK)
    hi = jnp.where(n < nb - 1, 3 * BLOCK, 2 * BLOCK)
    na = jnp.abs(rel)
    mask = (na <= WINDOW) & (row >= lo) & (row < hi)
    large = jnp.full(na.shape, NUM_BUCKETS // 4, jnp.int32)
    for t in _T5_STEPS:
        large = large + (na >= t).astype(jnp.int32)
    large = jnp.minimum(large, NUM_BUCKETS // 2 - 1)
    bkt = jnp.where(rel > 0, NUM_BUCKETS // 2, 0) + jnp.where(na < NUM_BUCKETS // 4, na, large)
    bkt = jnp.where(mask, bkt, NUM_BUCKETS)
    bkts = [bkt[j * BLOCK:(j + 1) * BLOCK] for j in range(3)]

    def transpose_bf16(a):
        return a.astype(F32).T.astype(BF16)

    q = q_ref[0]
    for g in range(SWA_KV_HEADS):
        lo, hi = g * HEAD_DIM, (g + 1) * HEAD_DIM
        k3 = jnp.concatenate([kp_ref[0, :, lo:hi], kc_ref[0, :, lo:hi], kn_ref[0, :, lo:hi]], axis=0)
        v3t = jnp.concatenate([transpose_bf16(r[0, :, lo:hi]) for r in (vp_ref, vc_ref, vn_ref)], axis=1)
        heads = range(g * SWA_GROUP, (g + 1) * SWA_GROUP)
        qt = jnp.concatenate([transpose_bf16(q[:, hd * HEAD_DIM:(hd + 1) * HEAD_DIM]) for hd in heads], axis=1)
        biases = []
        for hd in heads:
            tab = jnp.broadcast_to(t5_ref[hd:hd + 1, :], (BLOCK, LANES))
            biases.append(jnp.concatenate([jnp.take_along_axis(tab, b, axis=1) for b in bkts], axis=0))
        s = jnp.dot(k3, qt, preferred_element_type=F32) + jnp.concatenate(biases, axis=1)
        sk = jnp.concatenate([jnp.full((1, BLOCK), sink_ref[0, hd], F32) for hd in heads], axis=1)
        m = jnp.maximum(jnp.max(s, axis=0, keepdims=True), sk)
        e = jnp.exp(s - m)
        denom = jnp.sum(e, axis=0, keepdims=True) + jnp.exp(sk - m)
        ot = jnp.dot(v3t, e.astype(BF16), preferred_element_type=F32) / denom
        for j, hd in enumerate(heads):
            o_ref[0, :, hd * HEAD_DIM:(hd + 1) * HEAD_DIM] = ot[:, j * BLOCK:(j + 1) * BLOCK].T


def _swa_attn(t5, sink, sq, sk, sv, pos_col, pos_row):
    B, S, _ = sq.shape
    nb = S // BLOCK
    kvw = SWA_KV_HEADS * HEAD_DIM
    prev = lambda b, n: (b, jnp.maximum(n - 1, 0), 0)
    cur = lambda b, n: (b, n, 0)
    nxt = lambda b, n: (b, jnp.minimum(n + 1, nb - 1), 0)
    rcur = lambda b, n: (b, 0, n)
    smem = pl.BlockSpec(memory_space=pltpu.SMEM)
    return pl.pallas_call(
        functools.partial(_swa_kernel, nb=nb),
        grid=(B, nb),
        in_specs=[
            _const_spec(t5.shape), smem,
            pl.BlockSpec((1, BLOCK, SWA_HEADS * HEAD_DIM), cur),
            pl.BlockSpec((1, BLOCK, kvw), prev), pl.BlockSpec((1, BLOCK, kvw), cur), pl.BlockSpec((1, BLOCK, kvw), nxt),
            pl.BlockSpec((1, BLOCK, kvw), prev), pl.BlockSpec((1, BLOCK, kvw), cur), pl.BlockSpec((1, BLOCK, kvw), nxt),
            pl.BlockSpec((1, 1, BLOCK), rcur),
            pl.BlockSpec((1, BLOCK, 1), prev), pl.BlockSpec((1, BLOCK, 1), cur), pl.BlockSpec((1, BLOCK, 1), nxt),
        ],
        out_specs=pl.BlockSpec((1, BLOCK, SWA_HEADS * HEAD_DIM), cur),
        out_shape=jax.ShapeDtypeStruct((B, S, SWA_HEADS * HEAD_DIM), F32),
        compiler_params=_params(("parallel", "parallel")), name="swa_attn",
    )(t5, sink, sq, sk, sk, sk, sv, sv, sv, pos_row, pos_col, pos_col, pos_col)


def _out_proj_kernel(x_ref, mla_ref, swa_ref, g1_ref, g2_ref, wo_ref, gf_ref, xo_ref, h2_ref):
    w = wo_ref.shape[0] // 2
    m1 = _rms(mla_ref[...], g1_ref[...]).astype(BF16)
    m2 = _rms(swa_ref[...], g2_ref[...]).astype(BF16)
    y = jnp.dot(m1, wo_ref[0:w, :], preferred_element_type=F32)
    y = y + jnp.dot(m2, wo_ref[w:2 * w, :], preferred_element_type=F32)
    xn = x_ref[...] + y
    xo_ref[...] = xn
    h2_ref[...] = _rms(xn, gf_ref[...]).astype(BF16)


def _out_proj(x, mla_o, swa_o, g1, g2, wo, gf, *, tm):
    T, D = x.shape
    W = mla_o.shape[1]
    row = lambda i: (i, 0)
    return pl.pallas_call(
        _out_proj_kernel,
        grid=(T // tm,),
        in_specs=[
            pl.BlockSpec((tm, D), row), pl.BlockSpec((tm, W), row), pl.BlockSpec((tm, W), row),
            _const_spec(g1.shape), _const_spec(g2.shape), _const_spec(wo.shape), _const_spec(gf.shape),
        ],
        out_specs=(pl.BlockSpec((tm, D), row), pl.BlockSpec((tm, D), row)),
        out_shape=(jax.ShapeDtypeStruct((T, D), F32), jax.ShapeDtypeStruct((T, D), BF16)),
        compiler_params=_params(("parallel",)), name="out_proj",
    )(x, mla_o, swa_o, g1, g2, wo, gf)


def _ffn_kernel(hp_ref, h_ref, hn_ref, wg_ref, wu_ref, cw_ref, cb_ref, wd_ref, y_ref, hext_ref, *,
                tm, tiles_per_seq):
    i = pl.program_id(0)
    j = pl.program_id(1)
    halo = BF16_SUBLANES

    @pl.when(j == 0)
    def _():
        t = i % tiles_per_seq
        hext_ref[0:halo, :] = jnp.where(t == 0, jnp.zeros_like(hp_ref[...]), hp_ref[...])
        hext_ref[halo:halo + tm, :] = h_ref[...]
        hext_ref[halo + tm:, :] = jnp.where(t == tiles_per_seq - 1, jnp.zeros_like(hn_ref[...]), hn_ref[...])
        y_ref[...] = jnp.zeros_like(y_ref)

    gp = jnp.dot(hext_ref[...], wg_ref[...], preferred_element_type=F32)
    cw = cw_ref[...]
    g = (gp[halo - 1:halo - 1 + tm] * cw[0:1] + gp[halo:halo + tm] * cw[1:2]
         + gp[halo + 1:halo + 1 + tm] * cw[2:3] + cb_ref[...])
    u = jnp.dot(h_ref[...], wu_ref[...], preferred_element_type=F32)
    a = (g * _sigmoid(g) * u).astype(BF16)
    y_ref[...] += jnp.dot(a, wd_ref[...], preferred_element_type=F32)


def _ffn(h2, wg, wu, cw, cb, wd, *, layer, tm, tf, seq):
    T, D = h2.shape
    Fp = wg.shape[2]
    halo = BF16_SUBLANES
    hb = tm // halo
    nhb = T // halo
    return pl.pallas_call(
        functools.partial(_ffn_kernel, tm=tm, tiles_per_seq=seq // tm),
        grid=(T // tm, Fp // tf),
        in_specs=[
            pl.BlockSpec((halo, D), lambda i, j: (jnp.maximum(i * hb - 1, 0), 0)),
            pl.BlockSpec((tm, D), lambda i, j: (i, 0)),
            pl.BlockSpec((halo, D), lambda i, j: (jnp.minimum((i + 1) * hb, nhb - 1), 0)),
            pl.BlockSpec((None, D, tf), lambda i, j: (layer, 0, j)),
            pl.BlockSpec((None, D, tf), lambda i, j: (layer, 0, j)),
            pl.BlockSpec((None, 3, tf), lambda i, j: (layer, 0, j)),
            pl.BlockSpec((None, 1, tf), lambda i, j: (layer, 0, j)),
            pl.BlockSpec((None, tf, D), lambda i, j: (layer, j, 0)),
        ],
        out_specs=pl.BlockSpec((tm, D), lambda i, j: (i, 0)),
        out_shape=jax.ShapeDtypeStruct((T, D), F32),
        scratch_shapes=[pltpu.VMEM((tm + 2 * halo, D), BF16)],
        compiler_params=_params(("parallel", "arbitrary")), name="ffn",
    )(h2, h2, h2, wg, wu, cw, cb, wd)


def _ple_kernel(x_ref, y_ref, p_ref, wpg_ref, bpg_ref, wpp_ref, gfin_ref, o_ref, *, final):
    x2 = x_ref[...] + y_ref[...]
    gate = _sigmoid(jnp.dot(x2.astype(BF16), wpg_ref[...], preferred_element_type=F32) + bpg_ref[...])
    pp = jnp.dot(p_ref[...].astype(BF16), wpp_ref[...], preferred_element_type=F32)
    x3 = x2 + gate * pp
    if final:
        x3 = _rms(x3, gfin_ref[...])
    o_ref[...] = x3


def _ple(x, y, p, wpg, bpg, wpp, gfin, *, tm, final):
    T, D = x.shape
    P = p.shape[1]
    row = lambda i: (i, 0)
    return pl.pallas_call(
        functools.partial(_ple_kernel, final=final),
        grid=(T // tm,),
        in_specs=[
            pl.BlockSpec((tm, D), row), pl.BlockSpec((tm, D), row), pl.BlockSpec((tm, P), row),
            _const_spec(wpg.shape), _const_spec(bpg.shape), _const_spec(wpp.shape), _const_spec(gfin.shape),
        ],
        out_specs=pl.BlockSpec((tm, D), row),
        out_shape=jax.ShapeDtypeStruct((T, D), F32),
        compiler_params=_params(("parallel",)), name="ple",
    )(x, y, p, wpg, bpg, wpp, gfin)


def _prep_w_in(w_in):
    half = MLA_ROPE // 2
    kr0 = Q_RANK + KV_RANK
    t1, t2 = w_in[:, kr0:kr0 + half], w_in[:, kr0 + half:kr0 + MLA_ROPE]
    zpad = jnp.zeros((w_in.shape[0], LANES - MLA_ROPE), w_in.dtype)
    return jnp.concatenate(
        [w_in[:, :kr0], t1, t2, zpad, t2, t1, zpad, w_in[:, kr0 + MLA_ROPE:]], axis=1).astype(BF16)


def _prep_w_uq(w_uq):
    half = MLA_ROPE // 2
    w = w_uq.reshape(Q_RANK, MLA_HEADS, MLA_NOPE + MLA_ROPE)
    nope, t1, t2 = w[..., :MLA_NOPE], w[..., MLA_NOPE:MLA_NOPE + half], w[..., MLA_NOPE + half:]
    zpad = jnp.zeros((Q_RANK, MLA_HEADS, LANES - MLA_ROPE), w_uq.dtype)
    cols = jnp.concatenate([nope, t1, t2, zpad, t2, t1, zpad], axis=-1)
    return cols.reshape(Q_RANK, MLA_HEADS * 3 * LANES).T.astype(BF16)


def _prep_w_ukv(w_ukv):
    w = w_ukv.reshape(KV_RANK, MLA_HEADS, MLA_NOPE + MLA_V)
    wkn = w[..., :MLA_NOPE].reshape(KV_RANK, MLA_HEADS * MLA_NOPE).astype(BF16)
    wvt = w[..., MLA_NOPE:].reshape(KV_RANK, MLA_HEADS * MLA_V).T.astype(BF16)
    return wkn, wvt


def _pad_ff(w, axis, fp):
    pad = [(0, 0)] * w.ndim
    pad[axis] = (0, fp - w.shape[axis])
    return jnp.pad(w, pad)


def _pick_tile(n, pref):
    t = min(pref, n)
    while n % t:
        t //= 2
    return t


def kernel(x, p, positions, attn_norm, w_in, cq_norm, ckv_norm, w_uq, w_ukv, swa_sink, t5_bias,
           mla_out_norm, swa_out_norm, w_o, ffn_norm, w_gate, w_up, conv_w, conv_b, w_down,
           ple_gate_w, ple_gate_b, ple_proj, final_norm):
    B, S, D = x.shape
    depth = w_in.shape[0]
    T = B * S
    d_ff = w_gate.shape[-1]
    tf = 512
    fp = -(-d_ff // tf) * tf

    tm_in = _pick_tile(S, 512)
    tq = _pick_tile(S, 256)
    tk = _pick_tile(S, 1024)
    mla_streams = max(s for s in (4, 2, 1) if S % (s * tq) == 0)
    tm_out = _pick_tile(S, 512)
    tm_ffn = _pick_tile(S, 1024)
    tm_ple = _pick_tile(S, 512)

    pos_row = positions.reshape(B, 1, S)
    pos_col = positions.reshape(B, S, 1)
    invf = (ROPE_THETA ** (-jnp.arange(0, MLA_ROPE, 2, dtype=F32) / MLA_ROPE)).reshape(MLA_ROPE // 2, 1)
    row = lambda v: v.reshape(1, -1)
    t5_tab = jnp.pad(t5_bias.T, ((0, 0), (0, LANES - NUM_BUCKETS))).at[:, NUM_BUCKETS].set(NEG)
    wg_all = jnp.zeros((depth, D, fp), BF16).at[:, :, :d_ff].set(w_gate.astype(BF16))
    wu_all = jnp.zeros((depth, D, fp), BF16).at[:, :, :d_ff].set(w_up.astype(BF16))
    cw_all = _pad_ff(conv_w, 2, fp)
    cb_all = _pad_ff(conv_b[:, None, :], 2, fp)
    wd_all = jnp.zeros((depth, fp, D), BF16).at[:, :d_ff, :].set(w_down.astype(BF16))

    xf = x
    for i in range(depth):
        wkn, wvt = _prep_w_ukv(w_ukv[i])
        qt, k, vt, sq, sk, sv = _in_proj(
            xf.reshape(B, S, D), pos_row, invf, row(attn_norm[i]), _prep_w_in(w_in[i]),
            row(cq_norm[i]), row(ckv_norm[i]), _prep_w_uq(w_uq[i]), wkn, wvt, tm=tm_in)
        mla_o = _mla_attn(qt, k, vt, tq=tq, tk=tk, streams=mla_streams)
        swa_o = _swa_attn(t5_tab, row(swa_sink[i]), sq, sk, sv, pos_col, pos_row)
        x1, h2 = _out_proj(
            xf.reshape(T, D), mla_o.reshape(T, -1), swa_o.reshape(T, -1), row(mla_out_norm[i]),
            row(swa_out_norm[i]), w_o[i].astype(BF16), row(ffn_norm[i]), tm=tm_out)
        y = _ffn(h2, wg_all, wu_all, cw_all, cb_all, wd_all, layer=i, tm=tm_ffn, tf=tf, seq=S)
        xf = _ple(
            x1, y, p[i].reshape(T, -1), ple_gate_w[i].astype(BF16), row(ple_gate_b[i]),
            ple_proj[i].astype(BF16), row(final_norm), tm=tm_ple, final=(i == depth - 1))
    return xf.reshape(B, S, D)
```

```python
import functools
import math

import jax
import jax.numpy as jnp
from jax import lax
from jax.experimental import pallas as pl
from jax.experimental.pallas import tpu as pltpu

F32 = jnp.float32
BF16 = jnp.bfloat16

EPS = 1e-6
BLOCK = 128
MLA_HEADS = 8
MLA_NOPE = 128
MLA_ROPE = 64
MLA_V = 128
MLA_QK_PAD = 256
Q_RANK = 384
KV_RANK = 256
ROPE_THETA = 10000.0
SWA_HEADS = 8
SWA_KV_HEADS = 2
SWA_GROUP = SWA_HEADS // SWA_KV_HEADS
HEAD_DIM = 128
WINDOW = 128
NUM_BUCKETS = 32
NEG = -1e30
LOG2E = math.log2(math.e)

V7X_VMEM_BYTES = 64 * 1024 * 1024
VMEM_LIMIT_BYTES = V7X_VMEM_BYTES - 8 * 1024 * 1024
LANES = 128
BF16_SUBLANES = 16
MLA_V_EXT = MLA_V + BF16_SUBLANES

_Z_CQ = 0
_Z_CKV = _Z_CQ + Q_RANK
_Z_KA = _Z_CKV + KV_RANK
_Z_KB = _Z_KA + LANES
_Z_SQ = _Z_KB + LANES
_Z_SK = _Z_SQ + SWA_HEADS * HEAD_DIM
_Z_SV = _Z_SK + SWA_KV_HEADS * HEAD_DIM
_Z_END = _Z_SV + SWA_KV_HEADS * HEAD_DIM


def _params(sem):
    return pltpu.CompilerParams(dimension_semantics=sem, vmem_limit_bytes=VMEM_LIMIT_BYTES)


def _const_spec(shape):
    nd = len(shape)
    return pl.BlockSpec(shape, lambda *_: (0,) * nd, pipeline_mode=pl.Buffered(1))


def _rms(x, g):
    ms = jnp.mean(x * x, axis=-1, keepdims=True)
    return x * lax.rsqrt(ms + EPS) * g


def _sigmoid(x):
    return 1.0 / (1.0 + jnp.exp(-x))


def _in_proj_kernel(x_ref, pos_ref, invf_ref, g_ref, win_ref, cqn_ref, ckvn_ref, wq_ref, wkn_ref,
                    wvt_ref, qt_ref, k_ref, vt_ref, sq_ref, sk_ref, sv_ref, *, q_scale, swa_scale):
    x = x_ref[0]
    h = _rms(x, g_ref[...]).astype(BF16)
    z = jnp.dot(h, win_ref[...], preferred_element_type=F32)
    cq = _rms(z[:, _Z_CQ:_Z_CKV], cqn_ref[...]).astype(BF16)
    ckv = _rms(z[:, _Z_CKV:_Z_KA], ckvn_ref[...]).astype(BF16)

    ang = invf_ref[...] * pos_ref[0].astype(F32)
    cos_t, sin_t = jnp.cos(ang), jnp.sin(ang)
    zpad = jnp.zeros((LANES - MLA_ROPE, ang.shape[1]), F32)
    cc_t = jnp.concatenate([cos_t, cos_t, zpad], axis=0)
    ss_t = jnp.concatenate([-sin_t, sin_t, zpad], axis=0)

    r = lax.dot_general(wq_ref[...], cq, (((1,), (1,)), ((), ())), preferred_element_type=F32)
    for hd in range(MLA_HEADS):
        b = hd * 3 * LANES
        qt_ref[0, hd, 0:LANES, :] = (r[b:b + LANES] * q_scale).astype(BF16)
        roped = r[b + LANES:b + 2 * LANES] * cc_t + r[b + 2 * LANES:b + 3 * LANES] * ss_t
        qt_ref[0, hd, LANES:2 * LANES, :] = (roped * q_scale).astype(BF16)

    knope = jnp.dot(ckv, wkn_ref[...], preferred_element_type=F32)
    cc, ss = cc_t.T, ss_t.T
    krope = (z[:, _Z_KA:_Z_KB] * cc + z[:, _Z_KB:_Z_SQ] * ss).astype(BF16)
    for hd in range(MLA_HEADS):
        k_ref[0, hd, :, 0:LANES] = knope[:, hd * LANES:(hd + 1) * LANES].astype(BF16)
        k_ref[0, hd, :, LANES:MLA_QK_PAD] = krope

    vt = lax.dot_general(wvt_ref[...], ckv, (((1,), (1,)), ((), ())), preferred_element_type=F32)
    ones_row = (lax.broadcasted_iota(jnp.int32, (BF16_SUBLANES, vt.shape[1]), 0) == 0).astype(BF16)
    for hd in range(MLA_HEADS):
        vt_ref[0, hd, 0:MLA_V, :] = vt[hd * MLA_V:(hd + 1) * MLA_V].astype(BF16)
        vt_ref[0, hd, MLA_V:MLA_V_EXT, :] = ones_row

    sq_ref[0] = (z[:, _Z_SQ:_Z_SK] * swa_scale).astype(BF16)
    sk_ref[0] = z[:, _Z_SK:_Z_SV].astype(BF16)
    sv_ref[0] = z[:, _Z_SV:_Z_END].astype(BF16)


def _in_proj(x, pos_row, invf, g, win, cqn, ckvn, wq, wkn, wvt, *, tm):
    B, S, D = x.shape
    grid = (B, S // tm)
    q_scale = LOG2E / math.sqrt(MLA_NOPE + MLA_ROPE)
    swa_scale = 1.0 / math.sqrt(HEAD_DIM)
    out_shape = (
        jax.ShapeDtypeStruct((B, MLA_HEADS, MLA_QK_PAD, S), BF16),
        jax.ShapeDtypeStruct((B, MLA_HEADS, S, MLA_QK_PAD), BF16),
        jax.ShapeDtypeStruct((B, MLA_HEADS, MLA_V_EXT, S), BF16),
        jax.ShapeDtypeStruct((B, S, SWA_HEADS * HEAD_DIM), BF16),
        jax.ShapeDtypeStruct((B, S, SWA_KV_HEADS * HEAD_DIM), BF16),
        jax.ShapeDtypeStruct((B, S, SWA_KV_HEADS * HEAD_DIM), BF16),
    )
    in_specs = [
        pl.BlockSpec((1, tm, D), lambda b, i: (b, i, 0)),
        pl.BlockSpec((1, 1, tm), lambda b, i: (b, 0, i)),
        _const_spec(invf.shape), _const_spec(g.shape), _const_spec(win.shape),
        _const_spec(cqn.shape), _const_spec(ckvn.shape), _const_spec(wq.shape),
        _const_spec(wkn.shape), _const_spec(wvt.shape),
    ]
    out_specs = (
        pl.BlockSpec((1, MLA_HEADS, MLA_QK_PAD, tm), lambda b, i: (b, 0, 0, i)),
        pl.BlockSpec((1, MLA_HEADS, tm, MLA_QK_PAD), lambda b, i: (b, 0, i, 0)),
        pl.BlockSpec((1, MLA_HEADS, MLA_V_EXT, tm), lambda b, i: (b, 0, 0, i)),
        pl.BlockSpec((1, tm, SWA_HEADS * HEAD_DIM), lambda b, i: (b, i, 0)),
        pl.BlockSpec((1, tm, SWA_KV_HEADS * HEAD_DIM), lambda b, i: (b, i, 0)),
        pl.BlockSpec((1, tm, SWA_KV_HEADS * HEAD_DIM), lambda b, i: (b, i, 0)),
    )
    return pl.pallas_call(
        functools.partial(_in_proj_kernel, q_scale=q_scale, swa_scale=swa_scale),
        grid=grid, in_specs=in_specs, out_specs=out_specs, out_shape=out_shape,
        compiler_params=_params(("parallel", "parallel")), name="in_proj",
    )(x, pos_row, invf, g, win, cqn, ckvn, wq, wkn, wvt)


def _mla_kernel(qt_ref, qn_ref, k_ref, vt_ref, o_ref, m_ref, acc_ref, s_ref, mc_ref, *, tq, tk, unroll, streams):
    S = k_ref.shape[2]
    n = S // tk
    trips = n // unroll

    def scores(c, j, slot, from_next=None):
        start = pl.multiple_of(c * tk, tk)
        q = qt_ref[0, 0, :, j * tq:(j + 1) * tq]
        if from_next is not None:
            q = jnp.where(from_next, qn_ref[0, 0, :, j * tq:(j + 1) * tq], q)
        s = jnp.dot(k_ref[0, 0, pl.ds(start, tk), :], q, preferred_element_type=F32)
        s_ref[2 * j + slot] = s
        mc_ref[2 * j + slot] = jnp.max(s, axis=0, keepdims=True)

    def softmax_pv(c, j, slot):
        start = pl.multiple_of(c * tk, tk)
        m_prev = m_ref[j]
        m_new = jnp.maximum(m_prev, mc_ref[2 * j + slot])
        alpha = jnp.exp2(m_prev - m_new)
        p = jnp.exp2(s_ref[2 * j + slot] - m_new).astype(BF16)
        vc = vt_ref[0, 0, :, pl.ds(start, tk)]
        acc_ref[j] = alpha * acc_ref[j] + jnp.dot(vc, p, preferred_element_type=F32)
        m_ref[j] = m_new

    @pl.when(pl.program_id(2) == 0)
    def _():
        for j in range(streams):
            scores(0, j, 0)

    for j in range(streams):
        m_ref[j] = jnp.full(m_ref.shape[1:], NEG, F32)
        acc_ref[j] = jnp.zeros(acc_ref.shape[1:], F32)

    def group(i, carry):
        c0 = i * unroll
        last = i == trips - 1
        for u in range(unroll):
            for j in range(streams):
                if u == unroll - 1:
                    scores(jnp.where(last, 0, c0 + u + 1), j, (u + 1) % 2, from_next=last)
                else:
                    scores(c0 + u + 1, j, (u + 1) % 2)
                softmax_pv(c0 + u, j, u % 2)
        return carry

    lax.fori_loop(0, trips, group, 0)
    for j in range(streams):
        acc = acc_ref[j]
        o_ref[0, j * tq:(j + 1) * tq, :] = (acc[0:MLA_V] / acc[MLA_V:MLA_V + 1]).T


def _mla_attn(qt, k, vt, *, tq, tk, streams):
    B, H, _, S = qt.shape
    vrows = vt.shape[2]
    n = S // tk
    unroll = max(u for u in (2, 4, 8, 16) if n % u == 0 and n // u >= 2)
    tqs = tq * streams
    steps = S // tqs
    return pl.pallas_call(
        functools.partial(_mla_kernel, tq=tq, tk=tk, unroll=unroll, streams=streams),
        grid=(B, H, steps),
        in_specs=[
            pl.BlockSpec((1, 1, MLA_QK_PAD, tqs), lambda b, h, i: (b, h, 0, i)),
            pl.BlockSpec((1, 1, MLA_QK_PAD, tqs), lambda b, h, i: (b, h, 0, jnp.minimum(i + 1, steps - 1))),
            pl.BlockSpec((1, 1, S, MLA_QK_PAD), lambda b, h, i: (b, h, 0, 0)),
            pl.BlockSpec((1, 1, vrows, S), lambda b, h, i: (b, h, 0, 0)),
        ],
        out_specs=pl.BlockSpec((1, tqs, MLA_V), lambda b, h, i: (b, i, h)),
        out_shape=jax.ShapeDtypeStruct((B, S, H * MLA_V), F32),
        scratch_shapes=[pltpu.VMEM((streams, 1, tq), F32), pltpu.VMEM((streams, vrows, tq), F32),
                        pltpu.VMEM((2 * streams, tk, tq), F32), pltpu.VMEM((2 * streams, 1, tq), F32)],
        compiler_params=_params(("arbitrary", "arbitrary", "arbitrary")), name="mla_attn",
    )(qt, qt, k, vt)


_T5_STEPS = (12, 16, 23, 32, 46, 64, 91, 128)


def _swa_kernel(t5_ref, sink_ref, q_ref, kp_ref, kc_ref, kn_ref, vp_ref, vc_ref, vn_ref,
                pq_ref, pp_ref, pc_ref, pn_ref, o_ref, bias_ref, pk_prev_ref, pq_prev_ref, *, nb):
    n = pl.program_id(1)
    pq = pq_ref[0]
    pk = jnp.concatenate([pp_ref[0], pc_ref[0], pn_ref[0]], axis=0)

    @pl.when(n == 0)
    def _():
        pk_prev_ref[...] = pk
        pq_prev_ref[...] = pq

    dk = pk - pk_prev_ref[...]
    dq = pq - pq_prev_ref[...]
    shift = dk[0:1, 0:1]
    moved = jnp.max(jnp.abs(dk - shift)) + jnp.max(jnp.abs(dq - shift))
    rebuild = (n <= 1) | (n == nb - 1) | (moved != 0)
    pk_prev_ref[...] = pk
    pq_prev_ref[...] = pq

    @pl.when(rebuild)
    def _():
        rel = pk - pq
        row = lax.broadcasted_iota(jnp.int32, rel.shape, 0)
        lo = jnp.where(n > 0, 0, BLOCK)
        hi = jnp.where(n < nb - 1, 3 * BLOCK, 2 * BLOCK)
        na = jnp.abs(rel)
        mask = (na <= WINDOW) & (row >= lo) & (row < hi)
        large = jnp.full(na.shape, NUM_BUCKETS // 4, jnp.int32)
        for t in _T5_STEPS:
            large = large + (na >= t).astype(jnp.int32)
        large = jnp.minimum(large, NUM_BUCKETS // 2 - 1)
        bkt = jnp.where(rel > 0, NUM_BUCKETS // 2, 0) + jnp.where(na < NUM_BUCKETS // 4, na, large)
        bkts = [bkt[j * BLOCK:(j + 1) * BLOCK] for j in range(3)]
        for hd in range(SWA_HEADS):
            tab = jnp.broadcast_to(t5_ref[hd:hd + 1, :], (BLOCK, LANES))
            bias = jnp.concatenate([jnp.take_along_axis(tab, b, axis=1) for b in bkts], axis=0)
            g, j = divmod(hd, SWA_GROUP)
            bias_ref[g, :, j * BLOCK:(j + 1) * BLOCK] = jnp.where(mask, bias, NEG)

    def transpose_bf16(a):
        return a.astype(F32).T.astype(BF16)

    q = q_ref[0]
    for g in range(SWA_KV_HEADS):
        lo, hi = g * HEAD_DIM, (g + 1) * HEAD_DIM
        k3 = jnp.concatenate([kp_ref[0, :, lo:hi], kc_ref[0, :, lo:hi], kn_ref[0, :, lo:hi]], axis=0)
        v3t = jnp.concatenate([transpose_bf16(r[0, :, lo:hi]) for r in (vp_ref, vc_ref, vn_ref)], axis=1)
        heads = range(g * SWA_GROUP, (g + 1) * SWA_GROUP)
        qt = jnp.concatenate([transpose_bf16(q[:, hd * HEAD_DIM:(hd + 1) * HEAD_DIM]) for hd in heads], axis=1)
        s = jnp.dot(k3, qt, preferred_element_type=F32) + bias_ref[g]
        sk = jnp.concatenate([jnp.full((1, BLOCK), sink_ref[0, hd], F32) for hd in heads], axis=1)
        m = jnp.maximum(jnp.max(s, axis=0, keepdims=True), sk)
        e = jnp.exp(s - m)
        denom = jnp.sum(e, axis=0, keepdims=True) + jnp.exp(sk - m)
        ot = jnp.dot(v3t, e.astype(BF16), preferred_element_type=F32) / denom
        for j, hd in enumerate(heads):
            o_ref[0, :, hd * HEAD_DIM:(hd + 1) * HEAD_DIM] = ot[:, j * BLOCK:(j + 1) * BLOCK].T


def _swa_attn(t5, sink, sq, sk, sv, pos_col, pos_row):
    B, S, _ = sq.shape
    nb = S // BLOCK
    kvw = SWA_KV_HEADS * HEAD_DIM
    prev = lambda b, n: (b, jnp.maximum(n - 1, 0), 0)
    cur = lambda b, n: (b, n, 0)
    nxt = lambda b, n: (b, jnp.minimum(n + 1, nb - 1), 0)
    rcur = lambda b, n: (b, 0, n)
    smem = pl.BlockSpec(memory_space=pltpu.SMEM)
    return pl.pallas_call(
        functools.partial(_swa_kernel, nb=nb),
        grid=(B, nb),
        in_specs=[
            _const_spec(t5.shape), smem,
            pl.BlockSpec((1, BLOCK, SWA_HEADS * HEAD_DIM), cur),
            pl.BlockSpec((1, BLOCK, kvw), prev), pl.BlockSpec((1, BLOCK, kvw), cur), pl.BlockSpec((1, BLOCK, kvw), nxt),
            pl.BlockSpec((1, BLOCK, kvw), prev), pl.BlockSpec((1, BLOCK, kvw), cur), pl.BlockSpec((1, BLOCK, kvw), nxt),
            pl.BlockSpec((1, 1, BLOCK), rcur),
            pl.BlockSpec((1, BLOCK, 1), prev), pl.BlockSpec((1, BLOCK, 1), cur), pl.BlockSpec((1, BLOCK, 1), nxt),
        ],
        out_specs=pl.BlockSpec((1, BLOCK, SWA_HEADS * HEAD_DIM), cur),
        out_shape=jax.ShapeDtypeStruct((B, S, SWA_HEADS * HEAD_DIM), F32),
        scratch_shapes=[pltpu.VMEM((SWA_KV_HEADS, 3 * BLOCK, SWA_GROUP * BLOCK), F32),
                        pltpu.VMEM((3 * BLOCK, 1), jnp.int32), pltpu.VMEM((1, BLOCK), jnp.int32)],
        compiler_params=_params(("arbitrary", "arbitrary")), name="swa_attn",
    )(t5, sink, sq, sk, sk, sk, sv, sv, sv, pos_row, pos_col, pos_col, pos_col)


def _out_proj_kernel(x_ref, mla_ref, swa_ref, g1_ref, g2_ref, wo_ref, gf_ref, xo_ref, h2_ref):
    w = wo_ref.shape[0] // 2
    m1 = _rms(mla_ref[...], g1_ref[...]).astype(BF16)
    m2 = _rms(swa_ref[...], g2_ref[...]).astype(BF16)
    y = jnp.dot(m1, wo_ref[0:w, :], preferred_element_type=F32)
    y = y + jnp.dot(m2, wo_ref[w:2 * w, :], preferred_element_type=F32)
    xn = x_ref[...] + y
    xo_ref[...] = xn
    h2_ref[...] = _rms(xn, gf_ref[...]).astype(BF16)


def _out_proj(x, mla_o, swa_o, g1, g2, wo, gf, *, tm):
    T, D = x.shape
    W = mla_o.shape[1]
    row = lambda i: (i, 0)
    return pl.pallas_call(
        _out_proj_kernel,
        grid=(T // tm,),
        in_specs=[
            pl.BlockSpec((tm, D), row), pl.BlockSpec((tm, W), row), pl.BlockSpec((tm, W), row),
            _const_spec(g1.shape), _const_spec(g2.shape), _const_spec(wo.shape), _const_spec(gf.shape),
        ],
        out_specs=(pl.BlockSpec((tm, D), row), pl.BlockSpec((tm, D), row)),
        out_shape=(jax.ShapeDtypeStruct((T, D), F32), jax.ShapeDtypeStruct((T, D), BF16)),
        compiler_params=_params(("parallel",)), name="out_proj",
    )(x, mla_o, swa_o, g1, g2, wo, gf)


def _ffn_kernel(hp_ref, h_ref, hn_ref, wg_ref, wu_ref, cw_ref, cb_ref, wd_ref, y_ref, hext_ref, *,
                tm, tiles_per_seq):
    i = pl.program_id(0)
    j = pl.program_id(1)
    halo = BF16_SUBLANES

    @pl.when(j == 0)
    def _():
        t = i % tiles_per_seq
        hext_ref[0:halo, :] = jnp.where(t == 0, jnp.zeros_like(hp_ref[...]), hp_ref[...])
        hext_ref[halo:halo + tm, :] = h_ref[...]
        hext_ref[halo + tm:, :] = jnp.where(t == tiles_per_seq - 1, jnp.zeros_like(hn_ref[...]), hn_ref[...])
        y_ref[...] = jnp.zeros_like(y_ref)

    gp = jnp.dot(hext_ref[...], wg_ref[...], preferred_element_type=F32)
    cw = cw_ref[...]
    g = (gp[halo - 1:halo - 1 + tm] * cw[0:1] + gp[halo:halo + tm] * cw[1:2]
         + gp[halo + 1:halo + 1 + tm] * cw[2:3] + cb_ref[...])
    u = jnp.dot(h_ref[...], wu_ref[...], preferred_element_type=F32)
    a = (g * _sigmoid(g) * u).astype(BF16)
    y_ref[...] += jnp.dot(a, wd_ref[...], preferred_element_type=F32)


def _ffn(h2, wg, wu, cw, cb, wd, *, layer, tm, tf, seq):
    T, D = h2.shape
    Fp = wg.shape[2]
    halo = BF16_SUBLANES
    hb = tm // halo
    nhb = T // halo
    return pl.pallas_call(
        functools.partial(_ffn_kernel, tm=tm, tiles_per_seq=seq // tm),
        grid=(T // tm, Fp // tf),
        in_specs=[
            pl.BlockSpec((halo, D), lambda i, j: (jnp.maximum(i * hb - 1, 0), 0)),
            pl.BlockSpec((tm, D), lambda i, j: (i, 0)),
            pl.BlockSpec((halo, D), lambda i, j: (jnp.minimum((i + 1) * hb, nhb - 1), 0)),
            pl.BlockSpec((None, D, tf), lambda i, j: (layer, 0, j)),
            pl.BlockSpec((None, D, tf), lambda i, j: (layer, 0, j)),
            pl.BlockSpec((None, 3, tf), lambda i, j: (layer, 0, j)),
            pl.BlockSpec((None, 1, tf), lambda i, j: (layer, 0, j)),
            pl.BlockSpec((None, tf, D), lambda i, j: (layer, j, 0)),
        ],
        out_specs=pl.BlockSpec((tm, D), lambda i, j: (i, 0)),
        out_shape=jax.ShapeDtypeStruct((T, D), F32),
        scratch_shapes=[pltpu.VMEM((tm + 2 * halo, D), BF16)],
        compiler_params=_params(("parallel", "arbitrary")), name="ffn",
    )(h2, h2, h2, wg, wu, cw, cb, wd)


def _ple_kernel(x_ref, y_ref, p_ref, wpg_ref, bpg_ref, wpp_ref, gfin_ref, o_ref, *, final):
    x2 = x_ref[...] + y_ref[...]
    gate = _sigmoid(jnp.dot(x2.astype(BF16), wpg_ref[...], preferred_element_type=F32) + bpg_ref[...])
    pp = jnp.dot(p_ref[...].astype(BF16), wpp_ref[...], preferred_element_type=F32)
    x3 = x2 + gate * pp
    if final:
        x3 = _rms(x3, gfin_ref[...])
    o_ref[...] = x3


def _ple(x, y, p, wpg, bpg, wpp, gfin, *, tm, final):
    T, D = x.shape
    P = p.shape[1]
    row = lambda i: (i, 0)
    return pl.pallas_call(
        functools.partial(_ple_kernel, final=final),
        grid=(T // tm,),
        in_specs=[
            pl.BlockSpec((tm, D), row), pl.BlockSpec((tm, D), row), pl.BlockSpec((tm, P), row),
            _const_spec(wpg.shape), _const_spec(bpg.shape), _const_spec(wpp.shape), _const_spec(gfin.shape),
        ],
        out_specs=pl.BlockSpec((tm, D), row),
        out_shape=jax.ShapeDtypeStruct((T, D), F32),
        compiler_params=_params(("parallel",)), name="ple",
    )(x, y, p, wpg, bpg, wpp, gfin)


def _prep_w_in(w_in):
    half = MLA_ROPE // 2
    kr0 = Q_RANK + KV_RANK
    t1, t2 = w_in[:, kr0:kr0 + half], w_in[:, kr0 + half:kr0 + MLA_ROPE]
    zpad = jnp.zeros((w_in.shape[0], LANES - MLA_ROPE), w_in.dtype)
    return jnp.concatenate(
        [w_in[:, :kr0], t1, t2, zpad, t2, t1, zpad, w_in[:, kr0 + MLA_ROPE:]], axis=1).astype(BF16)


def _prep_w_uq(w_uq):
    half = MLA_ROPE // 2
    w = w_uq.reshape(Q_RANK, MLA_HEADS, MLA_NOPE + MLA_ROPE)
    nope, t1, t2 = w[..., :MLA_NOPE], w[..., MLA_NOPE:MLA_NOPE + half], w[..., MLA_NOPE + half:]
    zpad = jnp.zeros((Q_RANK, MLA_HEADS, LANES - MLA_ROPE), w_uq.dtype)
    cols = jnp.concatenate([nope, t1, t2, zpad, t2, t1, zpad], axis=-1)
    return cols.reshape(Q_RANK, MLA_HEADS * 3 * LANES).T.astype(BF16)


def _prep_w_ukv(w_ukv):
    w = w_ukv.reshape(KV_RANK, MLA_HEADS, MLA_NOPE + MLA_V)
    wkn = w[..., :MLA_NOPE].reshape(KV_RANK, MLA_HEADS * MLA_NOPE).astype(BF16)
    wvt = w[..., MLA_NOPE:].reshape(KV_RANK, MLA_HEADS * MLA_V).T.astype(BF16)
    return wkn, wvt


def _pad_ff(w, axis, fp):
    pad = [(0, 0)] * w.ndim
    pad[axis] = (0, fp - w.shape[axis])
    return jnp.pad(w, pad)


def _pick_tile(n, pref):
    t = min(pref, n)
    while n % t:
        t //= 2
    return t


def kernel(x, p, positions, attn_norm, w_in, cq_norm, ckv_norm, w_uq, w_ukv, swa_sink, t5_bias,
           mla_out_norm, swa_out_norm, w_o, ffn_norm, w_gate, w_up, conv_w, conv_b, w_down,
           ple_gate_w, ple_gate_b, ple_proj, final_norm):
    B, S, D = x.shape
    depth = w_in.shape[0]
    T = B * S
    d_ff = w_gate.shape[-1]
    tf = 512
    fp = -(-d_ff // tf) * tf

    tm_in = _pick_tile(S, 512)
    tq = _pick_tile(S, 512)
    tk = _pick_tile(S, 512)
    tm_out = _pick_tile(S, 512)
    tm_ffn = _pick_tile(S, 1024)
    tm_ple = _pick_tile(S, 512)

    pos_row = positions.reshape(B, 1, S)
    pos_col = positions.reshape(B, S, 1)
    invf = (ROPE_THETA ** (-jnp.arange(0, MLA_ROPE, 2, dtype=F32) / MLA_ROPE)).reshape(MLA_ROPE // 2, 1)
    row = lambda v: v.reshape(1, -1)
    t5_tab = jnp.pad(t5_bias.T, ((0, 0), (0, LANES - NUM_BUCKETS)))
    wg_all = _pad_ff(w_gate.astype(BF16), 2, fp)
    wu_all = _pad_ff(w_up.astype(BF16), 2, fp)
    cw_all = _pad_ff(conv_w, 2, fp)
    cb_all = _pad_ff(conv_b[:, None, :], 2, fp)
    wd_all = _pad_ff(w_down.astype(BF16), 1, fp)

    xf = x
    for i in range(depth):
        wkn, wvt = _prep_w_ukv(w_ukv[i])
        qt, k, vt, sq, sk, sv = _in_proj(
            xf.reshape(B, S, D), pos_row, invf, row(attn_norm[i]), _prep_w_in(w_in[i]),
            row(cq_norm[i]), row(ckv_norm[i]), _prep_w_uq(w_uq[i]), wkn, wvt, tm=tm_in)
        mla_o = _mla_attn(qt, k, vt, tq=tq, tk=tk, streams=2 if S % (2 * tq) == 0 else 1)
        swa_o = _swa_attn(t5_tab, row(swa_sink[i]), sq, sk, sv, pos_col, pos_row)
        x1, h2 = _out_proj(
            xf.reshape(T, D), mla_o.reshape(T, -1), swa_o.reshape(T, -1), row(mla_out_norm[i]),
            row(swa_out_norm[i]), w_o[i].astype(BF16), row(ffn_norm[i]), tm=tm_out)
        y = _ffn(h2, wg_all, wu_all, cw_all, cb_all, wd_all, layer=i, tm=tm_ffn, tf=tf, seq=S)
        xf = _ple(
            x1, y, p[i].reshape(T, -1), ple_gate_w[i].astype(BF16), row(ple_gate_b[i]),
            ple_proj[i].astype(BF16), row(final_norm), tm=tm_ple, final=(i == depth - 1))
    return xf.reshape(B, S, D)
```

```python
import functools
import math

import jax
import jax.numpy as jnp
from jax import lax
from jax.experimental import pallas as pl
from jax.experimental.pallas import tpu as pltpu

F32 = jnp.float32
BF16 = jnp.bfloat16

EPS = 1e-6
BLOCK = 128
MLA_HEADS = 8
MLA_NOPE = 128
MLA_ROPE = 64
MLA_V = 128
MLA_QK_PAD = 256
Q_RANK = 384
KV_RANK = 256
ROPE_THETA = 10000.0
SWA_HEADS = 8
SWA_KV_HEADS = 2
SWA_GROUP = SWA_HEADS // SWA_KV_HEADS
HEAD_DIM = 128
WINDOW = 128
NUM_BUCKETS = 32
NEG = -1e30
LOG2E = math.log2(math.e)

V7X_VMEM_BYTES = 64 * 1024 * 1024
VMEM_LIMIT_BYTES = V7X_VMEM_BYTES - 8 * 1024 * 1024
LANES = 128
BF16_SUBLANES = 16
MLA_V_EXT = MLA_V + BF16_SUBLANES

_Z_CQ = 0
_Z_CKV = _Z_CQ + Q_RANK
_Z_KA = _Z_CKV + KV_RANK
_Z_KB = _Z_KA + LANES
_Z_SQ = _Z_KB + LANES
_Z_SK = _Z_SQ + SWA_HEADS * HEAD_DIM
_Z_SV = _Z_SK + SWA_KV_HEADS * HEAD_DIM
_Z_END = _Z_SV + SWA_KV_HEADS * HEAD_DIM


def _params(sem):
    return pltpu.CompilerParams(dimension_semantics=sem, vmem_limit_bytes=VMEM_LIMIT_BYTES)


def _const_spec(shape):
    nd = len(shape)
    return pl.BlockSpec(shape, lambda *_: (0,) * nd, pipeline_mode=pl.Buffered(1))


def _rms(x, g):
    ms = jnp.mean(x * x, axis=-1, keepdims=True)
    return x * lax.rsqrt(ms + EPS) * g


def _sigmoid(x):
    return 1.0 / (1.0 + jnp.exp(-x))


def _in_proj_kernel(x_ref, pos_ref, invf_ref, g_ref, win_ref, cqn_ref, ckvn_ref, wq_ref, wkn_ref,
                    wvt_ref, qt_ref, k_ref, vt_ref, sq_ref, sk_ref, sv_ref, *, q_scale, swa_scale):
    x = x_ref[0]
    h = _rms(x, g_ref[...]).astype(BF16)
    z = jnp.dot(h, win_ref[...], preferred_element_type=F32)
    cq = _rms(z[:, _Z_CQ:_Z_CKV], cqn_ref[...]).astype(BF16)
    ckv = _rms(z[:, _Z_CKV:_Z_KA], ckvn_ref[...]).astype(BF16)

    ang = invf_ref[...] * pos_ref[0].astype(F32)
    cos_t, sin_t = jnp.cos(ang), jnp.sin(ang)
    zpad = jnp.zeros((LANES - MLA_ROPE, ang.shape[1]), F32)
    cc_t = jnp.concatenate([cos_t, cos_t, zpad], axis=0)
    ss_t = jnp.concatenate([-sin_t, sin_t, zpad], axis=0)

    r = lax.dot_general(wq_ref[...], cq, (((1,), (1,)), ((), ())), preferred_element_type=F32)
    for hd in range(MLA_HEADS):
        b = hd * 3 * LANES
        qt_ref[0, hd, 0:LANES, :] = (r[b:b + LANES] * q_scale).astype(BF16)
        roped = r[b + LANES:b + 2 * LANES] * cc_t + r[b + 2 * LANES:b + 3 * LANES] * ss_t
        qt_ref[0, hd, LANES:2 * LANES, :] = (roped * q_scale).astype(BF16)

    knope = jnp.dot(ckv, wkn_ref[...], preferred_element_type=F32)
    cc, ss = cc_t.T, ss_t.T
    krope = (z[:, _Z_KA:_Z_KB] * cc + z[:, _Z_KB:_Z_SQ] * ss).astype(BF16)
    for hd in range(MLA_HEADS):
        k_ref[0, hd, :, 0:LANES] = knope[:, hd * LANES:(hd + 1) * LANES].astype(BF16)
        k_ref[0, hd, :, LANES:MLA_QK_PAD] = krope

    vt = lax.dot_general(wvt_ref[...], ckv, (((1,), (1,)), ((), ())), preferred_element_type=F32)
    ones_row = (lax.broadcasted_iota(jnp.int32, (BF16_SUBLANES, vt.shape[1]), 0) == 0).astype(BF16)
    for hd in range(MLA_HEADS):
        vt_ref[0, hd, 0:MLA_V, :] = vt[hd * MLA_V:(hd + 1) * MLA_V].astype(BF16)
        vt_ref[0, hd, MLA_V:MLA_V_EXT, :] = ones_row

    sq_ref[0] = (z[:, _Z_SQ:_Z_SK] * swa_scale).astype(BF16)
    sk_ref[0] = z[:, _Z_SK:_Z_SV].astype(BF16)
    sv_ref[0] = z[:, _Z_SV:_Z_END].astype(BF16)


def _in_proj(x, pos_row, invf, g, win, cqn, ckvn, wq, wkn, wvt, *, tm):
    B, S, D = x.shape
    grid = (B, S // tm)
    q_scale = LOG2E / math.sqrt(MLA_NOPE + MLA_ROPE)
    swa_scale = 1.0 / math.sqrt(HEAD_DIM)
    out_shape = (
        jax.ShapeDtypeStruct((B, MLA_HEADS, MLA_QK_PAD, S), BF16),
        jax.ShapeDtypeStruct((B, MLA_HEADS, S, MLA_QK_PAD), BF16),
        jax.ShapeDtypeStruct((B, MLA_HEADS, MLA_V_EXT, S), BF16),
        jax.ShapeDtypeStruct((B, S, SWA_HEADS * HEAD_DIM), BF16),
        jax.ShapeDtypeStruct((B, S, SWA_KV_HEADS * HEAD_DIM), BF16),
        jax.ShapeDtypeStruct((B, S, SWA_KV_HEADS * HEAD_DIM), BF16),
    )
    in_specs = [
        pl.BlockSpec((1, tm, D), lambda b, i: (b, i, 0)),
        pl.BlockSpec((1, 1, tm), lambda b, i: (b, 0, i)),
        _const_spec(invf.shape), _const_spec(g.shape), _const_spec(win.shape),
        _const_spec(cqn.shape), _const_spec(ckvn.shape), _const_spec(wq.shape),
        _const_spec(wkn.shape), _const_spec(wvt.shape),
    ]
    out_specs = (
        pl.BlockSpec((1, MLA_HEADS, MLA_QK_PAD, tm), lambda b, i: (b, 0, 0, i)),
        pl.BlockSpec((1, MLA_HEADS, tm, MLA_QK_PAD), lambda b, i: (b, 0, i, 0)),
        pl.BlockSpec((1, MLA_HEADS, MLA_V_EXT, tm), lambda b, i: (b, 0, 0, i)),
        pl.BlockSpec((1, tm, SWA_HEADS * HEAD_DIM), lambda b, i: (b, i, 0)),
        pl.BlockSpec((1, tm, SWA_KV_HEADS * HEAD_DIM), lambda b, i: (b, i, 0)),
        pl.BlockSpec((1, tm, SWA_KV_HEADS * HEAD_DIM), lambda b, i: (b, i, 0)),
    )
    return pl.pallas_call(
        functools.partial(_in_proj_kernel, q_scale=q_scale, swa_scale=swa_scale),
        grid=grid, in_specs=in_specs, out_specs=out_specs, out_shape=out_shape,
        compiler_params=_params(("parallel", "parallel")), name="in_proj",
    )(x, pos_row, invf, g, win, cqn, ckvn, wq, wkn, wvt)


def _mla_kernel(qt_ref, qn_ref, k_ref, vt_ref, o_ref, m_ref, acc_ref, s_ref, mc_ref, *, tq, tk, unroll, streams):
    S = k_ref.shape[2]
    n = S // tk
    trips = n // unroll

    def scores(c, j, slot, from_next=None):
        start = pl.multiple_of(c * tk, tk)
        q = qt_ref[0, 0, :, j * tq:(j + 1) * tq]
        if from_next is not None:
            q = jnp.where(from_next, qn_ref[0, 0, :, j * tq:(j + 1) * tq], q)
        s = jnp.dot(k_ref[0, 0, pl.ds(start, tk), :], q, preferred_element_type=F32)
        s_ref[2 * j + slot] = s
        mc_ref[2 * j + slot] = jnp.max(s, axis=0, keepdims=True)

    def softmax_pv(c, j, slot):
        start = pl.multiple_of(c * tk, tk)
        m_prev = m_ref[j]
        m_new = jnp.maximum(m_prev, mc_ref[2 * j + slot])
        alpha = jnp.exp2(m_prev - m_new)
        p = jnp.exp2(s_ref[2 * j + slot] - m_new).astype(BF16)
        vc = vt_ref[0, 0, :, pl.ds(start, tk)]
        acc_ref[j] = alpha * acc_ref[j] + jnp.dot(vc, p, preferred_element_type=F32)
        m_ref[j] = m_new

    @pl.when(pl.program_id(2) == 0)
    def _():
        for j in range(streams):
            scores(0, j, 0)

    for j in range(streams):
        m_ref[j] = jnp.full(m_ref.shape[1:], NEG, F32)
        acc_ref[j] = jnp.zeros(acc_ref.shape[1:], F32)

    def group(i, carry):
        c0 = i * unroll
        last = i == trips - 1
        for u in range(unroll):
            for j in range(streams):
                if u == unroll - 1:
                    scores(jnp.where(last, 0, c0 + u + 1), j, (u + 1) % 2, from_next=last)
                else:
                    scores(c0 + u + 1, j, (u + 1) % 2)
                softmax_pv(c0 + u, j, u % 2)
        return carry

    lax.fori_loop(0, trips, group, 0)
    for j in range(streams):
        acc = acc_ref[j]
        o_ref[0, j * tq:(j + 1) * tq, :] = (acc[0:MLA_V] / acc[MLA_V:MLA_V + 1]).T


def _mla_attn(qt, k, vt, *, tq, tk, streams):
    B, H, _, S = qt.shape
    vrows = vt.shape[2]
    n = S // tk
    unroll = max(u for u in (2, 4, 8, 16) if n % u == 0 and n // u >= 2)
    tqs = tq * streams
    steps = S // tqs
    return pl.pallas_call(
        functools.partial(_mla_kernel, tq=tq, tk=tk, unroll=unroll, streams=streams),
        grid=(B, H, steps),
        in_specs=[
            pl.BlockSpec((1, 1, MLA_QK_PAD, tqs), lambda b, h, i: (b, h, 0, i)),
            pl.BlockSpec((1, 1, MLA_QK_PAD, tqs), lambda b, h, i: (b, h, 0, jnp.minimum(i + 1, steps - 1))),
            pl.BlockSpec((1, 1, S, MLA_QK_PAD), lambda b, h, i: (b, h, 0, 0)),
            pl.BlockSpec((1, 1, vrows, S), lambda b, h, i: (b, h, 0, 0)),
        ],
        out_specs=pl.BlockSpec((1, tqs, MLA_V), lambda b, h, i: (b, i, h)),
        out_shape=jax.ShapeDtypeStruct((B, S, H * MLA_V), F32),
        scratch_shapes=[pltpu.VMEM((streams, 1, tq), F32), pltpu.VMEM((streams, vrows, tq), F32),
                        pltpu.VMEM((2 * streams, tk, tq), F32), pltpu.VMEM((2 * streams, 1, tq), F32)],
        compiler_params=_params(("arbitrary", "arbitrary", "arbitrary")), name="mla_attn",
    )(qt, qt, k, vt)


_T5_STEPS = (12, 16, 23, 32, 46, 64, 91, 128)


def _swa_kernel(t5_ref, sink_ref, q_ref, kp_ref, kc_ref, kn_ref, vp_ref, vc_ref, vn_ref,
                pq_ref, pp_ref, pc_ref, pn_ref, rebuild_ref, o_ref, bias_ref, *, nb):
    n = pl.program_id(1)

    @pl.when(rebuild_ref[pl.program_id(0), n] != 0)
    def _():
        pq = pq_ref[0]
        pk = jnp.concatenate([pp_ref[0], pc_ref[0], pn_ref[0]], axis=0)
        rel = pk - pq
        row = lax.broadcasted_iota(jnp.int32, rel.shape, 0)
        lo = jnp.where(n > 0, 0, BLOCK)
        hi = jnp.where(n < nb - 1, 3 * BLOCK, 2 * BLOCK)
        na = jnp.abs(rel)
        mask = (na <= WINDOW) & (row >= lo) & (row < hi)
        large = jnp.full(na.shape, NUM_BUCKETS // 4, jnp.int32)
        for t in _T5_STEPS:
            large = large + (na >= t).astype(jnp.int32)
        large = jnp.minimum(large, NUM_BUCKETS // 2 - 1)
        bkt = jnp.where(rel > 0, NUM_BUCKETS // 2, 0) + jnp.where(na < NUM_BUCKETS // 4, na, large)
        bkts = [bkt[j * BLOCK:(j + 1) * BLOCK] for j in range(3)]
        for hd in range(SWA_HEADS):
            tab = jnp.broadcast_to(t5_ref[hd:hd + 1, :], (BLOCK, LANES))
            bias = jnp.concatenate([jnp.take_along_axis(tab, b, axis=1) for b in bkts], axis=0)
            g, j = divmod(hd, SWA_GROUP)
            bias_ref[g, :, j * BLOCK:(j + 1) * BLOCK] = jnp.where(mask, bias, NEG)

    def transpose_bf16(a):
        return a.astype(F32).T.astype(BF16)

    q = q_ref[0]
    for g in range(SWA_KV_HEADS):
        lo, hi = g * HEAD_DIM, (g + 1) * HEAD_DIM
        k3 = jnp.concatenate([kp_ref[0, :, lo:hi], kc_ref[0, :, lo:hi], kn_ref[0, :, lo:hi]], axis=0)
        v3t = jnp.concatenate([transpose_bf16(r[0, :, lo:hi]) for r in (vp_ref, vc_ref, vn_ref)], axis=1)
        heads = range(g * SWA_GROUP, (g + 1) * SWA_GROUP)
        qt = jnp.concatenate([transpose_bf16(q[:, hd * HEAD_DIM:(hd + 1) * HEAD_DIM]) for hd in heads], axis=1)
        s = jnp.dot(k3, qt, preferred_element_type=F32) + bias_ref[g]
        sk = jnp.concatenate([jnp.full((1, BLOCK), sink_ref[0, hd], F32) for hd in heads], axis=1)
        m = jnp.maximum(jnp.max(s, axis=0, keepdims=True), sk)
        e = jnp.exp(s - m)
        denom = jnp.sum(e, axis=0, keepdims=True) + jnp.exp(sk - m)
        ot = jnp.dot(v3t, e.astype(BF16), preferred_element_type=F32) / denom
        for j, hd in enumerate(heads):
            o_ref[0, :, hd * HEAD_DIM:(hd + 1) * HEAD_DIM] = ot[:, j * BLOCK:(j + 1) * BLOCK].T


def _swa_rebuild_flags(positions):
    B, S = positions.shape
    nb = S // BLOCK
    flags = jnp.ones((B, nb), jnp.int32)
    if nb < 4:
        return flags
    step = (positions[:, BLOCK:] - positions[:, :-BLOCK]).reshape(B, nb - 1, BLOCK)
    lo, hi = step.min(axis=-1), step.max(axis=-1)
    lo3 = jnp.minimum(jnp.minimum(lo[:, :-2], lo[:, 1:-1]), lo[:, 2:])
    hi3 = jnp.maximum(jnp.maximum(hi[:, :-2], hi[:, 1:-1]), hi[:, 2:])
    return flags.at[:, 2:nb - 1].set((lo3 != hi3).astype(jnp.int32))


def _swa_attn(t5, sink, sq, sk, sv, pos_col, pos_row, rebuild):
    B, S, _ = sq.shape
    nb = S // BLOCK
    kvw = SWA_KV_HEADS * HEAD_DIM
    prev = lambda b, n: (b, jnp.maximum(n - 1, 0), 0)
    cur = lambda b, n: (b, n, 0)
    nxt = lambda b, n: (b, jnp.minimum(n + 1, nb - 1), 0)
    rcur = lambda b, n: (b, 0, n)
    smem = pl.BlockSpec(memory_space=pltpu.SMEM)
    return pl.pallas_call(
        functools.partial(_swa_kernel, nb=nb),
        grid=(B, nb),
        in_specs=[
            _const_spec(t5.shape), smem,
            pl.BlockSpec((1, BLOCK, SWA_HEADS * HEAD_DIM), cur),
            pl.BlockSpec((1, BLOCK, kvw), prev), pl.BlockSpec((1, BLOCK, kvw), cur), pl.BlockSpec((1, BLOCK, kvw), nxt),
            pl.BlockSpec((1, BLOCK, kvw), prev), pl.BlockSpec((1, BLOCK, kvw), cur), pl.BlockSpec((1, BLOCK, kvw), nxt),
            pl.BlockSpec((1, 1, BLOCK), rcur),
            pl.BlockSpec((1, BLOCK, 1), prev), pl.BlockSpec((1, BLOCK, 1), cur), pl.BlockSpec((1, BLOCK, 1), nxt),
            smem,
        ],
        out_specs=pl.BlockSpec((1, BLOCK, SWA_HEADS * HEAD_DIM), cur),
        out_shape=jax.ShapeDtypeStruct((B, S, SWA_HEADS * HEAD_DIM), F32),
        scratch_shapes=[pltpu.VMEM((SWA_KV_HEADS, 3 * BLOCK, SWA_GROUP * BLOCK), F32)],
        compiler_params=_params(("arbitrary", "arbitrary")), name="swa_attn",
    )(t5, sink, sq, sk, sk, sk, sv, sv, sv, pos_row, pos_col, pos_col, pos_col, rebuild)


def _out_proj_kernel(x_ref, mla_ref, swa_ref, g1_ref, g2_ref, wo_ref, gf_ref, xo_ref, h2_ref):
    w = wo_ref.shape[0] // 2
    m1 = _rms(mla_ref[...], g1_ref[...]).astype(BF16)
    m2 = _rms(swa_ref[...], g2_ref[...]).astype(BF16)
    y = jnp.dot(m1, wo_ref[0:w, :], preferred_element_type=F32)
    y = y + jnp.dot(m2, wo_ref[w:2 * w, :], preferred_element_type=F32)
    xn = x_ref[...] + y
    xo_ref[...] = xn
    h2_ref[...] = _rms(xn, gf_ref[...]).astype(BF16)


def _out_proj(x, mla_o, swa_o, g1, g2, wo, gf, *, tm):
    T, D = x.shape
    W = mla_o.shape[1]
    row = lambda i: (i, 0)
    return pl.pallas_call(
        _out_proj_kernel,
        grid=(T // tm,),
        in_specs=[
            pl.BlockSpec((tm, D), row), pl.BlockSpec((tm, W), row), pl.BlockSpec((tm, W), row),
            _const_spec(g1.shape), _const_spec(g2.shape), _const_spec(wo.shape), _const_spec(gf.shape),
        ],
        out_specs=(pl.BlockSpec((tm, D), row), pl.BlockSpec((tm, D), row)),
        out_shape=(jax.ShapeDtypeStruct((T, D), F32), jax.ShapeDtypeStruct((T, D), BF16)),
        compiler_params=_params(("parallel",)), name="out_proj",
    )(x, mla_o, swa_o, g1, g2, wo, gf)


def _ffn_kernel(hp_ref, h_ref, hn_ref, wg_ref, wu_ref, cw_ref, cb_ref, wd_ref, y_ref, hext_ref, *,
                tm, tiles_per_seq):
    i = pl.program_id(0)
    j = pl.program_id(1)
    halo = BF16_SUBLANES

    @pl.when(j == 0)
    def _():
        t = i % tiles_per_seq
        hext_ref[0:halo, :] = jnp.where(t == 0, jnp.zeros_like(hp_ref[...]), hp_ref[...])
        hext_ref[halo:halo + tm, :] = h_ref[...]
        hext_ref[halo + tm:, :] = jnp.where(t == tiles_per_seq - 1, jnp.zeros_like(hn_ref[...]), hn_ref[...])
        y_ref[...] = jnp.zeros_like(y_ref)

    gp = jnp.dot(hext_ref[...], wg_ref[...], preferred_element_type=F32)
    cw = cw_ref[...]
    g = (gp[halo - 1:halo - 1 + tm] * cw[0:1] + gp[halo:halo + tm] * cw[1:2]
         + gp[halo + 1:halo + 1 + tm] * cw[2:3] + cb_ref[...])
    u = jnp.dot(h_ref[...], wu_ref[...], preferred_element_type=F32)
    a = (g * _sigmoid(g) * u).astype(BF16)
    y_ref[...] += jnp.dot(a, wd_ref[...], preferred_element_type=F32)


def _ffn(h2, wg, wu, cw, cb, wd, *, layer, tm, tf, seq):
    T, D = h2.shape
    Fp = wg.shape[2]
    halo = BF16_SUBLANES
    hb = tm // halo
    nhb = T // halo
    return pl.pallas_call(
        functools.partial(_ffn_kernel, tm=tm, tiles_per_seq=seq // tm),
        grid=(T // tm, Fp // tf),
        in_specs=[
            pl.BlockSpec((halo, D), lambda i, j: (jnp.maximum(i * hb - 1, 0), 0)),
            pl.BlockSpec((tm, D), lambda i, j: (i, 0)),
            pl.BlockSpec((halo, D), lambda i, j: (jnp.minimum((i + 1) * hb, nhb - 1), 0)),
            pl.BlockSpec((None, D, tf), lambda i, j: (layer, 0, j)),
            pl.BlockSpec((None, D, tf), lambda i, j: (layer, 0, j)),
            pl.BlockSpec((None, 3, tf), lambda i, j: (layer, 0, j)),
            pl.BlockSpec((None, 1, tf), lambda i, j: (layer, 0, j)),
            pl.BlockSpec((None, tf, D), lambda i, j: (layer, j, 0)),
        ],
        out_specs=pl.BlockSpec((tm, D), lambda i, j: (i, 0)),
        out_shape=jax.ShapeDtypeStruct((T, D), F32),
        scratch_shapes=[pltpu.VMEM((tm + 2 * halo, D), BF16)],
        compiler_params=_params(("parallel", "arbitrary")), name="ffn",
    )(h2, h2, h2, wg, wu, cw, cb, wd)


def _ple_kernel(x_ref, y_ref, p_ref, wpg_ref, bpg_ref, wpp_ref, gfin_ref, o_ref, *, final):
    x2 = x_ref[...] + y_ref[...]
    gate = _sigmoid(jnp.dot(x2.astype(BF16), wpg_ref[...], preferred_element_type=F32) + bpg_ref[...])
    pp = jnp.dot(p_ref[...].astype(BF16), wpp_ref[...], preferred_element_type=F32)
    x3 = x2 + gate * pp
    if final:
        x3 = _rms(x3, gfin_ref[...])
    o_ref[...] = x3


def _ple(x, y, p, wpg, bpg, wpp, gfin, *, tm, final):
    T, D = x.shape
    P = p.shape[1]
    row = lambda i: (i, 0)
    return pl.pallas_call(
        functools.partial(_ple_kernel, final=final),
        grid=(T // tm,),
        in_specs=[
            pl.BlockSpec((tm, D), row), pl.BlockSpec((tm, D), row), pl.BlockSpec((tm, P), row),
            _const_spec(wpg.shape), _const_spec(bpg.shape), _const_spec(wpp.shape), _const_spec(gfin.shape),
        ],
        out_specs=pl.BlockSpec((tm, D), row),
        out_shape=jax.ShapeDtypeStruct((T, D), F32),
        compiler_params=_params(("parallel",)), name="ple",
    )(x, y, p, wpg, bpg, wpp, gfin)


def _prep_w_in(w_in):
    half = MLA_ROPE // 2
    kr0 = Q_RANK + KV_RANK
    t1, t2 = w_in[:, kr0:kr0 + half], w_in[:, kr0 + half:kr0 + MLA_ROPE]
    zpad = jnp.zeros((w_in.shape[0], LANES - MLA_ROPE), w_in.dtype)
    return jnp.concatenate(
        [w_in[:, :kr0], t1, t2, zpad, t2, t1, zpad, w_in[:, kr0 + MLA_ROPE:]], axis=1).astype(BF16)


def _prep_w_uq(w_uq):
    half = MLA_ROPE // 2
    w = w_uq.reshape(Q_RANK, MLA_HEADS, MLA_NOPE + MLA_ROPE)
    nope, t1, t2 = w[..., :MLA_NOPE], w[..., MLA_NOPE:MLA_NOPE + half], w[..., MLA_NOPE + half:]
    zpad = jnp.zeros((Q_RANK, MLA_HEADS, LANES - MLA_ROPE), w_uq.dtype)
    cols = jnp.concatenate([nope, t1, t2, zpad, t2, t1, zpad], axis=-1)
    return cols.reshape(Q_RANK, MLA_HEADS * 3 * LANES).T.astype(BF16)


def _prep_w_ukv(w_ukv):
    w = w_ukv.reshape(KV_RANK, MLA_HEADS, MLA_NOPE + MLA_V)
    wkn = w[..., :MLA_NOPE].reshape(KV_RANK, MLA_HEADS * MLA_NOPE).astype(BF16)
    wvt = w[..., MLA_NOPE:].reshape(KV_RANK, MLA_HEADS * MLA_V).T.astype(BF16)
    return wkn, wvt


def _pad_ff(w, axis, fp):
    pad = [(0, 0)] * w.ndim
    pad[axis] = (0, fp - w.shape[axis])
    return jnp.pad(w, pad)


def _pick_tile(n, pref):
    t = min(pref, n)
    while n % t:
        t //= 2
    return t


def kernel(x, p, positions, attn_norm, w_in, cq_norm, ckv_norm, w_uq, w_ukv, swa_sink, t5_bias,
           mla_out_norm, swa_out_norm, w_o, ffn_norm, w_gate, w_up, conv_w, conv_b, w_down,
           ple_gate_w, ple_gate_b, ple_proj, final_norm):
    B, S, D = x.shape
    depth = w_in.shape[0]
    T = B * S
    d_ff = w_gate.shape[-1]
    tf = 512
    fp = -(-d_ff // tf) * tf

    tm_in = _pick_tile(S, 512)
    tq = _pick_tile(S, 512)
    tk = _pick_tile(S, 512)
    tm_out = _pick_tile(S, 512)
    tm_ffn = _pick_tile(S, 1024)
    tm_ple = _pick_tile(S, 512)

    pos_row = positions.reshape(B, 1, S)
    pos_col = positions.reshape(B, S, 1)
    invf = (ROPE_THETA ** (-jnp.arange(0, MLA_ROPE, 2, dtype=F32) / MLA_ROPE)).reshape(MLA_ROPE // 2, 1)
    row = lambda v: v.reshape(1, -1)
    t5_tab = jnp.pad(t5_bias.T, ((0, 0), (0, LANES - NUM_BUCKETS)))
    swa_rebuild = _swa_rebuild_flags(positions)
    wg_all = _pad_ff(w_gate.astype(BF16), 2, fp)
    wu_all = _pad_ff(w_up.astype(BF16), 2, fp)
    cw_all = _pad_ff(conv_w, 2, fp)
    cb_all = _pad_ff(conv_b[:, None, :], 2, fp)
    wd_all = _pad_ff(w_down.astype(BF16), 1, fp)

    xf = x
    for i in range(depth):
        wkn, wvt = _prep_w_ukv(w_ukv[i])
        qt, k, vt, sq, sk, sv = _in_proj(
            xf.reshape(B, S, D), pos_row, invf, row(attn_norm[i]), _prep_w_in(w_in[i]),
            row(cq_norm[i]), row(ckv_norm[i]), _prep_w_uq(w_uq[i]), wkn, wvt, tm=tm_in)
        mla_o = _mla_attn(qt, k, vt, tq=tq, tk=tk, streams=2 if S % (2 * tq) == 0 else 1)
        swa_o = _swa_attn(t5_tab, row(swa_sink[i]), sq, sk, sv, pos_col, pos_row, swa_rebuild)
        x1, h2 = _out_proj(
            xf.reshape(T, D), mla_o.reshape(T, -1), swa_o.reshape(T, -1), row(mla_out_norm[i]),
            row(swa_out_norm[i]), w_o[i].astype(BF16), row(ffn_norm[i]), tm=tm_out)
        y = _ffn(h2, wg_all, wu_all, cw_all, cb_all, wd_all, layer=i, tm=tm_ffn, tf=tf, seq=S)
        xf = _ple(
            x1, y, p[i].reshape(T, -1), ple_gate_w[i].astype(BF16), row(ple_gate_b[i]),
            ple_proj[i].astype(BF16), row(final_norm), tm=tm_ple, final=(i == depth - 1))
    return xf.reshape(B, S, D)
```

```python
import functools
import math

import jax
import jax.numpy as jnp
from jax import lax
from jax.experimental import pallas as pl
from jax.experimental.pallas import tpu as pltpu

F32 = jnp.float32
BF16 = jnp.bfloat16

EPS = 1e-6
BLOCK = 128
MLA_HEADS = 8
MLA_NOPE = 128
MLA_ROPE = 64
MLA_V = 128
MLA_QK_PAD = 256
Q_RANK = 384
KV_RANK = 256
ROPE_THETA = 10000.0
SWA_HEADS = 8
SWA_KV_HEADS = 2
SWA_GROUP = SWA_HEADS // SWA_KV_HEADS
HEAD_DIM = 128
WINDOW = 128
NUM_BUCKETS = 32
NEG = -1e30
LOG2E = math.log2(math.e)

V7X_VMEM_BYTES = 64 * 1024 * 1024
VMEM_LIMIT_BYTES = V7X_VMEM_BYTES - 8 * 1024 * 1024
LANES = 128
BF16_SUBLANES = 16
MLA_V_EXT = MLA_V + BF16_SUBLANES

_Z_CQ = 0
_Z_CKV = _Z_CQ + Q_RANK
_Z_KR = _Z_CKV + KV_RANK
_Z_SQ = _Z_KR + LANES
_Z_SK = _Z_SQ + SWA_HEADS * HEAD_DIM
_Z_SV = _Z_SK + SWA_KV_HEADS * HEAD_DIM
_Z_END = _Z_SV + SWA_KV_HEADS * HEAD_DIM


def _params(sem):
    return pltpu.CompilerParams(dimension_semantics=sem, vmem_limit_bytes=VMEM_LIMIT_BYTES)


def _const_spec(shape):
    nd = len(shape)
    return pl.BlockSpec(shape, lambda *_: (0,) * nd, pipeline_mode=pl.Buffered(1))


def _rms(x, g):
    ms = jnp.mean(x * x, axis=-1, keepdims=True)
    return x * lax.rsqrt(ms + EPS) * g


def _sigmoid(x):
    return 1.0 / (1.0 + jnp.exp(-x))


def _in_proj_kernel(x_ref, pos_ref, invf_ref, g_ref, win_ref, cqn_ref, ckvn_ref, wq_ref, wkn_ref,
                    wvt_ref, qt_ref, k_ref, vt_ref, sq_ref, sk_ref, sv_ref, *, q_scale, swa_scale):
    x = x_ref[0]
    h = _rms(x, g_ref[...]).astype(BF16)
    z = jnp.dot(h, win_ref[...], preferred_element_type=F32)
    cq = _rms(z[:, _Z_CQ:_Z_CKV], cqn_ref[...]).astype(BF16)
    ckv = _rms(z[:, _Z_CKV:_Z_KR], ckvn_ref[...]).astype(BF16)

    ang = invf_ref[...] * pos_ref[0].astype(F32)
    cos_t, sin_t = jnp.cos(ang), jnp.sin(ang)
    zpad = jnp.zeros((LANES - MLA_ROPE, ang.shape[1]), F32)
    cc_t = jnp.concatenate([cos_t, cos_t, zpad], axis=0)
    ss_t = jnp.concatenate([-sin_t, sin_t, zpad], axis=0)

    r = lax.dot_general(wq_ref[...], cq, (((1,), (1,)), ((), ())), preferred_element_type=F32)
    half = MLA_ROPE // 2
    qzero = jnp.zeros((MLA_QK_PAD - MLA_NOPE - MLA_ROPE, r.shape[1]), BF16)
    for hd in range(MLA_HEADS):
        b = hd * (MLA_NOPE + MLA_ROPE)
        qt_ref[0, hd, 0:MLA_NOPE, :] = (r[b:b + MLA_NOPE] * q_scale).astype(BF16)
        t = r[b + MLA_NOPE:b + MLA_NOPE + MLA_ROPE]
        partner = jnp.concatenate([t[half:], t[:half]], axis=0)
        roped = t * cc_t[0:MLA_ROPE] + partner * ss_t[0:MLA_ROPE]
        qt_ref[0, hd, MLA_NOPE:MLA_NOPE + MLA_ROPE, :] = (roped * q_scale).astype(BF16)
        qt_ref[0, hd, MLA_NOPE + MLA_ROPE:MLA_QK_PAD, :] = qzero

    knope = jnp.dot(ckv, wkn_ref[...], preferred_element_type=F32)
    cc, ss = cc_t.T, ss_t.T
    kr = z[:, _Z_KR:_Z_SQ]
    lane = lax.broadcasted_iota(jnp.int32, kr.shape, 1)
    partner = jnp.where(lane < MLA_ROPE // 2, pltpu.roll(kr, LANES - MLA_ROPE // 2, 1),
                        pltpu.roll(kr, MLA_ROPE // 2, 1))
    krope = (kr * cc + partner * ss).astype(BF16)
    for hd in range(MLA_HEADS):
        k_ref[0, hd, :, 0:LANES] = knope[:, hd * LANES:(hd + 1) * LANES].astype(BF16)
        k_ref[0, hd, :, LANES:MLA_QK_PAD] = krope

    vt = lax.dot_general(wvt_ref[...], ckv, (((1,), (1,)), ((), ())), preferred_element_type=F32)
    ones_row = (lax.broadcasted_iota(jnp.int32, (BF16_SUBLANES, vt.shape[1]), 0) == 0).astype(BF16)
    for hd in range(MLA_HEADS):
        vt_ref[0, hd, 0:MLA_V, :] = vt[hd * MLA_V:(hd + 1) * MLA_V].astype(BF16)
        vt_ref[0, hd, MLA_V:MLA_V_EXT, :] = ones_row

    sq_ref[0] = (z[:, _Z_SQ:_Z_SK] * swa_scale).astype(BF16)
    sk_ref[0] = z[:, _Z_SK:_Z_SV].astype(BF16)
    sv_ref[0] = z[:, _Z_SV:_Z_END].astype(BF16)


def _in_proj(x, pos_row, invf, g, win, cqn, ckvn, wq, wkn, wvt, *, tm):
    B, S, D = x.shape
    grid = (B, S // tm)
    q_scale = LOG2E / math.sqrt(MLA_NOPE + MLA_ROPE)
    swa_scale = 1.0 / math.sqrt(HEAD_DIM)
    out_shape = (
        jax.ShapeDtypeStruct((B, MLA_HEADS, MLA_QK_PAD, S), BF16),
        jax.ShapeDtypeStruct((B, MLA_HEADS, S, MLA_QK_PAD), BF16),
        jax.ShapeDtypeStruct((B, MLA_HEADS, MLA_V_EXT, S), BF16),
        jax.ShapeDtypeStruct((B, S, SWA_HEADS * HEAD_DIM), BF16),
        jax.ShapeDtypeStruct((B, S, SWA_KV_HEADS * HEAD_DIM), BF16),
        jax.ShapeDtypeStruct((B, S, SWA_KV_HEADS * HEAD_DIM), BF16),
    )
    in_specs = [
        pl.BlockSpec((1, tm, D), lambda b, i: (b, i, 0)),
        pl.BlockSpec((1, 1, tm), lambda b, i: (b, 0, i)),
        _const_spec(invf.shape), _const_spec(g.shape), _const_spec(win.shape),
        _const_spec(cqn.shape), _const_spec(ckvn.shape), _const_spec(wq.shape),
        _const_spec(wkn.shape), _const_spec(wvt.shape),
    ]
    out_specs = (
        pl.BlockSpec((1, MLA_HEADS, MLA_QK_PAD, tm), lambda b, i: (b, 0, 0, i)),
        pl.BlockSpec((1, MLA_HEADS, tm, MLA_QK_PAD), lambda b, i: (b, 0, i, 0)),
        pl.BlockSpec((1, MLA_HEADS, MLA_V_EXT, tm), lambda b, i: (b, 0, 0, i)),
        pl.BlockSpec((1, tm, SWA_HEADS * HEAD_DIM), lambda b, i: (b, i, 0)),
        pl.BlockSpec((1, tm, SWA_KV_HEADS * HEAD_DIM), lambda b, i: (b, i, 0)),
        pl.BlockSpec((1, tm, SWA_KV_HEADS * HEAD_DIM), lambda b, i: (b, i, 0)),
    )
    return pl.pallas_call(
        functools.partial(_in_proj_kernel, q_scale=q_scale, swa_scale=swa_scale),
        grid=grid, in_specs=in_specs, out_specs=out_specs, out_shape=out_shape,
        compiler_params=_params(("parallel", "parallel")), name="in_proj",
    )(x, pos_row, invf, g, win, cqn, ckvn, wq, wkn, wvt)


def _mla_kernel(qt_ref, qn_ref, k_ref, vt_ref, o_ref, m_ref, acc_ref, s_ref, mc_ref, *, tq, tk, unroll, streams):
    S = k_ref.shape[2]
    n = S // tk
    trips = n // unroll

    def scores(c, j, slot, from_next=None):
        start = pl.multiple_of(c * tk, tk)
        q = qt_ref[0, 0, :, j * tq:(j + 1) * tq]
        if from_next is not None:
            q = jnp.where(from_next, qn_ref[0, 0, :, j * tq:(j + 1) * tq], q)
        s = jnp.dot(k_ref[0, 0, pl.ds(start, tk), :], q, preferred_element_type=F32)
        s_ref[2 * j + slot] = s
        mc_ref[2 * j + slot] = jnp.max(s, axis=0, keepdims=True)

    def softmax_pv(c, j, slot):
        start = pl.multiple_of(c * tk, tk)
        m_prev = m_ref[j]
        m_new = jnp.maximum(m_prev, mc_ref[2 * j + slot])
        alpha = jnp.exp2(m_prev - m_new)
        p = jnp.exp2(s_ref[2 * j + slot] - m_new).astype(BF16)
        vc = vt_ref[0, 0, :, pl.ds(start, tk)]
        acc_ref[j] = alpha * acc_ref[j] + jnp.dot(vc, p, preferred_element_type=F32)
        m_ref[j] = m_new

    @pl.when(pl.program_id(2) == 0)
    def _():
        for j in range(streams):
            scores(0, j, 0)

    for j in range(streams):
        m_ref[j] = jnp.full(m_ref.shape[1:], NEG, F32)
        acc_ref[j] = jnp.zeros(acc_ref.shape[1:], F32)

    def group(i, carry):
        c0 = i * unroll
        last = i == trips - 1
        for u in range(unroll):
            for j in range(streams):
                if u == unroll - 1:
                    scores(jnp.where(last, 0, c0 + u + 1), j, (u + 1) % 2, from_next=last)
                else:
                    scores(c0 + u + 1, j, (u + 1) % 2)
                softmax_pv(c0 + u, j, u % 2)
        return carry

    lax.fori_loop(0, trips, group, 0)
    for j in range(streams):
        acc = acc_ref[j]
        o_ref[0, j * tq:(j + 1) * tq, :] = (acc[0:MLA_V] / acc[MLA_V:MLA_V + 1]).T


def _mla_attn(qt, k, vt, *, tq, tk, streams):
    B, H, _, S = qt.shape
    vrows = vt.shape[2]
    n = S // tk
    unroll = max(u for u in (2, 4, 8, 16) if n % u == 0 and n // u >= 2)
    tqs = tq * streams
    steps = S // tqs
    return pl.pallas_call(
        functools.partial(_mla_kernel, tq=tq, tk=tk, unroll=unroll, streams=streams),
        grid=(B, H, steps),
        in_specs=[
            pl.BlockSpec((1, 1, MLA_QK_PAD, tqs), lambda b, h, i: (b, h, 0, i)),
            pl.BlockSpec((1, 1, MLA_QK_PAD, tqs), lambda b, h, i: (b, h, 0, jnp.minimum(i + 1, steps - 1))),
            pl.BlockSpec((1, 1, S, MLA_QK_PAD), lambda b, h, i: (b, h, 0, 0)),
            pl.BlockSpec((1, 1, vrows, S), lambda b, h, i: (b, h, 0, 0)),
        ],
        out_specs=pl.BlockSpec((1, tqs, MLA_V), lambda b, h, i: (b, i, h)),
        out_shape=jax.ShapeDtypeStruct((B, S, H * MLA_V), F32),
        scratch_shapes=[pltpu.VMEM((streams, 1, tq), F32), pltpu.VMEM((streams, vrows, tq), F32),
                        pltpu.VMEM((2 * streams, tk, tq), F32), pltpu.VMEM((2 * streams, 1, tq), F32)],
        compiler_params=_params(("arbitrary", "arbitrary", "arbitrary")), name="mla_attn",
    )(qt, qt, k, vt)


_T5_STEPS = (12, 16, 23, 32, 46, 64, 91, 128)


def _swa_kernel(t5_ref, sink_ref, q_ref, kp_ref, kc_ref, kn_ref, vp_ref, vc_ref, vn_ref,
                pq_ref, pp_ref, pc_ref, pn_ref, rebuild_ref, o_ref, bias_ref, *, nb):
    n = pl.program_id(1)

    @pl.when(rebuild_ref[pl.program_id(0), n] != 0)
    def _():
        pq = pq_ref[0]
        pk = jnp.concatenate([pp_ref[0], pc_ref[0], pn_ref[0]], axis=0)
        rel = pk - pq
        row = lax.broadcasted_iota(jnp.int32, rel.shape, 0)
        lo = jnp.where(n > 0, 0, BLOCK)
        hi = jnp.where(n < nb - 1, 3 * BLOCK, 2 * BLOCK)
        na = jnp.abs(rel)
        mask = (na <= WINDOW) & (row >= lo) & (row < hi)
        large = jnp.full(na.shape, NUM_BUCKETS // 4, jnp.int32)
        for t in _T5_STEPS:
            large = large + (na >= t).astype(jnp.int32)
        large = jnp.minimum(large, NUM_BUCKETS // 2 - 1)
        bkt = jnp.where(rel > 0, NUM_BUCKETS // 2, 0) + jnp.where(na < NUM_BUCKETS // 4, na, large)
        bkts = [bkt[j * BLOCK:(j + 1) * BLOCK] for j in range(3)]
        for hd in range(SWA_HEADS):
            tab = jnp.broadcast_to(t5_ref[hd:hd + 1, :], (BLOCK, LANES))
            bias = jnp.concatenate([jnp.take_along_axis(tab, b, axis=1) for b in bkts], axis=0)
            g, j = divmod(hd, SWA_GROUP)
            bias_ref[g, :, j * BLOCK:(j + 1) * BLOCK] = jnp.where(mask, bias, NEG)

    def transpose_bf16(a):
        return a.astype(F32).T.astype(BF16)

    q = q_ref[0]
    for g in range(SWA_KV_HEADS):
        lo, hi = g * HEAD_DIM, (g + 1) * HEAD_DIM
        k3 = jnp.concatenate([kp_ref[0, :, lo:hi], kc_ref[0, :, lo:hi], kn_ref[0, :, lo:hi]], axis=0)
        v3t = jnp.concatenate([transpose_bf16(r[0, :, lo:hi]) for r in (vp_ref, vc_ref, vn_ref)], axis=1)
        heads = range(g * SWA_GROUP, (g + 1) * SWA_GROUP)
        qt = jnp.concatenate([transpose_bf16(q[:, hd * HEAD_DIM:(hd + 1) * HEAD_DIM]) for hd in heads], axis=1)
        s = jnp.dot(k3, qt, preferred_element_type=F32) + bias_ref[g]
        sk = jnp.concatenate([jnp.full((1, BLOCK), sink_ref[0, hd], F32) for hd in heads], axis=1)
        m = jnp.maximum(jnp.max(s, axis=0, keepdims=True), sk)
        e = jnp.exp(s - m)
        denom = jnp.sum(e, axis=0, keepdims=True) + jnp.exp(sk - m)
        ot = jnp.dot(v3t, e.astype(BF16), preferred_element_type=F32) / denom
        for j, hd in enumerate(heads):
            o_ref[0, :, hd * HEAD_DIM:(hd + 1) * HEAD_DIM] = ot[:, j * BLOCK:(j + 1) * BLOCK].T


def _swa_rebuild_flags(positions):
    B, S = positions.shape
    nb = S // BLOCK
    flags = jnp.ones((B, nb), jnp.int32)
    if nb < 4:
        return flags
    step = (positions[:, BLOCK:] - positions[:, :-BLOCK]).reshape(B, nb - 1, BLOCK)
    lo, hi = step.min(axis=-1), step.max(axis=-1)
    lo3 = jnp.minimum(jnp.minimum(lo[:, :-2], lo[:, 1:-1]), lo[:, 2:])
    hi3 = jnp.maximum(jnp.maximum(hi[:, :-2], hi[:, 1:-1]), hi[:, 2:])
    return flags.at[:, 2:nb - 1].set((lo3 != hi3).astype(jnp.int32))


def _swa_attn(t5, sink, sq, sk, sv, pos_col, pos_row, rebuild):
    B, S, _ = sq.shape
    nb = S // BLOCK
    kvw = SWA_KV_HEADS * HEAD_DIM
    prev = lambda b, n: (b, jnp.maximum(n - 1, 0), 0)
    cur = lambda b, n: (b, n, 0)
    nxt = lambda b, n: (b, jnp.minimum(n + 1, nb - 1), 0)
    rcur = lambda b, n: (b, 0, n)
    smem = pl.BlockSpec(memory_space=pltpu.SMEM)
    return pl.pallas_call(
        functools.partial(_swa_kernel, nb=nb),
        grid=(B, nb),
        in_specs=[
            _const_spec(t5.shape), smem,
            pl.BlockSpec((1, BLOCK, SWA_HEADS * HEAD_DIM), cur),
            pl.BlockSpec((1, BLOCK, kvw), prev), pl.BlockSpec((1, BLOCK, kvw), cur), pl.BlockSpec((1, BLOCK, kvw), nxt),
            pl.BlockSpec((1, BLOCK, kvw), prev), pl.BlockSpec((1, BLOCK, kvw), cur), pl.BlockSpec((1, BLOCK, kvw), nxt),
            pl.BlockSpec((1, 1, BLOCK), rcur),
            pl.BlockSpec((1, BLOCK, 1), prev), pl.BlockSpec((1, BLOCK, 1), cur), pl.BlockSpec((1, BLOCK, 1), nxt),
            smem,
        ],
        out_specs=pl.BlockSpec((1, BLOCK, SWA_HEADS * HEAD_DIM), cur),
        out_shape=jax.ShapeDtypeStruct((B, S, SWA_HEADS * HEAD_DIM), F32),
        scratch_shapes=[pltpu.VMEM((SWA_KV_HEADS, 3 * BLOCK, SWA_GROUP * BLOCK), F32)],
        compiler_params=_params(("arbitrary", "arbitrary")), name="swa_attn",
    )(t5, sink, sq, sk, sk, sk, sv, sv, sv, pos_row, pos_col, pos_col, pos_col, rebuild)


def _out_proj_kernel(x_ref, mla_ref, swa_ref, g1_ref, g2_ref, wo_ref, gf_ref, xo_ref, h2_ref):
    w = wo_ref.shape[0] // 2
    m1 = _rms(mla_ref[...], g1_ref[...]).astype(BF16)
    m2 = _rms(swa_ref[...], g2_ref[...]).astype(BF16)
    y = jnp.dot(m1, wo_ref[0:w, :], preferred_element_type=F32)
    y = y + jnp.dot(m2, wo_ref[w:2 * w, :], preferred_element_type=F32)
    xn = x_ref[...] + y
    xo_ref[...] = xn
    h2_ref[...] = _rms(xn, gf_ref[...]).astype(BF16)


def _out_proj(x, mla_o, swa_o, g1, g2, wo, gf, *, tm):
    T, D = x.shape
    W = mla_o.shape[1]
    row = lambda i: (i, 0)
    return pl.pallas_call(
        _out_proj_kernel,
        grid=(T // tm,),
        in_specs=[
            pl.BlockSpec((tm, D), row), pl.BlockSpec((tm, W), row), pl.BlockSpec((tm, W), row),
            _const_spec(g1.shape), _const_spec(g2.shape), _const_spec(wo.shape), _const_spec(gf.shape),
        ],
        out_specs=(pl.BlockSpec((tm, D), row), pl.BlockSpec((tm, D), row)),
        out_shape=(jax.ShapeDtypeStruct((T, D), F32), jax.ShapeDtypeStruct((T, D), BF16)),
        compiler_params=_params(("parallel",)), name="out_proj",
    )(x, mla_o, swa_o, g1, g2, wo, gf)


def _ffn_kernel(hp_ref, h_ref, hn_ref, wg_ref, wu_ref, cw_ref, cb_ref, wd_ref, y_ref, hext_ref, *,
                tm, tiles_per_seq):
    i = pl.program_id(0)
    j = pl.program_id(1)
    halo = BF16_SUBLANES

    @pl.when(j == 0)
    def _():
        t = i % tiles_per_seq
        hext_ref[0:halo, :] = jnp.where(t == 0, jnp.zeros_like(hp_ref[...]), hp_ref[...])
        hext_ref[halo:halo + tm, :] = h_ref[...]
        hext_ref[halo + tm:, :] = jnp.where(t == tiles_per_seq - 1, jnp.zeros_like(hn_ref[...]), hn_ref[...])
        y_ref[...] = jnp.zeros_like(y_ref)

    gp = jnp.dot(hext_ref[...], wg_ref[...], preferred_element_type=F32)
    cw = cw_ref[...]
    g = (gp[halo - 1:halo - 1 + tm] * cw[0:1] + gp[halo:halo + tm] * cw[1:2]
         + gp[halo + 1:halo + 1 + tm] * cw[2:3] + cb_ref[...])
    u = jnp.dot(h_ref[...], wu_ref[...], preferred_element_type=F32)
    a = (g * _sigmoid(g) * u).astype(BF16)
    y_ref[...] += jnp.dot(a, wd_ref[...], preferred_element_type=F32)


def _ffn(h2, wg, wu, cw, cb, wd, *, layer, tm, tf, seq):
    T, D = h2.shape
    Fp = wg.shape[2]
    halo = BF16_SUBLANES
    hb = tm // halo
    nhb = T // halo
    return pl.pallas_call(
        functools.partial(_ffn_kernel, tm=tm, tiles_per_seq=seq // tm),
        grid=(T // tm, Fp // tf),
        in_specs=[
            pl.BlockSpec((halo, D), lambda i, j: (jnp.maximum(i * hb - 1, 0), 0)),
            pl.BlockSpec((tm, D), lambda i, j: (i, 0)),
            pl.BlockSpec((halo, D), lambda i, j: (jnp.minimum((i + 1) * hb, nhb - 1), 0)),
            pl.BlockSpec((None, D, tf), lambda i, j: (layer, 0, j)),
            pl.BlockSpec((None, D, tf), lambda i, j: (layer, 0, j)),
            pl.BlockSpec((None, 3, tf), lambda i, j: (layer, 0, j)),
            pl.BlockSpec((None, 1, tf), lambda i, j: (layer, 0, j)),
            pl.BlockSpec((None, tf, D), lambda i, j: (layer, j, 0)),
        ],
        out_specs=pl.BlockSpec((tm, D), lambda i, j: (i, 0)),
        out_shape=jax.ShapeDtypeStruct((T, D), F32),
        scratch_shapes=[pltpu.VMEM((tm + 2 * halo, D), BF16)],
        compiler_params=_params(("parallel", "arbitrary")), name="ffn",
    )(h2, h2, h2, wg, wu, cw, cb, wd)


def _ple_kernel(x_ref, y_ref, p_ref, wpg_ref, bpg_ref, wpp_ref, gfin_ref, o_ref, *, final):
    x2 = x_ref[...] + y_ref[...]
    gate = _sigmoid(jnp.dot(x2.astype(BF16), wpg_ref[...], preferred_element_type=F32) + bpg_ref[...])
    pp = jnp.dot(p_ref[...].astype(BF16), wpp_ref[...], preferred_element_type=F32)
    x3 = x2 + gate * pp
    if final:
        x3 = _rms(x3, gfin_ref[...])
    o_ref[...] = x3


def _ple(x, y, p, wpg, bpg, wpp, gfin, *, tm, final):
    T, D = x.shape
    P = p.shape[1]
    row = lambda i: (i, 0)
    return pl.pallas_call(
        functools.partial(_ple_kernel, final=final),
        grid=(T // tm,),
        in_specs=[
            pl.BlockSpec((tm, D), row), pl.BlockSpec((tm, D), row), pl.BlockSpec((tm, P), row),
            _const_spec(wpg.shape), _const_spec(bpg.shape), _const_spec(wpp.shape), _const_spec(gfin.shape),
        ],
        out_specs=pl.BlockSpec((tm, D), row),
        out_shape=jax.ShapeDtypeStruct((T, D), F32),
        compiler_params=_params(("parallel",)), name="ple",
    )(x, y, p, wpg, bpg, wpp, gfin)


def _prep_w_in(w_in):
    kr1 = Q_RANK + KV_RANK + MLA_ROPE
    zpad = jnp.zeros((w_in.shape[0], LANES - MLA_ROPE), w_in.dtype)
    return jnp.concatenate([w_in[:, :kr1], zpad, w_in[:, kr1:]], axis=1).astype(BF16)


def _prep_w_uq(w_uq):
    return w_uq.T.astype(BF16)


def _prep_w_ukv(w_ukv):
    w = w_ukv.reshape(KV_RANK, MLA_HEADS, MLA_NOPE + MLA_V)
    wkn = w[..., :MLA_NOPE].reshape(KV_RANK, MLA_HEADS * MLA_NOPE).astype(BF16)
    wvt = w[..., MLA_NOPE:].reshape(KV_RANK, MLA_HEADS * MLA_V).T.astype(BF16)
    return wkn, wvt


def _pad_ff(w, axis, fp):
    pad = [(0, 0)] * w.ndim
    pad[axis] = (0, fp - w.shape[axis])
    return jnp.pad(w, pad)


def _pick_tile(n, pref):
    t = min(pref, n)
    while n % t:
        t //= 2
    return t


def kernel(x, p, positions, attn_norm, w_in, cq_norm, ckv_norm, w_uq, w_ukv, swa_sink, t5_bias,
           mla_out_norm, swa_out_norm, w_o, ffn_norm, w_gate, w_up, conv_w, conv_b, w_down,
           ple_gate_w, ple_gate_b, ple_proj, final_norm):
    B, S, D = x.shape
    depth = w_in.shape[0]
    T = B * S
    d_ff = w_gate.shape[-1]
    tf = 512
    fp = -(-d_ff // tf) * tf

    tm_in = _pick_tile(S, 512)
    tq = _pick_tile(S, 512)
    tk = _pick_tile(S, 512)
    tm_out = _pick_tile(S, 512)
    tm_ffn = _pick_tile(S, 1024)
    tm_ple = _pick_tile(S, 512)

    pos_row = positions.reshape(B, 1, S)
    pos_col = positions.reshape(B, S, 1)
    invf = (ROPE_THETA ** (-jnp.arange(0, MLA_ROPE, 2, dtype=F32) / MLA_ROPE)).reshape(MLA_ROPE // 2, 1)
    row = lambda v: v.reshape(1, -1)
    t5_tab = jnp.pad(t5_bias.T, ((0, 0), (0, LANES - NUM_BUCKETS)))
    swa_rebuild = _swa_rebuild_flags(positions)
    wg_all = _pad_ff(w_gate.astype(BF16), 2, fp)
    wu_all = _pad_ff(w_up.astype(BF16), 2, fp)
    cw_all = _pad_ff(conv_w, 2, fp)
    cb_all = _pad_ff(conv_b[:, None, :], 2, fp)
    wd_all = _pad_ff(w_down.astype(BF16), 1, fp)

    xf = x
    for i in range(depth):
        wkn, wvt = _prep_w_ukv(w_ukv[i])
        qt, k, vt, sq, sk, sv = _in_proj(
            xf.reshape(B, S, D), pos_row, invf, row(attn_norm[i]), _prep_w_in(w_in[i]),
            row(cq_norm[i]), row(ckv_norm[i]), _prep_w_uq(w_uq[i]), wkn, wvt, tm=tm_in)
        mla_o = _mla_attn(qt, k, vt, tq=tq, tk=tk, streams=2 if S % (2 * tq) == 0 else 1)
        swa_o = _swa_attn(t5_tab, row(swa_sink[i]), sq, sk, sv, pos_col, pos_row, swa_rebuild)
        x1, h2 = _out_proj(
            xf.reshape(T, D), mla_o.reshape(T, -1), swa_o.reshape(T, -1), row(mla_out_norm[i]),
            row(swa_out_norm[i]), w_o[i].astype(BF16), row(ffn_norm[i]), tm=tm_out)
        y = _ffn(h2, wg_all, wu_all, cw_all, cb_all, wd_all, layer=i, tm=tm_ffn, tf=tf, seq=S)
        xf = _ple(
            x1, y, p[i].reshape(T, -1), ple_gate_w[i].astype(BF16), row(ple_gate_b[i]),
            ple_proj[i].astype(BF16), row(final_norm), tm=tm_ple, final=(i == depth - 1))
    return xf.reshape(B, S, D)
```

```python
import functools
import math

import jax
import jax.numpy as jnp
from jax import lax
from jax.experimental import pallas as pl
from jax.experimental.pallas import tpu as pltpu

F32 = jnp.float32
BF16 = jnp.bfloat16

EPS = 1e-6
BLOCK = 128
MLA_HEADS = 8
MLA_NOPE = 128
MLA_ROPE = 64
MLA_V = 128
MLA_QK_PAD = 256
Q_RANK = 384
KV_RANK = 256
ROPE_THETA = 10000.0
SWA_HEADS = 8
SWA_KV_HEADS = 2
SWA_GROUP = SWA_HEADS // SWA_KV_HEADS
HEAD_DIM = 128
WINDOW = 128
NUM_BUCKETS = 32
NEG = -1e30
LOG2E = math.log2(math.e)

V7X_VMEM_BYTES = 64 * 1024 * 1024
VMEM_LIMIT_BYTES = V7X_VMEM_BYTES - 8 * 1024 * 1024
LANES = 128
BF16_SUBLANES = 16
MLA_V_EXT = MLA_V + BF16_SUBLANES

_Z_CQ = 0
_Z_CKV = _Z_CQ + Q_RANK
_Z_KR = _Z_CKV + KV_RANK
_Z_SQ = _Z_KR + LANES
_Z_SK = _Z_SQ + SWA_HEADS * HEAD_DIM
_Z_SV = _Z_SK + SWA_KV_HEADS * HEAD_DIM
_Z_END = _Z_SV + SWA_KV_HEADS * HEAD_DIM


def _params(sem):
    return pltpu.CompilerParams(dimension_semantics=sem, vmem_limit_bytes=VMEM_LIMIT_BYTES)


def _const_spec(shape):
    nd = len(shape)
    return pl.BlockSpec(shape, lambda *_: (0,) * nd, pipeline_mode=pl.Buffered(1))


def _rms(x, g):
    ms = jnp.mean(x * x, axis=-1, keepdims=True)
    return x * lax.rsqrt(ms + EPS) * g


def _sigmoid(x):
    return 1.0 / (1.0 + jnp.exp(-x))


def _in_proj_kernel(x_ref, pos_ref, invf_ref, g_ref, win_ref, cqn_ref, ckvn_ref, wq_ref, wkn_ref,
                    wvt_ref, qt_ref, k_ref, vt_ref, sq_ref, sk_ref, sv_ref, *, q_scale, swa_scale):
    x = x_ref[0]
    h = _rms(x, g_ref[...]).astype(BF16)
    z = jnp.dot(h, win_ref[...], preferred_element_type=F32)
    cq = _rms(z[:, _Z_CQ:_Z_CKV], cqn_ref[...]).astype(BF16)
    ckv = _rms(z[:, _Z_CKV:_Z_KR], ckvn_ref[...]).astype(BF16)

    ang = invf_ref[...] * pos_ref[0].astype(F32)
    cos_t, sin_t = jnp.cos(ang), jnp.sin(ang)
    zpad = jnp.zeros((LANES - MLA_ROPE, ang.shape[1]), F32)
    cc_t = jnp.concatenate([cos_t, cos_t, zpad], axis=0)
    ss_t = jnp.concatenate([-sin_t, sin_t, zpad], axis=0)

    r = lax.dot_general(wq_ref[...], cq, (((1,), (1,)), ((), ())), preferred_element_type=F32)
    half = MLA_ROPE // 2
    qzero = jnp.zeros((MLA_QK_PAD - MLA_NOPE - MLA_ROPE, r.shape[1]), BF16)
    for hd in range(MLA_HEADS):
        b = hd * (MLA_NOPE + MLA_ROPE)
        qt_ref[0, hd, 0:MLA_NOPE, :] = (r[b:b + MLA_NOPE] * q_scale).astype(BF16)
        t = r[b + MLA_NOPE:b + MLA_NOPE + MLA_ROPE]
        partner = jnp.concatenate([t[half:], t[:half]], axis=0)
        roped = t * cc_t[0:MLA_ROPE] + partner * ss_t[0:MLA_ROPE]
        qt_ref[0, hd, MLA_NOPE:MLA_NOPE + MLA_ROPE, :] = (roped * q_scale).astype(BF16)
        qt_ref[0, hd, MLA_NOPE + MLA_ROPE:MLA_QK_PAD, :] = qzero

    knope = jnp.dot(ckv, wkn_ref[...], preferred_element_type=F32)
    cc, ss = cc_t.T, ss_t.T
    kr = z[:, _Z_KR:_Z_SQ]
    lane = lax.broadcasted_iota(jnp.int32, kr.shape, 1)
    partner = jnp.where(lane < MLA_ROPE // 2, pltpu.roll(kr, LANES - MLA_ROPE // 2, 1),
                        pltpu.roll(kr, MLA_ROPE // 2, 1))
    krope = (kr * cc + partner * ss).astype(BF16)
    for hd in range(MLA_HEADS):
        k_ref[0, hd, :, 0:LANES] = knope[:, hd * LANES:(hd + 1) * LANES].astype(BF16)
        k_ref[0, hd, :, LANES:MLA_QK_PAD] = krope

    vt = lax.dot_general(wvt_ref[...], ckv, (((1,), (1,)), ((), ())), preferred_element_type=F32)
    ones_row = (lax.broadcasted_iota(jnp.int32, (BF16_SUBLANES, vt.shape[1]), 0) == 0).astype(BF16)
    for hd in range(MLA_HEADS):
        vt_ref[0, hd, 0:MLA_V, :] = vt[hd * MLA_V:(hd + 1) * MLA_V].astype(BF16)
        vt_ref[0, hd, MLA_V:MLA_V_EXT, :] = ones_row

    sq_ref[0] = (z[:, _Z_SQ:_Z_SK] * swa_scale).astype(BF16)
    sk_ref[0] = z[:, _Z_SK:_Z_SV].astype(BF16)
    sv_ref[0] = z[:, _Z_SV:_Z_END].astype(BF16)


def _in_proj(x, pos_row, invf, g, win, cqn, ckvn, wq, wkn, wvt, *, tm):
    B, S, D = x.shape
    grid = (B, S // tm)
    q_scale = LOG2E / math.sqrt(MLA_NOPE + MLA_ROPE)
    swa_scale = 1.0 / math.sqrt(HEAD_DIM)
    out_shape = (
        jax.ShapeDtypeStruct((B, MLA_HEADS, MLA_QK_PAD, S), BF16),
        jax.ShapeDtypeStruct((B, MLA_HEADS, S, MLA_QK_PAD), BF16),
        jax.ShapeDtypeStruct((B, MLA_HEADS, MLA_V_EXT, S), BF16),
        jax.ShapeDtypeStruct((B, S, SWA_HEADS * HEAD_DIM), BF16),
        jax.ShapeDtypeStruct((B, S, SWA_KV_HEADS * HEAD_DIM), BF16),
        jax.ShapeDtypeStruct((B, S, SWA_KV_HEADS * HEAD_DIM), BF16),
    )
    in_specs = [
        pl.BlockSpec((1, tm, D), lambda b, i: (b, i, 0)),
        pl.BlockSpec((1, 1, tm), lambda b, i: (b, 0, i)),
        _const_spec(invf.shape), _const_spec(g.shape), _const_spec(win.shape),
        _const_spec(cqn.shape), _const_spec(ckvn.shape), _const_spec(wq.shape),
        _const_spec(wkn.shape), _const_spec(wvt.shape),
    ]
    out_specs = (
        pl.BlockSpec((1, MLA_HEADS, MLA_QK_PAD, tm), lambda b, i: (b, 0, 0, i)),
        pl.BlockSpec((1, MLA_HEADS, tm, MLA_QK_PAD), lambda b, i: (b, 0, i, 0)),
        pl.BlockSpec((1, MLA_HEADS, MLA_V_EXT, tm), lambda b, i: (b, 0, 0, i)),
        pl.BlockSpec((1, tm, SWA_HEADS * HEAD_DIM), lambda b, i: (b, i, 0)),
        pl.BlockSpec((1, tm, SWA_KV_HEADS * HEAD_DIM), lambda b, i: (b, i, 0)),
        pl.BlockSpec((1, tm, SWA_KV_HEADS * HEAD_DIM), lambda b, i: (b, i, 0)),
    )
    return pl.pallas_call(
        functools.partial(_in_proj_kernel, q_scale=q_scale, swa_scale=swa_scale),
        grid=grid, in_specs=in_specs, out_specs=out_specs, out_shape=out_shape,
        compiler_params=_params(("parallel", "parallel")), name="in_proj",
    )(x, pos_row, invf, g, win, cqn, ckvn, wq, wkn, wvt)


def _mla_kernel(qt_ref, qn_ref, k_ref, vt_ref, o_ref, m_ref, acc_ref, s_ref, mc_ref, *, tq, tk, unroll, streams):
    S = k_ref.shape[2]
    n = S // tk
    trips = n // unroll

    def scores(c, j, slot, from_next=None):
        start = pl.multiple_of(c * tk, tk)
        q = qt_ref[0, 0, :, j * tq:(j + 1) * tq]
        if from_next is not None:
            q = jnp.where(from_next, qn_ref[0, 0, :, j * tq:(j + 1) * tq], q)
        s = jnp.dot(k_ref[0, 0, pl.ds(start, tk), :], q, preferred_element_type=F32)
        s_ref[2 * j + slot] = s
        mc_ref[2 * j + slot] = jnp.max(s, axis=0, keepdims=True)

    def softmax_pv(c, j, slot):
        start = pl.multiple_of(c * tk, tk)
        m_prev = m_ref[j]
        m_new = jnp.maximum(m_prev, mc_ref[2 * j + slot])
        alpha = jnp.exp2(m_prev - m_new)
        p = jnp.exp2(s_ref[2 * j + slot] - m_new).astype(BF16)
        vc = vt_ref[0, 0, :, pl.ds(start, tk)]
        acc_ref[j] = alpha * acc_ref[j] + jnp.dot(vc, p, preferred_element_type=F32)
        m_ref[j] = m_new

    @pl.when(pl.program_id(2) == 0)
    def _():
        for j in range(streams):
            scores(0, j, 0)

    for j in range(streams):
        m_ref[j] = jnp.full(m_ref.shape[1:], NEG, F32)
        acc_ref[j] = jnp.zeros(acc_ref.shape[1:], F32)

    def group(i, carry):
        c0 = i * unroll
        last = i == trips - 1
        for u in range(unroll):
            for j in range(streams):
                if u == unroll - 1:
                    scores(jnp.where(last, 0, c0 + u + 1), j, (u + 1) % 2, from_next=last)
                else:
                    scores(c0 + u + 1, j, (u + 1) % 2)
                softmax_pv(c0 + u, j, u % 2)
        return carry

    lax.fori_loop(0, trips, group, 0)
    for j in range(streams):
        acc = acc_ref[j]
        o_ref[0, j * tq:(j + 1) * tq, :] = (acc[0:MLA_V] / acc[MLA_V:MLA_V + 1]).T


def _mla_attn(qt, k, vt, *, tq, tk, streams):
    B, H, _, S = qt.shape
    vrows = vt.shape[2]
    n = S // tk
    unroll = max(u for u in (2, 4, 8, 16) if n % u == 0 and n // u >= 2)
    tqs = tq * streams
    steps = S // tqs
    return pl.pallas_call(
        functools.partial(_mla_kernel, tq=tq, tk=tk, unroll=unroll, streams=streams),
        grid=(B, H, steps),
        in_specs=[
            pl.BlockSpec((1, 1, MLA_QK_PAD, tqs), lambda b, h, i: (b, h, 0, i)),
            pl.BlockSpec((1, 1, MLA_QK_PAD, tqs), lambda b, h, i: (b, h, 0, jnp.minimum(i + 1, steps - 1))),
            pl.BlockSpec((1, 1, S, MLA_QK_PAD), lambda b, h, i: (b, h, 0, 0)),
            pl.BlockSpec((1, 1, vrows, S), lambda b, h, i: (b, h, 0, 0)),
        ],
        out_specs=pl.BlockSpec((1, tqs, MLA_V), lambda b, h, i: (b, i, h)),
        out_shape=jax.ShapeDtypeStruct((B, S, H * MLA_V), F32),
        scratch_shapes=[pltpu.VMEM((streams, 1, tq), F32), pltpu.VMEM((streams, vrows, tq), F32),
                        pltpu.VMEM((2 * streams, tk, tq), F32), pltpu.VMEM((2 * streams, 1, tq), F32)],
        compiler_params=_params(("arbitrary", "arbitrary", "arbitrary")), name="mla_attn",
    )(qt, qt, k, vt)


_T5_STEPS = (12, 16, 23, 32, 46, 64, 91, 128)


def _swa_kernel(t5_ref, sink_ref, q_ref, kp_ref, kc_ref, kn_ref, vp_ref, vc_ref, vn_ref,
                pq_ref, pp_ref, pc_ref, pn_ref, rebuild_ref, o_ref, bias_ref, *, nb):
    b = pl.program_id(0)
    n = pl.program_id(1)

    def blocks_of(p_ref, c_ref, n_ref, lo, hi):
        return [p_ref[0, :, lo:hi], c_ref[0, 0:BLOCK, lo:hi], c_ref[0, BLOCK:2 * BLOCK, lo:hi], n_ref[0, :, lo:hi]]

    def rebuild(a):
        m = 2 * n + a
        pq = pq_ref[0, :, a * BLOCK:(a + 1) * BLOCK]
        pk = jnp.concatenate(blocks_of(pp_ref, pc_ref, pn_ref, 0, 1)[a:a + 3], axis=0)
        rel = pk - pq
        row = lax.broadcasted_iota(jnp.int32, rel.shape, 0)
        lo = jnp.where(m > 0, 0, BLOCK)
        hi = jnp.where(m < nb - 1, 3 * BLOCK, 2 * BLOCK)
        na = jnp.abs(rel)
        mask = (na <= WINDOW) & (row >= lo) & (row < hi)
        large = jnp.full(na.shape, NUM_BUCKETS // 4, jnp.int32)
        for t in _T5_STEPS:
            large = large + (na >= t).astype(jnp.int32)
        large = jnp.minimum(large, NUM_BUCKETS // 2 - 1)
        bkt = jnp.where(rel > 0, NUM_BUCKETS // 2, 0) + jnp.where(na < NUM_BUCKETS // 4, na, large)
        bkts = [bkt[j * BLOCK:(j + 1) * BLOCK] for j in range(3)]
        for hd in range(SWA_HEADS):
            tab = jnp.broadcast_to(t5_ref[hd:hd + 1, :], (BLOCK, LANES))
            bias = jnp.concatenate([jnp.take_along_axis(tab, bk, axis=1) for bk in bkts], axis=0)
            g, j = divmod(hd, SWA_GROUP)
            bias_ref[g, :, j * BLOCK:(j + 1) * BLOCK] = jnp.where(mask, bias, NEG)

    def transpose_bf16(x):
        return x.astype(F32).T.astype(BF16)

    def attend(a):
        q = q_ref[0, a * BLOCK:(a + 1) * BLOCK, :]
        for g in range(SWA_KV_HEADS):
            lo, hi = g * HEAD_DIM, (g + 1) * HEAD_DIM
            k3 = jnp.concatenate(blocks_of(kp_ref, kc_ref, kn_ref, lo, hi)[a:a + 3], axis=0)
            v3t = jnp.concatenate(
                [transpose_bf16(v) for v in blocks_of(vp_ref, vc_ref, vn_ref, lo, hi)[a:a + 3]], axis=1)
            heads = range(g * SWA_GROUP, (g + 1) * SWA_GROUP)
            qt = jnp.concatenate(
                [transpose_bf16(q[:, hd * HEAD_DIM:(hd + 1) * HEAD_DIM]) for hd in heads], axis=1)
            s = jnp.dot(k3, qt, preferred_element_type=F32) + bias_ref[g]
            sk = jnp.concatenate([jnp.full((1, BLOCK), sink_ref[0, hd], F32) for hd in heads], axis=1)
            m = jnp.maximum(jnp.max(s, axis=0, keepdims=True), sk)
            e = jnp.exp(s - m)
            denom = jnp.sum(e, axis=0, keepdims=True) + jnp.exp(sk - m)
            ot = jnp.dot(v3t, e.astype(BF16), preferred_element_type=F32) / denom
            for j, hd in enumerate(heads):
                o_ref[0, a * BLOCK:(a + 1) * BLOCK, hd * HEAD_DIM:(hd + 1) * HEAD_DIM] = (
                    ot[:, j * BLOCK:(j + 1) * BLOCK].T)

    f0 = rebuild_ref[b, 2 * n]
    f1 = rebuild_ref[b, 2 * n + 1]

    @pl.when((f0 | f1) != 0)
    def _():
        pl.when(f0 != 0)(lambda: rebuild(0))
        attend(0)
        pl.when(f1 != 0)(lambda: rebuild(1))
        attend(1)

    @pl.when((f0 | f1) == 0)
    def _():
        attend(0)
        attend(1)


def _swa_rebuild_flags(positions):
    B, S = positions.shape
    nb = S // BLOCK
    flags = jnp.ones((B, nb), jnp.int32)
    if nb < 4:
        return flags
    step = (positions[:, BLOCK:] - positions[:, :-BLOCK]).reshape(B, nb - 1, BLOCK)
    lo, hi = step.min(axis=-1), step.max(axis=-1)
    lo3 = jnp.minimum(jnp.minimum(lo[:, :-2], lo[:, 1:-1]), lo[:, 2:])
    hi3 = jnp.maximum(jnp.maximum(hi[:, :-2], hi[:, 1:-1]), hi[:, 2:])
    return flags.at[:, 2:nb - 1].set((lo3 != hi3).astype(jnp.int32))


def _swa_attn(t5, sink, sq, sk, sv, pos_col, pos_row, rebuild):
    B, S, _ = sq.shape
    nb = S // BLOCK
    assert nb % 2 == 0
    kvw = SWA_KV_HEADS * HEAD_DIM
    prev = lambda b, n: (b, jnp.maximum(2 * n - 1, 0), 0)
    cur = lambda b, n: (b, n, 0)
    nxt = lambda b, n: (b, jnp.minimum(2 * n + 2, nb - 1), 0)
    rcur = lambda b, n: (b, 0, n)
    smem = pl.BlockSpec(memory_space=pltpu.SMEM)

    def key_specs(width):
        return [pl.BlockSpec((1, BLOCK, width), prev), pl.BlockSpec((1, 2 * BLOCK, width), cur),
                pl.BlockSpec((1, BLOCK, width), nxt)]

    return pl.pallas_call(
        functools.partial(_swa_kernel, nb=nb),
        grid=(B, nb // 2),
        in_specs=[_const_spec(t5.shape), smem, pl.BlockSpec((1, 2 * BLOCK, SWA_HEADS * HEAD_DIM), cur)]
        + key_specs(kvw) + key_specs(kvw) + [pl.BlockSpec((1, 1, 2 * BLOCK), rcur)] + key_specs(1) + [smem],
        out_specs=pl.BlockSpec((1, 2 * BLOCK, SWA_HEADS * HEAD_DIM), cur),
        out_shape=jax.ShapeDtypeStruct((B, S, SWA_HEADS * HEAD_DIM), F32),
        scratch_shapes=[pltpu.VMEM((SWA_KV_HEADS, 3 * BLOCK, SWA_GROUP * BLOCK), F32)],
        compiler_params=_params(("arbitrary", "arbitrary")), name="swa_attn",
    )(t5, sink, sq, sk, sk, sk, sv, sv, sv, pos_row, pos_col, pos_col, pos_col, rebuild)


def _out_proj_kernel(x_ref, mla_ref, swa_ref, g1_ref, g2_ref, wo_ref, gf_ref, xo_ref, h2_ref):
    w = wo_ref.shape[0] // 2
    m1 = _rms(mla_ref[...], g1_ref[...]).astype(BF16)
    m2 = _rms(swa_ref[...], g2_ref[...]).astype(BF16)
    y = jnp.dot(m1, wo_ref[0:w, :], preferred_element_type=F32)
    y = y + jnp.dot(m2, wo_ref[w:2 * w, :], preferred_element_type=F32)
    xn = x_ref[...] + y
    xo_ref[...] = xn
    h2_ref[...] = _rms(xn, gf_ref[...]).astype(BF16)


def _out_proj(x, mla_o, swa_o, g1, g2, wo, gf, *, tm):
    T, D = x.shape
    W = mla_o.shape[1]
    row = lambda i: (i, 0)
    return pl.pallas_call(
        _out_proj_kernel,
        grid=(T // tm,),
        in_specs=[
            pl.BlockSpec((tm, D), row), pl.BlockSpec((tm, W), row), pl.BlockSpec((tm, W), row),
            _const_spec(g1.shape), _const_spec(g2.shape), _const_spec(wo.shape), _const_spec(gf.shape),
        ],
        out_specs=(pl.BlockSpec((tm, D), row), pl.BlockSpec((tm, D), row)),
        out_shape=(jax.ShapeDtypeStruct((T, D), F32), jax.ShapeDtypeStruct((T, D), BF16)),
        compiler_params=_params(("parallel",)), name="out_proj",
    )(x, mla_o, swa_o, g1, g2, wo, gf)


def _ffn_kernel(hp_ref, h_ref, hn_ref, wg_ref, wu_ref, cw_ref, cb_ref, wd_ref, y_ref, hext_ref, *,
                tm, tiles_per_seq):
    i = pl.program_id(0)
    j = pl.program_id(1)
    halo = BF16_SUBLANES

    @pl.when(j == 0)
    def _():
        t = i % tiles_per_seq
        hext_ref[0:halo, :] = jnp.where(t == 0, jnp.zeros_like(hp_ref[...]), hp_ref[...])
        hext_ref[halo:halo + tm, :] = h_ref[...]
        hext_ref[halo + tm:, :] = jnp.where(t == tiles_per_seq - 1, jnp.zeros_like(hn_ref[...]), hn_ref[...])
        y_ref[...] = jnp.zeros_like(y_ref)

    gp = jnp.dot(hext_ref[...], wg_ref[...], preferred_element_type=F32)
    cw = cw_ref[...]
    g = (gp[halo - 1:halo - 1 + tm] * cw[0:1] + gp[halo:halo + tm] * cw[1:2]
         + gp[halo + 1:halo + 1 + tm] * cw[2:3] + cb_ref[...])
    u = jnp.dot(h_ref[...], wu_ref[...], preferred_element_type=F32)
    a = (g * _sigmoid(g) * u).astype(BF16)
    y_ref[...] += jnp.dot(a, wd_ref[...], preferred_element_type=F32)


def _ffn(h2, wg, wu, cw, cb, wd, *, layer, tm, tf, seq):
    T, D = h2.shape
    Fp = wg.shape[2]
    halo = BF16_SUBLANES
    hb = tm // halo
    nhb = T // halo
    return pl.pallas_call(
        functools.partial(_ffn_kernel, tm=tm, tiles_per_seq=seq // tm),
        grid=(T // tm, Fp // tf),
        in_specs=[
            pl.BlockSpec((halo, D), lambda i, j: (jnp.maximum(i * hb - 1, 0), 0)),
            pl.BlockSpec((tm, D), lambda i, j: (i, 0)),
            pl.BlockSpec((halo, D), lambda i, j: (jnp.minimum((i + 1) * hb, nhb - 1), 0)),
            pl.BlockSpec((None, D, tf), lambda i, j: (layer, 0, j)),
            pl.BlockSpec((None, D, tf), lambda i, j: (layer, 0, j)),
            pl.BlockSpec((None, 3, tf), lambda i, j: (layer, 0, j)),
            pl.BlockSpec((None, 1, tf), lambda i, j: (layer, 0, j)),
            pl.BlockSpec((None, tf, D), lambda i, j: (layer, j, 0)),
        ],
        out_specs=pl.BlockSpec((tm, D), lambda i, j: (i, 0)),
        out_shape=jax.ShapeDtypeStruct((T, D), F32),
        scratch_shapes=[pltpu.VMEM((tm + 2 * halo, D), BF16)],
        compiler_params=_params(("parallel", "arbitrary")), name="ffn",
    )(h2, h2, h2, wg, wu, cw, cb, wd)


def _ple_kernel(x_ref, y_ref, p_ref, wpg_ref, bpg_ref, wpp_ref, gfin_ref, o_ref, *, final):
    x2 = x_ref[...] + y_ref[...]
    gate = _sigmoid(jnp.dot(x2.astype(BF16), wpg_ref[...], preferred_element_type=F32) + bpg_ref[...])
    pp = jnp.dot(p_ref[...].astype(BF16), wpp_ref[...], preferred_element_type=F32)
    x3 = x2 + gate * pp
    if final:
        x3 = _rms(x3, gfin_ref[...])
    o_ref[...] = x3


def _ple(x, y, p, wpg, bpg, wpp, gfin, *, tm, final):
    T, D = x.shape
    P = p.shape[1]
    row = lambda i: (i, 0)
    return pl.pallas_call(
        functools.partial(_ple_kernel, final=final),
        grid=(T // tm,),
        in_specs=[
            pl.BlockSpec((tm, D), row), pl.BlockSpec((tm, D), row), pl.BlockSpec((tm, P), row),
            _const_spec(wpg.shape), _const_spec(bpg.shape), _const_spec(wpp.shape), _const_spec(gfin.shape),
        ],
        out_specs=pl.BlockSpec((tm, D), row),
        out_shape=jax.ShapeDtypeStruct((T, D), F32),
        compiler_params=_params(("parallel",)), name="ple",
    )(x, y, p, wpg, bpg, wpp, gfin)


def _prep_w_in(w_in):
    kr1 = Q_RANK + KV_RANK + MLA_ROPE
    zpad = jnp.zeros((w_in.shape[0], LANES - MLA_ROPE), w_in.dtype)
    return jnp.concatenate([w_in[:, :kr1], zpad, w_in[:, kr1:]], axis=1).astype(BF16)


def _prep_w_uq(w_uq):
    return w_uq.T.astype(BF16)


def _prep_w_ukv(w_ukv):
    w = w_ukv.reshape(KV_RANK, MLA_HEADS, MLA_NOPE + MLA_V)
    wkn = w[..., :MLA_NOPE].reshape(KV_RANK, MLA_HEADS * MLA_NOPE).astype(BF16)
    wvt = w[..., MLA_NOPE:].reshape(KV_RANK, MLA_HEADS * MLA_V).T.astype(BF16)
    return wkn, wvt


def _pad_ff(w, axis, fp):
    pad = [(0, 0)] * w.ndim
    pad[axis] = (0, fp - w.shape[axis])
    return jnp.pad(w, pad)


def _pick_tile(n, pref):
    t = min(pref, n)
    while n % t:
        t //= 2
    return t


def kernel(x, p, positions, attn_norm, w_in, cq_norm, ckv_norm, w_uq, w_ukv, swa_sink, t5_bias,
           mla_out_norm, swa_out_norm, w_o, ffn_norm, w_gate, w_up, conv_w, conv_b, w_down,
           ple_gate_w, ple_gate_b, ple_proj, final_norm):
    B, S, D = x.shape
    depth = w_in.shape[0]
    T = B * S
    d_ff = w_gate.shape[-1]
    tf = 512
    fp = -(-d_ff // tf) * tf

    tm_in = _pick_tile(S, 512)
    tq = _pick_tile(S, 512)
    tk = _pick_tile(S, 512)
    tm_out = _pick_tile(S, 512)
    tm_ffn = _pick_tile(S, 1024)
    tm_ple = _pick_tile(S, 512)

    pos_row = positions.reshape(B, 1, S)
    pos_col = positions.reshape(B, S, 1)
    invf = (ROPE_THETA ** (-jnp.arange(0, MLA_ROPE, 2, dtype=F32) / MLA_ROPE)).reshape(MLA_ROPE // 2, 1)
    row = lambda v: v.reshape(1, -1)
    t5_tab = jnp.pad(t5_bias.T, ((0, 0), (0, LANES - NUM_BUCKETS)))
    swa_rebuild = _swa_rebuild_flags(positions)
    wg_all = _pad_ff(w_gate.astype(BF16), 2, fp)
    wu_all = _pad_ff(w_up.astype(BF16), 2, fp)
    cw_all = _pad_ff(conv_w, 2, fp)
    cb_all = _pad_ff(conv_b[:, None, :], 2, fp)
    wd_all = _pad_ff(w_down.astype(BF16), 1, fp)

    xf = x
    for i in range(depth):
        wkn, wvt = _prep_w_ukv(w_ukv[i])
        qt, k, vt, sq, sk, sv = _in_proj(
            xf.reshape(B, S, D), pos_row, invf, row(attn_norm[i]), _prep_w_in(w_in[i]),
            row(cq_norm[i]), row(ckv_norm[i]), _prep_w_uq(w_uq[i]), wkn, wvt, tm=tm_in)
        mla_o = _mla_attn(qt, k, vt, tq=tq, tk=tk, streams=2 if S % (2 * tq) == 0 else 1)
        swa_o = _swa_attn(t5_tab, row(swa_sink[i]), sq, sk, sv, pos_col, pos_row, swa_rebuild)
        x1, h2 = _out_proj(
            xf.reshape(T, D), mla_o.reshape(T, -1), swa_o.reshape(T, -1), row(mla_out_norm[i]),
            row(swa_out_norm[i]), w_o[i].astype(BF16), row(ffn_norm[i]), tm=tm_out)
        y = _ffn(h2, wg_all, wu_all, cw_all, cb_all, wd_all, layer=i, tm=tm_ffn, tf=tf, seq=S)
        xf = _ple(
            x1, y, p[i].reshape(T, -1), ple_gate_w[i].astype(BF16), row(ple_gate_b[i]),
            ple_proj[i].astype(BF16), row(final_norm), tm=tm_ple, final=(i == depth - 1))
    return xf.reshape(B, S, D)
```

```python
import functools
import math

import jax
import jax.numpy as jnp
from jax import lax
from jax.experimental import pallas as pl
from jax.experimental.pallas import tpu as pltpu

F32 = jnp.float32
BF16 = jnp.bfloat16

EPS = 1e-6
BLOCK = 128
MLA_HEADS = 8
MLA_NOPE = 128
MLA_ROPE = 64
MLA_V = 128
MLA_QK_PAD = 256
Q_RANK = 384
KV_RANK = 256
ROPE_THETA = 10000.0
SWA_HEADS = 8
SWA_KV_HEADS = 2
SWA_GROUP = SWA_HEADS // SWA_KV_HEADS
HEAD_DIM = 128
WINDOW = 128
NUM_BUCKETS = 32
SWA_STEP_BLOCKS = 4
NEG = -1e30
LOG2E = math.log2(math.e)

V7X_VMEM_BYTES = 64 * 1024 * 1024
VMEM_LIMIT_BYTES = V7X_VMEM_BYTES - 8 * 1024 * 1024
LANES = 128
BF16_SUBLANES = 16
MLA_V_EXT = MLA_V + BF16_SUBLANES

_Z_CQ = 0
_Z_CKV = _Z_CQ + Q_RANK
_Z_KR = _Z_CKV + KV_RANK
_Z_SQ = _Z_KR + LANES
_Z_SK = _Z_SQ + SWA_HEADS * HEAD_DIM
_Z_SV = _Z_SK + SWA_KV_HEADS * HEAD_DIM
_Z_END = _Z_SV + SWA_KV_HEADS * HEAD_DIM


def _params(sem):
    return pltpu.CompilerParams(dimension_semantics=sem, vmem_limit_bytes=VMEM_LIMIT_BYTES)


def _const_spec(shape):
    nd = len(shape)
    return pl.BlockSpec(shape, lambda *_: (0,) * nd, pipeline_mode=pl.Buffered(1))


def _rms(x, g):
    ms = jnp.mean(x * x, axis=-1, keepdims=True)
    return x * lax.rsqrt(ms + EPS) * g


def _sigmoid(x):
    return 1.0 / (1.0 + jnp.exp(-x))


def _in_proj_kernel(x_ref, pos_ref, invf_ref, g_ref, win_ref, cqn_ref, ckvn_ref, wq_ref, wkn_ref,
                    wvt_ref, qt_ref, k_ref, vt_ref, sq_ref, sk_ref, sv_ref, *, q_scale, swa_scale):
    x = x_ref[0]
    h = _rms(x, g_ref[...]).astype(BF16)
    z = jnp.dot(h, win_ref[...], preferred_element_type=F32)
    cq = _rms(z[:, _Z_CQ:_Z_CKV], cqn_ref[...]).astype(BF16)
    ckv = _rms(z[:, _Z_CKV:_Z_KR], ckvn_ref[...]).astype(BF16)

    ang = invf_ref[...] * pos_ref[0].astype(F32)
    cos_t, sin_t = jnp.cos(ang), jnp.sin(ang)
    zpad = jnp.zeros((LANES - MLA_ROPE, ang.shape[1]), F32)
    cc_t = jnp.concatenate([cos_t, cos_t, zpad], axis=0)
    ss_t = jnp.concatenate([-sin_t, sin_t, zpad], axis=0)

    r = lax.dot_general(wq_ref[...], cq, (((1,), (1,)), ((), ())), preferred_element_type=F32)
    half = MLA_ROPE // 2
    qzero = jnp.zeros((MLA_QK_PAD - MLA_NOPE - MLA_ROPE, r.shape[1]), BF16)
    for hd in range(MLA_HEADS):
        b = hd * (MLA_NOPE + MLA_ROPE)
        qt_ref[0, hd, 0:MLA_NOPE, :] = (r[b:b + MLA_NOPE] * q_scale).astype(BF16)
        t = r[b + MLA_NOPE:b + MLA_NOPE + MLA_ROPE]
        partner = jnp.concatenate([t[half:], t[:half]], axis=0)
        roped = t * cc_t[0:MLA_ROPE] + partner * ss_t[0:MLA_ROPE]
        qt_ref[0, hd, MLA_NOPE:MLA_NOPE + MLA_ROPE, :] = (roped * q_scale).astype(BF16)
        qt_ref[0, hd, MLA_NOPE + MLA_ROPE:MLA_QK_PAD, :] = qzero

    knope = jnp.dot(ckv, wkn_ref[...], preferred_element_type=F32)
    cc, ss = cc_t.T, ss_t.T
    kr = z[:, _Z_KR:_Z_SQ]
    lane = lax.broadcasted_iota(jnp.int32, kr.shape, 1)
    partner = jnp.where(lane < MLA_ROPE // 2, pltpu.roll(kr, LANES - MLA_ROPE // 2, 1),
                        pltpu.roll(kr, MLA_ROPE // 2, 1))
    krope = (kr * cc + partner * ss).astype(BF16)
    for hd in range(MLA_HEADS):
        k_ref[0, hd, :, 0:LANES] = knope[:, hd * LANES:(hd + 1) * LANES].astype(BF16)
        k_ref[0, hd, :, LANES:MLA_QK_PAD] = krope

    vt = lax.dot_general(wvt_ref[...], ckv, (((1,), (1,)), ((), ())), preferred_element_type=F32)
    ones_row = (lax.broadcasted_iota(jnp.int32, (BF16_SUBLANES, vt.shape[1]), 0) == 0).astype(BF16)
    for hd in range(MLA_HEADS):
        vt_ref[0, hd, 0:MLA_V, :] = vt[hd * MLA_V:(hd + 1) * MLA_V].astype(BF16)
        vt_ref[0, hd, MLA_V:MLA_V_EXT, :] = ones_row

    sq_ref[0] = (z[:, _Z_SQ:_Z_SK] * swa_scale).astype(BF16)
    sk_ref[0] = z[:, _Z_SK:_Z_SV].astype(BF16)
    sv_ref[0] = z[:, _Z_SV:_Z_END].astype(BF16)


def _in_proj(x, pos_row, invf, g, win, cqn, ckvn, wq, wkn, wvt, *, tm):
    B, S, D = x.shape
    grid = (B, S // tm)
    q_scale = LOG2E / math.sqrt(MLA_NOPE + MLA_ROPE)
    swa_scale = 1.0 / math.sqrt(HEAD_DIM)
    out_shape = (
        jax.ShapeDtypeStruct((B, MLA_HEADS, MLA_QK_PAD, S), BF16),
        jax.ShapeDtypeStruct((B, MLA_HEADS, S, MLA_QK_PAD), BF16),
        jax.ShapeDtypeStruct((B, MLA_HEADS, MLA_V_EXT, S), BF16),
        jax.ShapeDtypeStruct((B, S, SWA_HEADS * HEAD_DIM), BF16),
        jax.ShapeDtypeStruct((B, S, SWA_KV_HEADS * HEAD_DIM), BF16),
        jax.ShapeDtypeStruct((B, S, SWA_KV_HEADS * HEAD_DIM), BF16),
    )
    in_specs = [
        pl.BlockSpec((1, tm, D), lambda b, i: (b, i, 0)),
        pl.BlockSpec((1, 1, tm), lambda b, i: (b, 0, i)),
        _const_spec(invf.shape), _const_spec(g.shape), _const_spec(win.shape),
        _const_spec(cqn.shape), _const_spec(ckvn.shape), _const_spec(wq.shape),
        _const_spec(wkn.shape), _const_spec(wvt.shape),
    ]
    out_specs = (
        pl.BlockSpec((1, MLA_HEADS, MLA_QK_PAD, tm), lambda b, i: (b, 0, 0, i)),
        pl.BlockSpec((1, MLA_HEADS, tm, MLA_QK_PAD), lambda b, i: (b, 0, i, 0)),
        pl.BlockSpec((1, MLA_HEADS, MLA_V_EXT, tm), lambda b, i: (b, 0, 0, i)),
        pl.BlockSpec((1, tm, SWA_HEADS * HEAD_DIM), lambda b, i: (b, i, 0)),
        pl.BlockSpec((1, tm, SWA_KV_HEADS * HEAD_DIM), lambda b, i: (b, i, 0)),
        pl.BlockSpec((1, tm, SWA_KV_HEADS * HEAD_DIM), lambda b, i: (b, i, 0)),
    )
    return pl.pallas_call(
        functools.partial(_in_proj_kernel, q_scale=q_scale, swa_scale=swa_scale),
        grid=grid, in_specs=in_specs, out_specs=out_specs, out_shape=out_shape,
        compiler_params=_params(("parallel", "parallel")), name="in_proj",
    )(x, pos_row, invf, g, win, cqn, ckvn, wq, wkn, wvt)


def _mla_kernel(qt_ref, qn_ref, k_ref, vt_ref, o_ref, m_ref, acc_ref, s_ref, mc_ref, *, tq, tk, unroll, streams):
    S = k_ref.shape[2]
    n = S // tk
    trips = n // unroll

    def scores(c, j, slot, from_next=None):
        start = pl.multiple_of(c * tk, tk)
        q = qt_ref[0, 0, :, j * tq:(j + 1) * tq]
        if from_next is not None:
            q = jnp.where(from_next, qn_ref[0, 0, :, j * tq:(j + 1) * tq], q)
        s = jnp.dot(k_ref[0, 0, pl.ds(start, tk), :], q, preferred_element_type=F32)
        s_ref[2 * j + slot] = s
        mc_ref[2 * j + slot] = jnp.max(s, axis=0, keepdims=True)

    def softmax_pv(c, j, slot):
        start = pl.multiple_of(c * tk, tk)
        m_prev = m_ref[j]
        m_new = jnp.maximum(m_prev, mc_ref[2 * j + slot])
        alpha = jnp.exp2(m_prev - m_new)
        p = jnp.exp2(s_ref[2 * j + slot] - m_new).astype(BF16)
        vc = vt_ref[0, 0, :, pl.ds(start, tk)]
        acc_ref[j] = alpha * acc_ref[j] + jnp.dot(vc, p, preferred_element_type=F32)
        m_ref[j] = m_new

    @pl.when(pl.program_id(2) == 0)
    def _():
        for j in range(streams):
            scores(0, j, 0)

    for j in range(streams):
        m_ref[j] = jnp.full(m_ref.shape[1:], NEG, F32)
        acc_ref[j] = jnp.zeros(acc_ref.shape[1:], F32)

    def group(i, carry):
        c0 = i * unroll
        last = i == trips - 1
        for u in range(unroll):
            for j in range(streams):
                if u == unroll - 1:
                    scores(jnp.where(last, 0, c0 + u + 1), j, (u + 1) % 2, from_next=last)
                else:
                    scores(c0 + u + 1, j, (u + 1) % 2)
                softmax_pv(c0 + u, j, u % 2)
        return carry

    lax.fori_loop(0, trips, group, 0)
    for j in range(streams):
        acc = acc_ref[j]
        o_ref[0, j * tq:(j + 1) * tq, :] = (acc[0:MLA_V] / acc[MLA_V:MLA_V + 1]).T


def _mla_attn(qt, k, vt, *, tq, tk, streams):
    B, H, _, S = qt.shape
    vrows = vt.shape[2]
    n = S // tk
    unroll = max(u for u in (2, 4, 8, 16) if n % u == 0 and n // u >= 2)
    tqs = tq * streams
    steps = S // tqs
    return pl.pallas_call(
        functools.partial(_mla_kernel, tq=tq, tk=tk, unroll=unroll, streams=streams),
        grid=(B, H, steps),
        in_specs=[
            pl.BlockSpec((1, 1, MLA_QK_PAD, tqs), lambda b, h, i: (b, h, 0, i)),
            pl.BlockSpec((1, 1, MLA_QK_PAD, tqs), lambda b, h, i: (b, h, 0, jnp.minimum(i + 1, steps - 1))),
            pl.BlockSpec((1, 1, S, MLA_QK_PAD), lambda b, h, i: (b, h, 0, 0)),
            pl.BlockSpec((1, 1, vrows, S), lambda b, h, i: (b, h, 0, 0)),
        ],
        out_specs=pl.BlockSpec((1, tqs, MLA_V), lambda b, h, i: (b, i, h)),
        out_shape=jax.ShapeDtypeStruct((B, S, H * MLA_V), F32),
        scratch_shapes=[pltpu.VMEM((streams, 1, tq), F32), pltpu.VMEM((streams, vrows, tq), F32),
                        pltpu.VMEM((2 * streams, tk, tq), F32), pltpu.VMEM((2 * streams, 1, tq), F32)],
        compiler_params=_params(("arbitrary", "arbitrary", "arbitrary")), name="mla_attn",
    )(qt, qt, k, vt)


_T5_STEPS = (12, 16, 23, 32, 46, 64, 91, 128)


def _swa_kernel(t5_ref, sink_ref, q_ref, kp_ref, kc_ref, kn_ref, vp_ref, vc_ref, vn_ref,
                pq_ref, pp_ref, pc_ref, pn_ref, rebuild_ref, o_ref, bias_ref, *, nb):
    b = pl.program_id(0)
    n = pl.program_id(1)

    def blocks_of(p_ref, c_ref, n_ref, lo, hi):
        cur = [c_ref[0, i * BLOCK:(i + 1) * BLOCK, lo:hi] for i in range(SWA_STEP_BLOCKS)]
        return [p_ref[0, :, lo:hi]] + cur + [n_ref[0, :, lo:hi]]

    def rebuild(a):
        m = SWA_STEP_BLOCKS * n + a
        pq = pq_ref[0, :, a * BLOCK:(a + 1) * BLOCK]
        pk = jnp.concatenate(blocks_of(pp_ref, pc_ref, pn_ref, 0, 1)[a:a + 3], axis=0)
        rel = pk - pq
        row = lax.broadcasted_iota(jnp.int32, rel.shape, 0)
        lo = jnp.where(m > 0, 0, BLOCK)
        hi = jnp.where(m < nb - 1, 3 * BLOCK, 2 * BLOCK)
        na = jnp.abs(rel)
        mask = (na <= WINDOW) & (row >= lo) & (row < hi)
        large = jnp.full(na.shape, NUM_BUCKETS // 4, jnp.int32)
        for t in _T5_STEPS:
            large = large + (na >= t).astype(jnp.int32)
        large = jnp.minimum(large, NUM_BUCKETS // 2 - 1)
        bkt = jnp.where(rel > 0, NUM_BUCKETS // 2, 0) + jnp.where(na < NUM_BUCKETS // 4, na, large)
        bkts = [bkt[j * BLOCK:(j + 1) * BLOCK] for j in range(3)]
        for hd in range(SWA_HEADS):
            tab = jnp.broadcast_to(t5_ref[hd:hd + 1, :], (BLOCK, LANES))
            bias = jnp.concatenate([jnp.take_along_axis(tab, bk, axis=1) for bk in bkts], axis=0)
            g, j = divmod(hd, SWA_GROUP)
            bias_ref[g, :, j * BLOCK:(j + 1) * BLOCK] = jnp.where(mask, bias, NEG)

    def transpose_bf16(x):
        return x.astype(F32).T.astype(BF16)

    def attend(a):
        q = q_ref[0, a * BLOCK:(a + 1) * BLOCK, :]
        for g in range(SWA_KV_HEADS):
            lo, hi = g * HEAD_DIM, (g + 1) * HEAD_DIM
            k3 = jnp.concatenate(blocks_of(kp_ref, kc_ref, kn_ref, lo, hi)[a:a + 3], axis=0)
            v3t = jnp.concatenate(
                [transpose_bf16(v) for v in blocks_of(vp_ref, vc_ref, vn_ref, lo, hi)[a:a + 3]], axis=1)
            heads = range(g * SWA_GROUP, (g + 1) * SWA_GROUP)
            qt = jnp.concatenate(
                [transpose_bf16(q[:, hd * HEAD_DIM:(hd + 1) * HEAD_DIM]) for hd in heads], axis=1)
            s = jnp.dot(k3, qt, preferred_element_type=F32) + bias_ref[g]
            sk = jnp.concatenate([jnp.full((1, BLOCK), sink_ref[0, hd], F32) for hd in heads], axis=1)
            m = jnp.maximum(jnp.max(s, axis=0, keepdims=True), sk)
            e = jnp.exp(s - m)
            denom = jnp.sum(e, axis=0, keepdims=True) + jnp.exp(sk - m)
            ot = jnp.dot(v3t, e.astype(BF16), preferred_element_type=F32) / denom
            for j, hd in enumerate(heads):
                o_ref[0, a * BLOCK:(a + 1) * BLOCK, hd * HEAD_DIM:(hd + 1) * HEAD_DIM] = (
                    ot[:, j * BLOCK:(j + 1) * BLOCK].T)

    flags = [rebuild_ref[b, SWA_STEP_BLOCKS * n + a] for a in range(SWA_STEP_BLOCKS)]
    any_flag = functools.reduce(lambda x, y: x | y, flags)

    @pl.when(any_flag != 0)
    def _():
        for a in range(SWA_STEP_BLOCKS):
            pl.when(flags[a] != 0)(functools.partial(rebuild, a))
            attend(a)

    @pl.when(any_flag == 0)
    def _():
        for a in range(SWA_STEP_BLOCKS):
            attend(a)


def _swa_rebuild_flags(positions):
    B, S = positions.shape
    nb = S // BLOCK
    flags = jnp.ones((B, nb), jnp.int32)
    if nb < 4:
        return flags
    step = (positions[:, BLOCK:] - positions[:, :-BLOCK]).reshape(B, nb - 1, BLOCK)
    lo, hi = step.min(axis=-1), step.max(axis=-1)
    lo3 = jnp.minimum(jnp.minimum(lo[:, :-2], lo[:, 1:-1]), lo[:, 2:])
    hi3 = jnp.maximum(jnp.maximum(hi[:, :-2], hi[:, 1:-1]), hi[:, 2:])
    return flags.at[:, 2:nb - 1].set((lo3 != hi3).astype(jnp.int32))


def _swa_attn(t5, sink, sq, sk, sv, pos_col, pos_row, rebuild):
    B, S, _ = sq.shape
    nb = S // BLOCK
    r = SWA_STEP_BLOCKS
    assert nb % r == 0
    kvw = SWA_KV_HEADS * HEAD_DIM
    prev = lambda b, n: (b, jnp.maximum(r * n - 1, 0), 0)
    cur = lambda b, n: (b, n, 0)
    nxt = lambda b, n: (b, jnp.minimum(r * n + r, nb - 1), 0)
    rcur = lambda b, n: (b, 0, n)
    smem = pl.BlockSpec(memory_space=pltpu.SMEM)

    def key_specs(width):
        return [pl.BlockSpec((1, BLOCK, width), prev), pl.BlockSpec((1, r * BLOCK, width), cur),
                pl.BlockSpec((1, BLOCK, width), nxt)]

    return pl.pallas_call(
        functools.partial(_swa_kernel, nb=nb),
        grid=(B, nb // r),
        in_specs=[_const_spec(t5.shape), smem, pl.BlockSpec((1, r * BLOCK, SWA_HEADS * HEAD_DIM), cur)]
        + key_specs(kvw) + key_specs(kvw) + [pl.BlockSpec((1, 1, r * BLOCK), rcur)] + key_specs(1) + [smem],
        out_specs=pl.BlockSpec((1, r * BLOCK, SWA_HEADS * HEAD_DIM), cur),
        out_shape=jax.ShapeDtypeStruct((B, S, SWA_HEADS * HEAD_DIM), F32),
        scratch_shapes=[pltpu.VMEM((SWA_KV_HEADS, 3 * BLOCK, SWA_GROUP * BLOCK), F32)],
        compiler_params=_params(("arbitrary", "arbitrary")), name="swa_attn",
    )(t5, sink, sq, sk, sk, sk, sv, sv, sv, pos_row, pos_col, pos_col, pos_col, rebuild)


def _out_proj_kernel(x_ref, mla_ref, swa_ref, g1_ref, g2_ref, wo_ref, gf_ref, xo_ref, h2_ref):
    w = wo_ref.shape[0] // 2
    m1 = _rms(mla_ref[...], g1_ref[...]).astype(BF16)
    m2 = _rms(swa_ref[...], g2_ref[...]).astype(BF16)
    y = jnp.dot(m1, wo_ref[0:w, :], preferred_element_type=F32)
    y = y + jnp.dot(m2, wo_ref[w:2 * w, :], preferred_element_type=F32)
    xn = x_ref[...] + y
    xo_ref[...] = xn
    h2_ref[...] = _rms(xn, gf_ref[...]).astype(BF16)


def _out_proj(x, mla_o, swa_o, g1, g2, wo, gf, *, tm):
    T, D = x.shape
    W = mla_o.shape[1]
    row = lambda i: (i, 0)
    return pl.pallas_call(
        _out_proj_kernel,
        grid=(T // tm,),
        in_specs=[
            pl.BlockSpec((tm, D), row), pl.BlockSpec((tm, W), row), pl.BlockSpec((tm, W), row),
            _const_spec(g1.shape), _const_spec(g2.shape), _const_spec(wo.shape), _const_spec(gf.shape),
        ],
        out_specs=(pl.BlockSpec((tm, D), row), pl.BlockSpec((tm, D), row)),
        out_shape=(jax.ShapeDtypeStruct((T, D), F32), jax.ShapeDtypeStruct((T, D), BF16)),
        compiler_params=_params(("parallel",)), name="out_proj",
    )(x, mla_o, swa_o, g1, g2, wo, gf)


def _ffn_kernel(hp_ref, h_ref, hn_ref, wg_ref, wu_ref, cw_ref, cb_ref, wd_ref, y_ref, hext_ref, *,
                tm, tiles_per_seq):
    i = pl.program_id(0)
    j = pl.program_id(1)
    halo = BF16_SUBLANES

    @pl.when(j == 0)
    def _():
        t = i % tiles_per_seq
        hext_ref[0:halo, :] = jnp.where(t == 0, jnp.zeros_like(hp_ref[...]), hp_ref[...])
        hext_ref[halo:halo + tm, :] = h_ref[...]
        hext_ref[halo + tm:, :] = jnp.where(t == tiles_per_seq - 1, jnp.zeros_like(hn_ref[...]), hn_ref[...])
        y_ref[...] = jnp.zeros_like(y_ref)

    gp = jnp.dot(hext_ref[...], wg_ref[...], preferred_element_type=F32)
    cw = cw_ref[...]
    g = (gp[halo - 1:halo - 1 + tm] * cw[0:1] + gp[halo:halo + tm] * cw[1:2]
         + gp[halo + 1:halo + 1 + tm] * cw[2:3] + cb_ref[...])
    u = jnp.dot(h_ref[...], wu_ref[...], preferred_element_type=F32)
    a = (g * _sigmoid(g) * u).astype(BF16)
    y_ref[...] += jnp.dot(a, wd_ref[...], preferred_element_type=F32)


def _ffn(h2, wg, wu, cw, cb, wd, *, layer, tm, tf, seq):
    T, D = h2.shape
    Fp = wg.shape[2]
    halo = BF16_SUBLANES
    hb = tm // halo
    nhb = T // halo
    return pl.pallas_call(
        functools.partial(_ffn_kernel, tm=tm, tiles_per_seq=seq // tm),
        grid=(T // tm, Fp // tf),
        in_specs=[
            pl.BlockSpec((halo, D), lambda i, j: (jnp.maximum(i * hb - 1, 0), 0)),
            pl.BlockSpec((tm, D), lambda i, j: (i, 0)),
            pl.BlockSpec((halo, D), lambda i, j: (jnp.minimum((i + 1) * hb, nhb - 1), 0)),
            pl.BlockSpec((None, D, tf), lambda i, j: (layer, 0, j)),
            pl.BlockSpec((None, D, tf), lambda i, j: (layer, 0, j)),
            pl.BlockSpec((None, 3, tf), lambda i, j: (layer, 0, j)),
            pl.BlockSpec((None, 1, tf), lambda i, j: (layer, 0, j)),
            pl.BlockSpec((None, tf, D), lambda i, j: (layer, j, 0)),
        ],
        out_specs=pl.BlockSpec((tm, D), lambda i, j: (i, 0)),
        out_shape=jax.ShapeDtypeStruct((T, D), F32),
        scratch_shapes=[pltpu.VMEM((tm + 2 * halo, D), BF16)],
        compiler_params=_params(("parallel", "arbitrary")), name="ffn",
    )(h2, h2, h2, wg, wu, cw, cb, wd)


def _ple_kernel(x_ref, y_ref, p_ref, wpg_ref, bpg_ref, wpp_ref, gfin_ref, o_ref, *, final):
    x2 = x_ref[...] + y_ref[...]
    gate = _sigmoid(jnp.dot(x2.astype(BF16), wpg_ref[...], preferred_element_type=F32) + bpg_ref[...])
    pp = jnp.dot(p_ref[...].astype(BF16), wpp_ref[...], preferred_element_type=F32)
    x3 = x2 + gate * pp
    if final:
        x3 = _rms(x3, gfin_ref[...])
    o_ref[...] = x3


def _ple(x, y, p, wpg, bpg, wpp, gfin, *, tm, final):
    T, D = x.shape
    P = p.shape[1]
    row = lambda i: (i, 0)
    return pl.pallas_call(
        functools.partial(_ple_kernel, final=final),
        grid=(T // tm,),
        in_specs=[
            pl.BlockSpec((tm, D), row), pl.BlockSpec((tm, D), row), pl.BlockSpec((tm, P), row),
            _const_spec(wpg.shape), _const_spec(bpg.shape), _const_spec(wpp.shape), _const_spec(gfin.shape),
        ],
        out_specs=pl.BlockSpec((tm, D), row),
        out_shape=jax.ShapeDtypeStruct((T, D), F32),
        compiler_params=_params(("parallel",)), name="ple",
    )(x, y, p, wpg, bpg, wpp, gfin)


def _prep_w_in(w_in):
    kr1 = Q_RANK + KV_RANK + MLA_ROPE
    zpad = jnp.zeros((w_in.shape[0], LANES - MLA_ROPE), w_in.dtype)
    return jnp.concatenate([w_in[:, :kr1], zpad, w_in[:, kr1:]], axis=1).astype(BF16)


def _prep_w_uq(w_uq):
    return w_uq.T.astype(BF16)


def _prep_w_ukv(w_ukv):
    w = w_ukv.reshape(KV_RANK, MLA_HEADS, MLA_NOPE + MLA_V)
    wkn = w[..., :MLA_NOPE].reshape(KV_RANK, MLA_HEADS * MLA_NOPE).astype(BF16)
    wvt = w[..., MLA_NOPE:].reshape(KV_RANK, MLA_HEADS * MLA_V).T.astype(BF16)
    return wkn, wvt


def _pad_ff(w, axis, fp):
    pad = [(0, 0)] * w.ndim
    pad[axis] = (0, fp - w.shape[axis])
    return jnp.pad(w, pad)


def _pick_tile(n, pref):
    t = min(pref, n)
    while n % t:
        t //= 2
    return t


def kernel(x, p, positions, attn_norm, w_in, cq_norm, ckv_norm, w_uq, w_ukv, swa_sink, t5_bias,
           mla_out_norm, swa_out_norm, w_o, ffn_norm, w_gate, w_up, conv_w, conv_b, w_down,
           ple_gate_w, ple_gate_b, ple_proj, final_norm):
    B, S, D = x.shape
    depth = w_in.shape[0]
    T = B * S
    d_ff = w_gate.shape[-1]
    tf = 512
    fp = -(-d_ff // tf) * tf

    tm_in = _pick_tile(S, 512)
    tq = _pick_tile(S, 512)
    tk = _pick_tile(S, 512)
    tm_out = _pick_tile(S, 512)
    tm_ffn = _pick_tile(S, 1024)
    tm_ple = _pick_tile(S, 512)

    pos_row = positions.reshape(B, 1, S)
    pos_col = positions.reshape(B, S, 1)
    invf = (ROPE_THETA ** (-jnp.arange(0, MLA_ROPE, 2, dtype=F32) / MLA_ROPE)).reshape(MLA_ROPE // 2, 1)
    row = lambda v: v.reshape(1, -1)
    t5_tab = jnp.pad(t5_bias.T, ((0, 0), (0, LANES - NUM_BUCKETS)))
    swa_rebuild = _swa_rebuild_flags(positions)
    wg_all = _pad_ff(w_gate.astype(BF16), 2, fp)
    wu_all = _pad_ff(w_up.astype(BF16), 2, fp)
    cw_all = _pad_ff(conv_w, 2, fp)
    cb_all = _pad_ff(conv_b[:, None, :], 2, fp)
    wd_all = _pad_ff(w_down.astype(BF16), 1, fp)

    xf = x
    for i in range(depth):
        wkn, wvt = _prep_w_ukv(w_ukv[i])
        qt, k, vt, sq, sk, sv = _in_proj(
            xf.reshape(B, S, D), pos_row, invf, row(attn_norm[i]), _prep_w_in(w_in[i]),
            row(cq_norm[i]), row(ckv_norm[i]), _prep_w_uq(w_uq[i]), wkn, wvt, tm=tm_in)
        mla_o = _mla_attn(qt, k, vt, tq=tq, tk=tk, streams=2 if S % (2 * tq) == 0 else 1)
        swa_o = _swa_attn(t5_tab, row(swa_sink[i]), sq, sk, sv, pos_col, pos_row, swa_rebuild)
        x1, h2 = _out_proj(
            xf.reshape(T, D), mla_o.reshape(T, -1), swa_o.reshape(T, -1), row(mla_out_norm[i]),
            row(swa_out_norm[i]), w_o[i].astype(BF16), row(ffn_norm[i]), tm=tm_out)
        y = _ffn(h2, wg_all, wu_all, cw_all, cb_all, wd_all, layer=i, tm=tm_ffn, tf=tf, seq=S)
        xf = _ple(
            x1, y, p[i].reshape(T, -1), ple_gate_w[i].astype(BF16), row(ple_gate_b[i]),
            ple_proj[i].astype(BF16), row(final_norm), tm=tm_ple, final=(i == depth - 1))
    return xf.reshape(B, S, D)
```
